```python
import jax, jax.numpy as jnp
from jax import lax
import numpy as np

D_MODEL = 1024
BATCH = 8
SEQ = 2048
DEPTH = 2

N_SB_HEADS = 8
SB_HEAD_DIM = 64
SB_WIDTH = N_SB_HEADS * SB_HEAD_DIM
Q_BLOCK = 128
POOL_WINDOWS = (2, 4, 8, 16)
POOL_GROUPS = len(POOL_WINDOWS)
POOL_GROUP_DIM = 128
POOL_WIDTH = POOL_GROUPS * POOL_GROUP_DIM
CONV_WIDTH = 512
CONV_TAPS = 31
SG_GROUPS = 4
SG_GROUP_DIM = 128
SG_WIDTH = SG_GROUPS * SG_GROUP_DIM
SG_CHUNK = 128
EVEN_IN = 3 * SB_WIDTH + POOL_WIDTH
ODD_IN = 2 * CONV_WIDTH + 2 * SG_WIDTH
MIX_WIDTH = 1024
D_FF = -(-(8 * D_MODEL) // (3 * 256)) * 256
PLE_DIM = 256
N_EVEN = (DEPTH + 1) // 2
N_ODD = DEPTH // 2
ALPHA = (2 * DEPTH) ** 0.25
BETA_INIT = (8 * DEPTH) ** -0.25
LN_EPS = 1e-5

kernel_name = "hybrid_stickbreak_pool_conformer_gmlp"


def layer_norm(x, g, b):
    xf = x.astype(jnp.float32)
    mu = jnp.mean(xf, axis=-1, keepdims=True)
    var = jnp.mean(jnp.square(xf - mu), axis=-1, keepdims=True)
    y = (xf - mu) * lax.rsqrt(var + LN_EPS)
    return (y * g.astype(jnp.float32) + b.astype(jnp.float32)).astype(x.dtype)


def stick_breaking_attention(q, k, v):
    S = q.shape[1]
    scale = SB_HEAD_DIM ** -0.5
    outs = []
    for i in range(S // Q_BLOCK):
        q0, q1 = i * Q_BLOCK, (i + 1) * Q_BLOCK
        qb = q[:, q0:q1]
        kb = k[:, :q1]
        vb = v[:, :q1]
        z = jnp.einsum('bqhd,bkhd->bhqk', qb, kb).astype(jnp.float32) * scale
        t_idx = q0 + jnp.arange(Q_BLOCK)[:, None]
        s_idx = jnp.arange(q1)[None, :]
        mask = s_idx < t_idx
        log_keep = jnp.where(mask, jax.nn.log_sigmoid(-z), 0.0)
        between = lax.cumsum(log_keep, axis=3, reverse=True) - log_keep
        w = jnp.where(mask, jnp.exp(jax.nn.log_sigmoid(z) + between), 0.0)
        outs.append(jnp.einsum('bhqk,bkhd->bqhd', w.astype(vb.dtype), vb))
    return jnp.concatenate(outs, axis=1)


def causal_window_mean(u, w):
    B, S, C = u.shape
    c = jnp.cumsum(u.astype(jnp.float32), axis=1)
    c_pad = jnp.concatenate([jnp.zeros((B, 1, C), jnp.float32), c], axis=1)
    hi = c_pad[:, 1:]
    lo = jnp.pad(c_pad[:, :S + 1 - w], ((0, 0), (w - 1, 0), (0, 0)))
    count = jnp.minimum(jnp.arange(1, S + 1), w).astype(jnp.float32)[None, :, None]
    return ((hi - lo) / count).astype(u.dtype)


def multiscale_pool(u, pool_w, pool_scale):
    B, S, _ = u.shape
    ug = u.reshape(B, S, POOL_GROUPS, POOL_GROUP_DIM)
    pooled = jnp.stack([causal_window_mean(ug[:, :, g], w) - ug[:, :, g]
                        for g, w in enumerate(POOL_WINDOWS)], axis=2)
    mixed = jnp.einsum('bsgc,gcd->bsgd', pooled, pool_w)
    return mixed.reshape(B, S, POOL_WIDTH) * pool_scale


def even_mixer(x, w_in, w_out, pool_w, pool_scale):
    B, S, _ = x.shape
    h = x @ w_in
    q = h[..., :SB_WIDTH].reshape(B, S, N_SB_HEADS, SB_HEAD_DIM)
    k = h[..., SB_WIDTH:2 * SB_WIDTH].reshape(B, S, N_SB_HEADS, SB_HEAD_DIM)
    v = h[..., 2 * SB_WIDTH:3 * SB_WIDTH].reshape(B, S, N_SB_HEADS, SB_HEAD_DIM)
    u = h[..., 3 * SB_WIDTH:]
    a = stick_breaking_attention(q, k, v).reshape(B, S, SB_WIDTH)
    b = multiscale_pool(u, pool_w, pool_scale)
    return jnp.concatenate([a, b], axis=-1) @ w_out


def conformer_conv(a, g, dw, ln_g, ln_b):
    h = a * jax.nn.sigmoid(g)
    h = lax.conv_general_dilated(h, dw[:, None, :], window_strides=(1,),
                                 padding=[(CONV_TAPS - 1, 0)],
                                 dimension_numbers=('NWC', 'WIO', 'NWC'),
                                 feature_group_count=CONV_WIDTH)
    return jax.nn.silu(layer_norm(h, ln_g, ln_b))


def chunked_spatial_gating(zc, ln_g, ln_b, sg_w, sg_b):
    B, S, _ = zc.shape
    z = jax.nn.gelu(zc)
    u, v = z[..., :SG_WIDTH], z[..., SG_WIDTH:]
    v = layer_norm(v, ln_g, ln_b)
    vc = v.reshape(B, S // SG_CHUNK, SG_CHUNK, SG_GROUPS, SG_GROUP_DIM)
    mask = jnp.tril(jnp.ones((SG_CHUNK, SG_CHUNK), sg_w.dtype))
    sv = jnp.einsum('gts,bcsgd->bctgd', sg_w * mask[None], vc)
    sv = sv + sg_b.T[None, None, :, :, None]
    return u * sv.reshape(B, S, SG_WIDTH)


def odd_mixer(x, w_in, w_out, conv_dw, conv_ln_g, conv_ln_b, sg_ln_g, sg_ln_b, sg_w, sg_b):
    h = x @ w_in
    a = h[..., :CONV_WIDTH]
    g = h[..., CONV_WIDTH:2 * CONV_WIDTH]
    zc = h[..., 2 * CONV_WIDTH:]
    c_out = conformer_conv(a, g, conv_dw, conv_ln_g, conv_ln_b)
    d_out = chunked_spatial_gating(zc, sg_ln_g, sg_ln_b, sg_w, sg_b)
    return jnp.concatenate([c_out, d_out], axis=-1) @ w_out


def swiglu(x, w_gate, w_up, w_down):
    return (jax.nn.silu(x @ w_gate) * (x @ w_up)) @ w_down


def _fwd_setup_inputs(seed: int = 0) -> dict:
    key = jax.random.key(seed)
    ks = iter(jax.random.split(key, 40))
    nrm = lambda shape, s: jax.random.normal(next(ks), shape, jnp.float32) * s
    d = D_MODEL
    return {
        "x": nrm((BATCH, SEQ, d), 1.0),
        "p": nrm((DEPTH, BATCH, SEQ, PLE_DIM), 1.0),
        "even_w_in": nrm((N_EVEN, d, EVEN_IN), d ** -0.5),
        "even_w_out": nrm((N_EVEN, MIX_WIDTH, d), MIX_WIDTH ** -0.5 * BETA_INIT),
        "pool_w": nrm((N_EVEN, POOL_GROUPS, POOL_GROUP_DIM, POOL_GROUP_DIM), POOL_GROUP_DIM ** -0.5),
        "pool_scale": 1.0 + nrm((N_EVEN, POOL_WIDTH), 0.02),
        "odd_w_in": nrm((N_ODD, d, ODD_IN), d ** -0.5),
        "odd_w_out": nrm((N_ODD, MIX_WIDTH, d), MIX_WIDTH ** -0.5 * BETA_INIT),
        "conv_dw": nrm((N_ODD, CONV_TAPS, CONV_WIDTH), CONV_TAPS ** -0.5),
        "conv_ln_g": 1.0 + nrm((N_ODD, CONV_WIDTH), 0.02),
        "conv_ln_b": nrm((N_ODD, CONV_WIDTH), 0.02),
        "sg_ln_g": 1.0 + nrm((N_ODD, SG_WIDTH), 0.02),
        "sg_ln_b": nrm((N_ODD, SG_WIDTH), 0.02),
        "sg_w": nrm((N_ODD, SG_GROUPS, SG_CHUNK, SG_CHUNK), SG_CHUNK ** -0.5),
        "sg_b": 1.0 + nrm((N_ODD, SG_GROUPS, SG_CHUNK), 0.1),
        "ln_mix_g": 1.0 + nrm((DEPTH, d), 0.02),
        "ln_mix_b": nrm((DEPTH, d), 0.02),
        "ffn_w_gate": nrm((DEPTH, d, D_FF), d ** -0.5),
        "ffn_w_up": nrm((DEPTH, d, D_FF), d ** -0.5),
        "ffn_w_down": nrm((DEPTH, D_FF, d), D_FF ** -0.5 * BETA_INIT),
        "ln_ffn_g": 1.0 + nrm((DEPTH, d), 0.02),
        "ln_ffn_b": nrm((DEPTH, d), 0.02),
        "ple_w_proj": nrm((DEPTH, PLE_DIM, d), PLE_DIM ** -0.5),
        "ple_w_gate": nrm((DEPTH, d, d), d ** -0.5),
        "ple_b_gate": nrm((DEPTH, d), 0.02),
    }


def _fwd_reference(x, p, even_w_in, even_w_out, pool_w, pool_scale,
              odd_w_in, odd_w_out, conv_dw, conv_ln_g, conv_ln_b,
              sg_ln_g, sg_ln_b, sg_w, sg_b,
              ln_mix_g, ln_mix_b, ffn_w_gate, ffn_w_up, ffn_w_down,
              ln_ffn_g, ln_ffn_b, ple_w_proj, ple_w_gate, ple_b_gate):
    for i in range(DEPTH):
        j = i // 2
        if i % 2 == 0:
            mix = even_mixer(x, even_w_in[j], even_w_out[j], pool_w[j], pool_scale[j])
        else:
            mix = odd_mixer(x, odd_w_in[j], odd_w_out[j], conv_dw[j], conv_ln_g[j], conv_ln_b[j],
                            sg_ln_g[j], sg_ln_b[j], sg_w[j], sg_b[j])
        x = layer_norm(ALPHA * x + mix, ln_mix_g[i], ln_mix_b[i])
        x = layer_norm(ALPHA * x + swiglu(x, ffn_w_gate[i], ffn_w_up[i], ffn_w_down[i]),
                       ln_ffn_g[i], ln_ffn_b[i])
        gate = jax.nn.sigmoid(x @ ple_w_gate[i] + ple_b_gate[i])
        x = x + gate * (p[i] @ ple_w_proj[i])
    return x


import jax as _jax
import jax.numpy as _jnp

TWIN_FORMAT = 'train_step'
FWD_PARAMS = ['x', 'p', 'even_w_in', 'even_w_out', 'pool_w', 'pool_scale', 'odd_w_in', 'odd_w_out', 'conv_dw', 'conv_ln_g', 'conv_ln_b', 'sg_ln_g', 'sg_ln_b', 'sg_w', 'sg_b', 'ln_mix_g', 'ln_mix_b', 'ffn_w_gate', 'ffn_w_up', 'ffn_w_down', 'ln_ffn_g', 'ln_ffn_b', 'ple_w_proj', 'ple_w_gate', 'ple_b_gate']
TWIN_WEIGHTS = ['even_w_in', 'even_w_out', 'pool_w', 'pool_scale', 'odd_w_in', 'odd_w_out', 'conv_dw', 'conv_ln_g', 'conv_ln_b', 'sg_ln_g', 'sg_ln_b', 'sg_w', 'sg_b', 'ln_mix_g', 'ln_mix_b', 'ffn_w_gate', 'ffn_w_up', 'ffn_w_down', 'ln_ffn_g', 'ln_ffn_b', 'ple_w_proj', 'ple_w_gate', 'ple_b_gate']
TWIN_DIFF_INPUT = 'x'
TWIN_INPUTS = ['x', 'p', 'even_w_in', 'even_w_out', 'pool_w', 'pool_scale', 'odd_w_in', 'odd_w_out', 'conv_dw', 'conv_ln_g', 'conv_ln_b', 'sg_ln_g', 'sg_ln_b', 'sg_w', 'sg_b', 'ln_mix_g', 'ln_mix_b', 'ffn_w_gate', 'ffn_w_up', 'ffn_w_down', 'ln_ffn_g', 'ln_ffn_b', 'ple_w_proj', 'ple_w_gate', 'ple_b_gate', 'loss_target', 'm_even_w_in', 'm_even_w_out', 'm_pool_w', 'm_pool_scale', 'm_odd_w_in', 'm_odd_w_out', 'm_conv_dw', 'm_conv_ln_g', 'm_conv_ln_b', 'm_sg_ln_g', 'm_sg_ln_b', 'm_sg_w', 'm_sg_b', 'm_ln_mix_g', 'm_ln_mix_b', 'm_ffn_w_gate', 'm_ffn_w_up', 'm_ffn_w_down', 'm_ln_ffn_g', 'm_ln_ffn_b', 'm_ple_w_proj', 'm_ple_w_gate', 'm_ple_b_gate', 'v_even_w_in', 'v_even_w_out', 'v_pool_w', 'v_pool_scale', 'v_odd_w_in', 'v_odd_w_out', 'v_conv_dw', 'v_conv_ln_g', 'v_conv_ln_b', 'v_sg_ln_g', 'v_sg_ln_b', 'v_sg_w', 'v_sg_b', 'v_ln_mix_g', 'v_ln_mix_b', 'v_ffn_w_gate', 'v_ffn_w_up', 'v_ffn_w_down', 'v_ln_ffn_g', 'v_ln_ffn_b', 'v_ple_w_proj', 'v_ple_w_gate', 'v_ple_b_gate']
TWIN_OUTPUTS = ['loss', 'grad_x', 'grad_even_w_in', 'grad_even_w_out', 'grad_pool_w', 'grad_pool_scale', 'grad_odd_w_in', 'grad_odd_w_out', 'grad_conv_dw', 'grad_conv_ln_g', 'grad_conv_ln_b', 'grad_sg_ln_g', 'grad_sg_ln_b', 'grad_sg_w', 'grad_sg_b', 'grad_ln_mix_g', 'grad_ln_mix_b', 'grad_ffn_w_gate', 'grad_ffn_w_up', 'grad_ffn_w_down', 'grad_ln_ffn_g', 'grad_ln_ffn_b', 'grad_ple_w_proj', 'grad_ple_w_gate', 'grad_ple_b_gate', 'delta_even_w_in', 'delta_even_w_out', 'delta_pool_w', 'delta_pool_scale', 'delta_odd_w_in', 'delta_odd_w_out', 'delta_conv_dw', 'delta_conv_ln_g', 'delta_conv_ln_b', 'delta_sg_ln_g', 'delta_sg_ln_b', 'delta_sg_w', 'delta_sg_b', 'delta_ln_mix_g', 'delta_ln_mix_b', 'delta_ffn_w_gate', 'delta_ffn_w_up', 'delta_ffn_w_down', 'delta_ln_ffn_g', 'delta_ln_ffn_b', 'delta_ple_w_proj', 'delta_ple_w_gate', 'delta_ple_b_gate', 'new_m_even_w_in', 'new_m_even_w_out', 'new_m_pool_w', 'new_m_pool_scale', 'new_m_odd_w_in', 'new_m_odd_w_out', 'new_m_conv_dw', 'new_m_conv_ln_g', 'new_m_conv_ln_b', 'new_m_sg_ln_g', 'new_m_sg_ln_b', 'new_m_sg_w', 'new_m_sg_b', 'new_m_ln_mix_g', 'new_m_ln_mix_b', 'new_m_ffn_w_gate', 'new_m_ffn_w_up', 'new_m_ffn_w_down', 'new_m_ln_ffn_g', 'new_m_ln_ffn_b', 'new_m_ple_w_proj', 'new_m_ple_w_gate', 'new_m_ple_b_gate', 'new_v_even_w_in', 'new_v_even_w_out', 'new_v_pool_w', 'new_v_pool_scale', 'new_v_odd_w_in', 'new_v_odd_w_out', 'new_v_conv_dw', 'new_v_conv_ln_g', 'new_v_conv_ln_b', 'new_v_sg_ln_g', 'new_v_sg_ln_b', 'new_v_sg_w', 'new_v_sg_b', 'new_v_ln_mix_g', 'new_v_ln_mix_b', 'new_v_ffn_w_gate', 'new_v_ffn_w_up', 'new_v_ffn_w_down', 'new_v_ln_ffn_g', 'new_v_ln_ffn_b', 'new_v_ple_w_proj', 'new_v_ple_w_gate', 'new_v_ple_b_gate']
TWIN_LEAF_KINDS = {'loss': 'loss', 'grad_x': 'grad_x', 'grad_even_w_in': 'grad_w', 'grad_even_w_out': 'grad_w', 'grad_pool_w': 'grad_w', 'grad_pool_scale': 'grad_w', 'grad_odd_w_in': 'grad_w', 'grad_odd_w_out': 'grad_w', 'grad_conv_dw': 'grad_w', 'grad_conv_ln_g': 'grad_w', 'grad_conv_ln_b': 'grad_w', 'grad_sg_ln_g': 'grad_w', 'grad_sg_ln_b': 'grad_w', 'grad_sg_w': 'grad_w', 'grad_sg_b': 'grad_w', 'grad_ln_mix_g': 'grad_w', 'grad_ln_mix_b': 'grad_w', 'grad_ffn_w_gate': 'grad_w', 'grad_ffn_w_up': 'grad_w', 'grad_ffn_w_down': 'grad_w', 'grad_ln_ffn_g': 'grad_w', 'grad_ln_ffn_b': 'grad_w', 'grad_ple_w_proj': 'grad_w', 'grad_ple_w_gate': 'grad_w', 'grad_ple_b_gate': 'grad_w', 'delta_even_w_in': 'delta_w', 'delta_even_w_out': 'delta_w', 'delta_pool_w': 'delta_w', 'delta_pool_scale': 'delta_w', 'delta_odd_w_in': 'delta_w', 'delta_odd_w_out': 'delta_w', 'delta_conv_dw': 'delta_w', 'delta_conv_ln_g': 'delta_w', 'delta_conv_ln_b': 'delta_w', 'delta_sg_ln_g': 'delta_w', 'delta_sg_ln_b': 'delta_w', 'delta_sg_w': 'delta_w', 'delta_sg_b': 'delta_w', 'delta_ln_mix_g': 'delta_w', 'delta_ln_mix_b': 'delta_w', 'delta_ffn_w_gate': 'delta_w', 'delta_ffn_w_up': 'delta_w', 'delta_ffn_w_down': 'delta_w', 'delta_ln_ffn_g': 'delta_w', 'delta_ln_ffn_b': 'delta_w', 'delta_ple_w_proj': 'delta_w', 'delta_ple_w_gate': 'delta_w', 'delta_ple_b_gate': 'delta_w', 'new_m_even_w_in': 'new_m', 'new_m_even_w_out': 'new_m', 'new_m_pool_w': 'new_m', 'new_m_pool_scale': 'new_m', 'new_m_odd_w_in': 'new_m', 'new_m_odd_w_out': 'new_m', 'new_m_conv_dw': 'new_m', 'new_m_conv_ln_g': 'new_m', 'new_m_conv_ln_b': 'new_m', 'new_m_sg_ln_g': 'new_m', 'new_m_sg_ln_b': 'new_m', 'new_m_sg_w': 'new_m', 'new_m_sg_b': 'new_m', 'new_m_ln_mix_g': 'new_m', 'new_m_ln_mix_b': 'new_m', 'new_m_ffn_w_gate': 'new_m', 'new_m_ffn_w_up': 'new_m', 'new_m_ffn_w_down': 'new_m', 'new_m_ln_ffn_g': 'new_m', 'new_m_ln_ffn_b': 'new_m', 'new_m_ple_w_proj': 'new_m', 'new_m_ple_w_gate': 'new_m', 'new_m_ple_b_gate': 'new_m', 'new_v_even_w_in': 'new_v', 'new_v_even_w_out': 'new_v', 'new_v_pool_w': 'new_v', 'new_v_pool_scale': 'new_v', 'new_v_odd_w_in': 'new_v', 'new_v_odd_w_out': 'new_v', 'new_v_conv_dw': 'new_v', 'new_v_conv_ln_g': 'new_v', 'new_v_conv_ln_b': 'new_v', 'new_v_sg_ln_g': 'new_v', 'new_v_sg_ln_b': 'new_v', 'new_v_sg_w': 'new_v', 'new_v_sg_b': 'new_v', 'new_v_ln_mix_g': 'new_v', 'new_v_ln_mix_b': 'new_v', 'new_v_ffn_w_gate': 'new_v', 'new_v_ffn_w_up': 'new_v', 'new_v_ffn_w_down': 'new_v', 'new_v_ln_ffn_g': 'new_v', 'new_v_ln_ffn_b': 'new_v', 'new_v_ple_w_proj': 'new_v', 'new_v_ple_w_gate': 'new_v', 'new_v_ple_b_gate': 'new_v'}


def _forward(args):
    return _fwd_reference(*[args[k] for k in FWD_PARAMS])


def _output_shape():
    out = _jax.eval_shape(lambda: _forward(_fwd_setup_inputs(0)))
    return out.shape, out.dtype

N_MICROBATCH = 1
ADAM_LR = 0.001
ADAM_B1 = 0.9
ADAM_B2 = 0.999
ADAM_EPS = 1e-08
ADAM_WD = 0.01
ADAM_STEP = 10
PER_EXAMPLE_BATCH_AXIS = {'x': 0, 'p': 1, 'loss_target': 0}
SHARED_INPUTS = []
_WEIGHT_DTYPES = {'even_w_in': _jnp.float32, 'even_w_out': _jnp.float32, 'pool_w': _jnp.float32, 'pool_scale': _jnp.float32, 'odd_w_in': _jnp.float32, 'odd_w_out': _jnp.float32, 'conv_dw': _jnp.float32, 'conv_ln_g': _jnp.float32, 'conv_ln_b': _jnp.float32, 'sg_ln_g': _jnp.float32, 'sg_ln_b': _jnp.float32, 'sg_w': _jnp.float32, 'sg_b': _jnp.float32, 'ln_mix_g': _jnp.float32, 'ln_mix_b': _jnp.float32, 'ffn_w_gate': _jnp.float32, 'ffn_w_up': _jnp.float32, 'ffn_w_down': _jnp.float32, 'ln_ffn_g': _jnp.float32, 'ln_ffn_b': _jnp.float32, 'ple_w_proj': _jnp.float32, 'ple_w_gate': _jnp.float32, 'ple_b_gate': _jnp.float32}
MOMENT_SCALE = {'even_w_in': 2.838173e-02, 'even_w_out': 7.490901e-02, 'pool_w': 4.248148e-02, 'pool_scale': 4.427806e-02, 'odd_w_in': 3.108248e-02, 'odd_w_out': 2.647841e-01, 'conv_dw': 3.727987e-02, 'conv_ln_g': 1.661854e-01, 'conv_ln_b': 2.420318e-01, 'sg_ln_g': 2.707821e-02, 'sg_ln_b': 2.469044e-02, 'sg_w': 2.456622e-02, 'sg_b': 3.349199e-02, 'ln_mix_g': 4.856685e-01, 'ln_mix_b': 1.557559e+00, 'ffn_w_gate': 1.891695e-02, 'ffn_w_up': 1.871295e-02, 'ffn_w_down': 6.203076e-02, 'ln_ffn_g': 1.168851e+01, 'ln_ffn_b': 1.940296e+00, 'ple_w_proj': 1.825296e-01, 'ple_w_gate': 1.007501e-01, 'ple_b_gate': 1.175856e+00}


def _to_microbatches(a, axis):
    t = _jnp.moveaxis(a, axis, 0)
    t = t.reshape((N_MICROBATCH, t.shape[0] // N_MICROBATCH) + t.shape[1:])
    return _jnp.moveaxis(t, 1, axis + 1)


def setup_inputs(seed: int = 0) -> dict:
    inp = _fwd_setup_inputs(seed)
    key = _jax.random.fold_in(_jax.random.key(seed), 7919)
    shape, _ = _output_shape()
    out = dict(inp)
    out["loss_target"] = _jax.random.normal(_jax.random.fold_in(key, 0), shape, _jnp.float32)
    for i, name in enumerate(TWIN_WEIGHTS):
        w = inp[name].astype(_jnp.float32)
        if MOMENT_SCALE is None:
            s = _jnp.sqrt(_jnp.mean(_jnp.square(w)) + 1e-30)
        else:
            s = MOMENT_SCALE[name]
        km, kv = _jax.random.split(_jax.random.fold_in(key, i + 1))
        out[name] = w
        out["m_" + name] = s * _jax.random.normal(km, w.shape, _jnp.float32)
        out["v_" + name] = (s * s) * _jax.random.uniform(kv, w.shape, _jnp.float32, 0.5, 1.5)
    if N_MICROBATCH > 1:
        for name, axis in PER_EXAMPLE_BATCH_AXIS.items():
            out[name] = _to_microbatches(out[name], axis)
    return {'x': out['x'], 'p': out['p'], 'even_w_in': out['even_w_in'], 'even_w_out': out['even_w_out'], 'pool_w': out['pool_w'], 'pool_scale': out['pool_scale'], 'odd_w_in': out['odd_w_in'], 'odd_w_out': out['odd_w_out'], 'conv_dw': out['conv_dw'], 'conv_ln_g': out['conv_ln_g'], 'conv_ln_b': out['conv_ln_b'], 'sg_ln_g': out['sg_ln_g'], 'sg_ln_b': out['sg_ln_b'], 'sg_w': out['sg_w'], 'sg_b': out['sg_b'], 'ln_mix_g': out['ln_mix_g'], 'ln_mix_b': out['ln_mix_b'], 'ffn_w_gate': out['ffn_w_gate'], 'ffn_w_up': out['ffn_w_up'], 'ffn_w_down': out['ffn_w_down'], 'ln_ffn_g': out['ln_ffn_g'], 'ln_ffn_b': out['ln_ffn_b'], 'ple_w_proj': out['ple_w_proj'], 'ple_w_gate': out['ple_w_gate'], 'ple_b_gate': out['ple_b_gate'], 'loss_target': out['loss_target'], 'm_even_w_in': out['m_even_w_in'], 'm_even_w_out': out['m_even_w_out'], 'm_pool_w': out['m_pool_w'], 'm_pool_scale': out['m_pool_scale'], 'm_odd_w_in': out['m_odd_w_in'], 'm_odd_w_out': out['m_odd_w_out'], 'm_conv_dw': out['m_conv_dw'], 'm_conv_ln_g': out['m_conv_ln_g'], 'm_conv_ln_b': out['m_conv_ln_b'], 'm_sg_ln_g': out['m_sg_ln_g'], 'm_sg_ln_b': out['m_sg_ln_b'], 'm_sg_w': out['m_sg_w'], 'm_sg_b': out['m_sg_b'], 'm_ln_mix_g': out['m_ln_mix_g'], 'm_ln_mix_b': out['m_ln_mix_b'], 'm_ffn_w_gate': out['m_ffn_w_gate'], 'm_ffn_w_up': out['m_ffn_w_up'], 'm_ffn_w_down': out['m_ffn_w_down'], 'm_ln_ffn_g': out['m_ln_ffn_g'], 'm_ln_ffn_b': out['m_ln_ffn_b'], 'm_ple_w_proj': out['m_ple_w_proj'], 'm_ple_w_gate': out['m_ple_w_gate'], 'm_ple_b_gate': out['m_ple_b_gate'], 'v_even_w_in': out['v_even_w_in'], 'v_even_w_out': out['v_even_w_out'], 'v_pool_w': out['v_pool_w'], 'v_pool_scale': out['v_pool_scale'], 'v_odd_w_in': out['v_odd_w_in'], 'v_odd_w_out': out['v_odd_w_out'], 'v_conv_dw': out['v_conv_dw'], 'v_conv_ln_g': out['v_conv_ln_g'], 'v_conv_ln_b': out['v_conv_ln_b'], 'v_sg_ln_g': out['v_sg_ln_g'], 'v_sg_ln_b': out['v_sg_ln_b'], 'v_sg_w': out['v_sg_w'], 'v_sg_b': out['v_sg_b'], 'v_ln_mix_g': out['v_ln_mix_g'], 'v_ln_mix_b': out['v_ln_mix_b'], 'v_ffn_w_gate': out['v_ffn_w_gate'], 'v_ffn_w_up': out['v_ffn_w_up'], 'v_ffn_w_down': out['v_ffn_w_down'], 'v_ln_ffn_g': out['v_ln_ffn_g'], 'v_ln_ffn_b': out['v_ln_ffn_b'], 'v_ple_w_proj': out['v_ple_w_proj'], 'v_ple_w_gate': out['v_ple_w_gate'], 'v_ple_b_gate': out['v_ple_b_gate']}


def _loss(weights, diff, rest, loss_target):
    with _jax.named_scope("forward"):
        args = {**rest, TWIN_DIFF_INPUT: diff, **{k: w.astype(_WEIGHT_DTYPES[k]) for k, w in weights.items()}}
        y = _forward(args)
    with _jax.named_scope("loss_head"):
        err = _jnp.square(y.astype(_jnp.float32) - loss_target)
        return 0.5 * _jnp.sum(_jnp.mean(err, axis=-1)) if err.ndim else 0.5 * err


def _adamw(w, g, m, v):
    m = ADAM_B1 * m + (1.0 - ADAM_B1) * g
    v = ADAM_B2 * v + (1.0 - ADAM_B2) * _jnp.square(g)
    m_hat = m / (1.0 - ADAM_B1 ** ADAM_STEP)
    v_hat = v / (1.0 - ADAM_B2 ** ADAM_STEP)
    delta = -ADAM_LR * (m_hat / (_jnp.sqrt(v_hat) + ADAM_EPS) + ADAM_WD * w)
    return delta, m, v


def reference(x, p, even_w_in, even_w_out, pool_w, pool_scale, odd_w_in, odd_w_out, conv_dw, conv_ln_g, conv_ln_b, sg_ln_g, sg_ln_b, sg_w, sg_b, ln_mix_g, ln_mix_b, ffn_w_gate, ffn_w_up, ffn_w_down, ln_ffn_g, ln_ffn_b, ple_w_proj, ple_w_gate, ple_b_gate, loss_target, m_even_w_in, m_even_w_out, m_pool_w, m_pool_scale, m_odd_w_in, m_odd_w_out, m_conv_dw, m_conv_ln_g, m_conv_ln_b, m_sg_ln_g, m_sg_ln_b, m_sg_w, m_sg_b, m_ln_mix_g, m_ln_mix_b, m_ffn_w_gate, m_ffn_w_up, m_ffn_w_down, m_ln_ffn_g, m_ln_ffn_b, m_ple_w_proj, m_ple_w_gate, m_ple_b_gate, v_even_w_in, v_even_w_out, v_pool_w, v_pool_scale, v_odd_w_in, v_odd_w_out, v_conv_dw, v_conv_ln_g, v_conv_ln_b, v_sg_ln_g, v_sg_ln_b, v_sg_w, v_sg_b, v_ln_mix_g, v_ln_mix_b, v_ffn_w_gate, v_ffn_w_up, v_ffn_w_down, v_ln_ffn_g, v_ln_ffn_b, v_ple_w_proj, v_ple_w_gate, v_ple_b_gate):
    given = dict(x=x, p=p, even_w_in=even_w_in, even_w_out=even_w_out, pool_w=pool_w, pool_scale=pool_scale, odd_w_in=odd_w_in, odd_w_out=odd_w_out, conv_dw=conv_dw, conv_ln_g=conv_ln_g, conv_ln_b=conv_ln_b, sg_ln_g=sg_ln_g, sg_ln_b=sg_ln_b, sg_w=sg_w, sg_b=sg_b, ln_mix_g=ln_mix_g, ln_mix_b=ln_mix_b, ffn_w_gate=ffn_w_gate, ffn_w_up=ffn_w_up, ffn_w_down=ffn_w_down, ln_ffn_g=ln_ffn_g, ln_ffn_b=ln_ffn_b, ple_w_proj=ple_w_proj, ple_w_gate=ple_w_gate, ple_b_gate=ple_b_gate, loss_target=loss_target, m_even_w_in=m_even_w_in, m_even_w_out=m_even_w_out, m_pool_w=m_pool_w, m_pool_scale=m_pool_scale, m_odd_w_in=m_odd_w_in, m_odd_w_out=m_odd_w_out, m_conv_dw=m_conv_dw, m_conv_ln_g=m_conv_ln_g, m_conv_ln_b=m_conv_ln_b, m_sg_ln_g=m_sg_ln_g, m_sg_ln_b=m_sg_ln_b, m_sg_w=m_sg_w, m_sg_b=m_sg_b, m_ln_mix_g=m_ln_mix_g, m_ln_mix_b=m_ln_mix_b, m_ffn_w_gate=m_ffn_w_gate, m_ffn_w_up=m_ffn_w_up, m_ffn_w_down=m_ffn_w_down, m_ln_ffn_g=m_ln_ffn_g, m_ln_ffn_b=m_ln_ffn_b, m_ple_w_proj=m_ple_w_proj, m_ple_w_gate=m_ple_w_gate, m_ple_b_gate=m_ple_b_gate, v_even_w_in=v_even_w_in, v_even_w_out=v_even_w_out, v_pool_w=v_pool_w, v_pool_scale=v_pool_scale, v_odd_w_in=v_odd_w_in, v_odd_w_out=v_odd_w_out, v_conv_dw=v_conv_dw, v_conv_ln_g=v_conv_ln_g, v_conv_ln_b=v_conv_ln_b, v_sg_ln_g=v_sg_ln_g, v_sg_ln_b=v_sg_ln_b, v_sg_w=v_sg_w, v_sg_b=v_sg_b, v_ln_mix_g=v_ln_mix_g, v_ln_mix_b=v_ln_mix_b, v_ffn_w_gate=v_ffn_w_gate, v_ffn_w_up=v_ffn_w_up, v_ffn_w_down=v_ffn_w_down, v_ln_ffn_g=v_ln_ffn_g, v_ln_ffn_b=v_ln_ffn_b, v_ple_w_proj=v_ple_w_proj, v_ple_w_gate=v_ple_w_gate, v_ple_b_gate=v_ple_b_gate)
    weights = {n: given[n] for n in TWIN_WEIGHTS}
    shared = {n: given[n] for n in SHARED_INPUTS}
    per_example = {n: given[n] for n in ['x', 'p']}
    grad_fn = _jax.value_and_grad(_loss, argnums=(0, 1))

    def one_microbatch(ex, loss_target):
        ex = dict(ex)
        diff = ex.pop(TWIN_DIFF_INPUT)
        return grad_fn(weights, diff, {**shared, **ex}, loss_target)

    if N_MICROBATCH == 1:
        loss, (grad_w, grad_x) = one_microbatch(per_example, given["loss_target"])
    else:
        def body(carry, xs):
            loss_sum, grad_sum = carry
            l_k, (gw_k, gx_k) = one_microbatch(xs[0], xs[1])
            with _jax.named_scope("update"):
                return (loss_sum + l_k, _jax.tree.map(_jnp.add, grad_sum, gw_k)), gx_k

        init = (_jnp.zeros((), _jnp.float32), _jax.tree.map(_jnp.zeros_like, weights))
        (loss, grad_w), grad_x = _jax.lax.scan(body, init, (per_example, given["loss_target"]))
    with _jax.named_scope("update"):
        delta_w, new_m, new_v = {}, {}, {}
        for n in TWIN_WEIGHTS:
            delta_w[n], new_m[n], new_v[n] = _adamw(weights[n], grad_w[n], given["m_" + n], given["v_" + n])
    return (loss, grad_x, *[grad_w[n] for n in TWIN_WEIGHTS], *[delta_w[n] for n in TWIN_WEIGHTS],
            *[new_m[n] for n in TWIN_WEIGHTS], *[new_v[n] for n in TWIN_WEIGHTS])
```

```python
import jax
import jax.numpy as jnp
from jax import lax
from jax.experimental import pallas as pl
from jax.experimental.pallas import tpu as pltpu

f32 = jnp.float32
bf16 = jnp.bfloat16

D_MODEL = 1024
N_HEADS = 8
HEAD_DIM = 64
Q_BLOCK = 128
POOL_WINDOWS = (2, 4, 8, 16)
GROUP_DIM = 128
CONV_TAPS = 31
SG_CHUNK = 128
DEPTH = 2
ALPHA = (2 * DEPTH) ** 0.25
LN_EPS = 1e-5
SB_SCALE = HEAD_DIM ** -0.5
ADAM_LR, ADAM_B1, ADAM_B2, ADAM_EPS, ADAM_WD, ADAM_STEP = 0.001, 0.9, 0.999, 1e-08, 0.01, 10
N_CHIPS = 4
N_DEV = 8
LANES = 128
VMEM_LIMIT = 56 * 1024 * 1024
TM = 512
TR = 256
CONV_ROWS = 64
CONV_HALO = 32
POOL_HALO = 16

_NN = (((1,), (0,)), ((), ()))
_NT = (((1,), (1,)), ((), ()))
_TN = (((0,), (0,)), ((), ()))
_ANY = pl.BlockSpec(memory_space=pl.ANY)
_MESH = pl.DeviceIdType.MESH


def _params():
    return pltpu.CompilerParams(vmem_limit_bytes=VMEM_LIMIT)


def _sds(shape, dtype=f32):
    return jax.ShapeDtypeStruct(tuple(shape), dtype)


def _dot(a, b, dims=_NN):
    return lax.dot_general(a, b, dims, preferred_element_type=f32)


def _sigmoid(x):
    return 1.0 / (1.0 + jnp.exp(-x))


def _mm(name, a, b, grid, a_spec, b_spec, out_shape, out_spec, dims, reduce=False, add=None, add_spec=None,
        add_scale=1.0):
    has_add = add is not None
    k_axis = len(grid) - 1

    def body(*refs):
        if has_add:
            a_ref, b_ref, add_ref, o_ref = refs
        else:
            a_ref, b_ref, o_ref = refs
        r = _dot(a_ref[...].astype(bf16), b_ref[...].astype(bf16), dims)
        if reduce:
            k = pl.program_id(k_axis)

            @pl.when(k == 0)
            def _():
                o_ref[...] = r + add_scale * add_ref[...] if has_add else r

            @pl.when(k > 0)
            def _():
                o_ref[...] += r
        else:
            o_ref[...] = r + add_scale * add_ref[...] if has_add else r

    ins = [a, b] + ([add] if has_add else [])
    specs = [a_spec, b_spec] + ([add_spec] if has_add else [])
    return pl.pallas_call(body, grid=grid, in_specs=specs, out_specs=out_spec, out_shape=_sds(out_shape),
                          name=name, compiler_params=_params())(*ins)


def _tm(s):
    return min(TM, s)


def _act_spec(a, tm, width):
    if a.ndim == 3:
        return pl.BlockSpec((None, tm, width), lambda i, k: (k, i, 0))
    return pl.BlockSpec((tm, width), lambda i, k: (i, k))


def _mm_nn_col(name, x, w4, layer, natural=False):
    s, kk = x.shape
    nq = w4.shape[3]
    tm = _tm(s)
    if natural:
        out_shape, out_spec = (s, 4 * nq), pl.BlockSpec((tm, nq), lambda i, k: (i, k))
    else:
        out_shape, out_spec = (4, s, nq), pl.BlockSpec((None, tm, nq), lambda i, k: (k, i, 0))
    return _mm(name, x, w4, (s // tm, 4), pl.BlockSpec((tm, kk), lambda i, k: (i, 0)),
               pl.BlockSpec((None, None, kk, nq), lambda i, k: (k, layer, 0, 0)), out_shape, out_spec, _NN)


def _mm_nn_row(name, a, w4, layer):
    s = a.shape[-2]
    kq, n = w4.shape[2], w4.shape[3]
    tm = _tm(s)
    return _mm(name, a, w4, (s // tm, 4), _act_spec(a, tm, kq),
               pl.BlockSpec((None, None, kq, n), lambda i, k: (k, layer, 0, 0)), (s, n),
               pl.BlockSpec((tm, n), lambda i, k: (i, 0)), _NN, reduce=True)


def _mm_nt_col(name, dh, w4, layer, add, add_scale):
    s = dh.shape[-2]
    kk, nq = w4.shape[2], w4.shape[3]
    tm = _tm(s)
    row = pl.BlockSpec((tm, kk), lambda i, k: (i, 0))
    return _mm(name, dh, w4, (s // tm, 4), _act_spec(dh, tm, nq),
               pl.BlockSpec((None, None, kk, nq), lambda i, k: (k, layer, 0, 0)), (s, kk), row, _NT,
               reduce=True, add=add, add_spec=row, add_scale=add_scale)


def _mm_nt_row(name, dy, w4, layer, natural=False, add=None):
    s, n = dy.shape
    kq = w4.shape[2]
    tm = _tm(s)
    if natural:
        out_shape, out_spec = (s, 4 * kq), pl.BlockSpec((tm, kq), lambda i, k: (i, k))
    else:
        out_shape, out_spec = (4, s, kq), pl.BlockSpec((None, tm, kq), lambda i, k: (k, i, 0))
    return _mm(name, dy, w4, (s // tm, 4), pl.BlockSpec((tm, n), lambda i, k: (i, 0)),
               pl.BlockSpec((None, None, kq, n), lambda i, k: (k, layer, 0, 0)), out_shape, out_spec, _NT,
               add=add, add_spec=out_spec if add is not None else None)


def _mm_tn_col(name, x, dh):
    s, kk = x.shape
    if dh.ndim == 3:
        nq = dh.shape[2]
        b_spec = pl.BlockSpec((None, s, nq), lambda k, j: (k, 0, 0))
    else:
        nq = dh.shape[1] // 4
        b_spec = pl.BlockSpec((s, nq), lambda k, j: (0, k))
    tk = min(256, kk)
    return _mm(name, x, dh, (4, kk // tk), pl.BlockSpec((s, tk), lambda k, j: (0, j)), b_spec, (4, kk, nq),
               pl.BlockSpec((None, tk, nq), lambda k, j: (k, j, 0)), _TN)


def _mm_tn_row(name, a, dy):
    s, n = dy.shape
    if a.ndim == 3:
        kq = a.shape[2]
        a_spec = pl.BlockSpec((None, s, kq), lambda k, j: (k, 0, 0))
    else:
        kq = a.shape[1] // 4
        a_spec = pl.BlockSpec((s, kq), lambda k, j: (0, k))
    tn = min(512, n)
    return _mm(name, a, dy, (4, n // tn), a_spec, pl.BlockSpec((s, tn), lambda k, j: (0, j)), (4, kq, n),
               pl.BlockSpec((None, kq, tn), lambda k, j: (k, 0, j)), _TN)


def _tr(s):
    return min(TR, s)


def _rows(tm, d):
    return pl.BlockSpec((tm, d), lambda i: (i, 0))


def _vec(d):
    return pl.BlockSpec((1, d), lambda i: (0, 0))


def _ln_fwd(name, x, mix, g, b):
    s, d = x.shape
    tm = _tr(s)

    def body(x_ref, m_ref, g_ref, b_ref, y_ref, xh_ref, rs_ref):
        r = ALPHA * x_ref[...] + m_ref[...]
        mu = jnp.mean(r, axis=-1, keepdims=True)
        c = r - mu
        rstd = lax.rsqrt(jnp.mean(c * c, axis=-1, keepdims=True) + LN_EPS)
        xh = c * rstd
        y_ref[...] = xh * g_ref[...] + b_ref[...]
        xh_ref[...] = xh
        rs_ref[...] = rstd

    return pl.pallas_call(
        body, grid=(s // tm,), in_specs=[_rows(tm, d), _rows(tm, d), _vec(d), _vec(d)],
        out_specs=[_rows(tm, d), _rows(tm, d), _rows(tm, 1)],
        out_shape=[_sds((s, d)), _sds((s, d)), _sds((s, 1))], name=name, compiler_params=_params())(x, mix, g, b)


def _ln_bwd_rows(dy, xh, rstd, g):
    dxh = dy * g
    m1 = jnp.mean(dxh, axis=-1, keepdims=True)
    m2 = jnp.mean(dxh * xh, axis=-1, keepdims=True)
    return rstd * (dxh - m1 - xh * m2)


def _ln_bwd(name, dy, xh, rstd, g):
    s, d = dy.shape
    tm = _tr(s)

    def body(dy_ref, xh_ref, rs_ref, g_ref, dr_ref, dg_ref, db_ref):
        @pl.when(pl.program_id(0) == 0)
        def _():
            dg_ref[...] = jnp.zeros_like(dg_ref)
            db_ref[...] = jnp.zeros_like(db_ref)

        dyv, xhv = dy_ref[...], xh_ref[...]
        dr_ref[...] = _ln_bwd_rows(dyv, xhv, rs_ref[...], g_ref[...])
        dg_ref[...] += jnp.sum(dyv * xhv, axis=0, keepdims=True)
        db_ref[...] += jnp.sum(dyv, axis=0, keepdims=True)

    return pl.pallas_call(
        body, grid=(s // tm,), in_specs=[_rows(tm, d), _rows(tm, d), _rows(tm, 1), _vec(d)],
        out_specs=[_rows(tm, d), _vec(d), _vec(d)],
        out_shape=[_sds((s, d)), _sds((1, d)), _sds((1, d))], name=name, compiler_params=_params())(dy, xh, rstd, g)


def _sm_spec(tm, w):
    return pl.BlockSpec((None, tm, w), lambda k, i: (k, i, 0))


def _swiglu_fwd(name, hg, hu):
    _, s, w = hg.shape
    tm = _tr(s)

    def body(g_ref, u_ref, a_ref):
        g = g_ref[...]
        a_ref[...] = g * _sigmoid(g) * u_ref[...]

    return pl.pallas_call(body, grid=(4, s // tm), in_specs=[_sm_spec(tm, w)] * 2, out_specs=_sm_spec(tm, w),
                          out_shape=_sds(hg.shape), name=name, compiler_params=_params())(hg, hu)


def _swiglu_bwd(name, da, hg, hu):
    _, s, w = hg.shape
    tm = _tr(s)

    def body(da_ref, g_ref, u_ref, dg_ref, du_ref):
        g, da_v = g_ref[...], da_ref[...]
        sg = _sigmoid(g)
        du_ref[...] = da_v * g * sg
        dg_ref[...] = da_v * u_ref[...] * sg * (1.0 + g * (1.0 - sg))

    return pl.pallas_call(body, grid=(4, s // tm), in_specs=[_sm_spec(tm, w)] * 3, out_specs=[_sm_spec(tm, w)] * 2,
                          out_shape=[_sds(hg.shape)] * 2, name=name, compiler_params=_params())(da, hg, hu)


def _ple_fwd(name, x2, gp, bias, pp):
    s, d = x2.shape
    tm = _tr(s)

    def body(x_ref, gp_ref, b_ref, pp_ref, y_ref):
        y_ref[...] = x_ref[...] + _sigmoid(gp_ref[...] + b_ref[...]) * pp_ref[...]

    return pl.pallas_call(body, grid=(s // tm,), in_specs=[_rows(tm, d), _rows(tm, d), _vec(d), _rows(tm, d)],
                          out_specs=_rows(tm, d), out_shape=_sds((s, d)), name=name,
                          compiler_params=_params())(x2, gp, bias, pp)


def _ple_bwd(name, dy, gp, bias, pp):
    s, d = dy.shape
    tm = _tr(s)

    def body(dy_ref, gp_ref, b_ref, pp_ref, dgp_ref, dpp_ref, db_ref):
        @pl.when(pl.program_id(0) == 0)
        def _():
            db_ref[...] = jnp.zeros_like(db_ref)

        dyv = dy_ref[...]
        gate = _sigmoid(gp_ref[...] + b_ref[...])
        dgp = dyv * pp_ref[...] * gate * (1.0 - gate)
        dgp_ref[...] = dgp
        dpp_ref[...] = dyv * gate
        db_ref[...] += jnp.sum(dgp, axis=0, keepdims=True)

    return pl.pallas_call(body, grid=(s // tm,), in_specs=[_rows(tm, d), _rows(tm, d), _vec(d), _rows(tm, d)],
                          out_specs=[_rows(tm, d), _rows(tm, d), _vec(d)],
                          out_shape=[_sds((s, d)), _sds((s, d)), _sds((1, d))], name=name,
                          compiler_params=_params())(dy, gp, bias, pp)


def _loss_head(name, y, target):
    s, d = y.shape
    tm = _tr(s)

    def body(y_ref, t_ref, l_ref, dy_ref):
        @pl.when(pl.program_id(0) == 0)
        def _():
            l_ref[...] = jnp.zeros_like(l_ref)

        e = y_ref[...] - t_ref[...]
        dy_ref[...] = e * (1.0 / d)
        tot = jnp.sum(jnp.sum(e * e, axis=1, keepdims=True), axis=0, keepdims=True) * (0.5 / d)
        l_ref[...] += jnp.broadcast_to(tot, l_ref.shape)

    return pl.pallas_call(body, grid=(s // tm,), in_specs=[_rows(tm, d), _rows(tm, d)],
                          out_specs=[pl.BlockSpec((1, LANES), lambda i: (0, 0)), _rows(tm, d)],
                          out_shape=[_sds((1, LANES)), _sds((s, d))], name=name, compiler_params=_params())(y, target)


def _split_dot(x, m01):
    hi = x.astype(bf16)
    lo = (x - hi.astype(f32)).astype(bf16)
    return _dot(hi, m01) + _dot(lo, m01)


def _softplus(z):
    return jnp.maximum(z, 0.0) + jnp.log(1.0 + jnp.exp(-jnp.abs(z)))


def _sba_fwd(q, k, v):
    h, s, dh = q.shape
    qb_n = Q_BLOCK

    def body(q_ref, k_ref, v_ref, o_ref):
        i = pl.program_id(1)
        qb = q_ref[...].astype(bf16)
        row = lax.broadcasted_iota(jnp.int32, (qb_n, qb_n), 0)
        col = lax.broadcasted_iota(jnp.int32, (qb_n, qb_n), 1)
        later = (row >= col).astype(bf16)

        def step(n, carry):
            tail, acc = carry
            off = pl.multiple_of((i - n) * qb_n, qb_n)
            kb = k_ref[pl.ds(off, qb_n), :].astype(bf16)
            vb = v_ref[pl.ds(off, qb_n), :].astype(bf16)
            z = _dot(qb, kb, _NT) * SB_SCALE
            mask = col < row + jnp.minimum(n, 1) * qb_n
            sp = _softplus(z)
            lk = jnp.where(mask, -sp, 0.0)
            cum = _split_dot(lk, later)
            w = jnp.where(mask, jnp.exp(z - sp + cum - lk + tail), 0.0)
            return tail + cum[:, 0:1], acc + _dot(w.astype(bf16), vb)

        _, acc = lax.fori_loop(0, i + 1, step, (jnp.zeros((qb_n, 1), f32), jnp.zeros((qb_n, dh), f32)))
        o_ref[...] = acc

    blk = pl.BlockSpec((None, qb_n, dh), lambda hh, i: (hh, i, 0))
    full = pl.BlockSpec((None, s, dh), lambda hh, i: (hh, 0, 0))
    return pl.pallas_call(body, grid=(h, s // qb_n), in_specs=[blk, full, full], out_specs=blk,
                          out_shape=_sds((h, s, dh)), name="sba_fwd", compiler_params=_params())(q, k, v)


def _sba_bwd(q, k, v, do):
    h, s, dh = q.shape
    qb_n = Q_BLOCK

    def body(q_ref, k_ref, v_ref, do_ref, dq_ref, dk_ref, dv_ref):
        i = pl.program_id(1)

        @pl.when(i == 0)
        def _():
            dk_ref[...] = jnp.zeros_like(dk_ref)
            dv_ref[...] = jnp.zeros_like(dv_ref)

        qb = q_ref[...].astype(bf16)
        dob = do_ref[...].astype(bf16)
        row = lax.broadcasted_iota(jnp.int32, (qb_n, qb_n), 0)
        col = lax.broadcasted_iota(jnp.int32, (qb_n, qb_n), 1)
        upto = (row <= col).astype(bf16)
        before = (row < col).astype(bf16)

        def scores(j):
            off = pl.multiple_of(j * qb_n, qb_n)
            kb = k_ref[pl.ds(off, qb_n), :].astype(bf16)
            z = _dot(qb, kb, _NT) * SB_SCALE
            mask = col < row + jnp.minimum(i - j, 1) * qb_n
            sp = _softplus(z)
            return off, kb, z, mask, sp, jnp.where(mask, -sp, 0.0)

        def total(j, t):
            return t + jnp.sum(scores(j)[5], axis=1, keepdims=True)

        tot = lax.fori_loop(0, i + 1, total, jnp.zeros((qb_n, 1), f32))

        def step(j, carry):
            head, ehead, dq = carry
            off, kb, z, mask, sp, lk = scores(j)
            vb = v_ref[pl.ds(off, qb_n), :].astype(bf16)
            pre = _split_dot(lk, upto)
            w = jnp.where(mask, jnp.exp(z - sp + (tot - head - pre)), 0.0)
            e = _dot(dob, vb, _NT) * w
            epre = ehead + _split_dot(e, before)
            dz = jnp.where(mask, e * jnp.exp(-sp) - epre * jnp.exp(z - sp), 0.0) * SB_SCALE
            dzb = dz.astype(bf16)
            dk_ref[pl.ds(off, qb_n), :] += _dot(dzb, qb, _TN)
            dv_ref[pl.ds(off, qb_n), :] += _dot(w.astype(bf16), dob, _TN)
            return (head + pre[:, qb_n - 1:qb_n], ehead + jnp.sum(e, axis=1, keepdims=True), dq + _dot(dzb, kb))

        zero = jnp.zeros((qb_n, 1), f32)
        _, _, dq = lax.fori_loop(0, i + 1, step, (zero, zero, jnp.zeros((qb_n, dh), f32)))
        dq_ref[...] = dq

    blk = pl.BlockSpec((None, qb_n, dh), lambda hh, i: (hh, i, 0))
    full = pl.BlockSpec((None, s, dh), lambda hh, i: (hh, 0, 0))
    return pl.pallas_call(body, grid=(h, s // qb_n), in_specs=[blk, full, full, blk], out_specs=[blk, full, full],
                          out_shape=[_sds((h, s, dh))] * 3, name="sba_bwd", compiler_params=_params())(q, k, v, do)


def _pool_fwd(hsm, pool_w, pool_scale):
    _, s, wd = hsm.shape
    ch = min(256, s)

    def body(u_ref, w_ref, sc_ref, b_ref, pooled_ref, pad_ref):
        pad_ref[0:POOL_HALO, :] = jnp.zeros((POOL_HALO, wd), f32)
        pad_ref[POOL_HALO:POOL_HALO + s, :] = u_ref[...]
        for g, win in enumerate(POOL_WINDOWS):
            cols = slice(g * GROUP_DIM, (g + 1) * GROUP_DIM)
            wg = w_ref[g].astype(bf16)
            for r0 in range(0, s, ch):
                acc = pad_ref[POOL_HALO + r0:POOL_HALO + r0 + ch, cols]
                own = acc
                for dlt in range(1, win):
                    acc = acc + pad_ref[POOL_HALO + r0 - dlt:POOL_HALO + r0 - dlt + ch, cols]
                t = r0 + lax.broadcasted_iota(jnp.int32, (ch, 1), 0)
                cnt = jnp.minimum(t + 1, win).astype(f32)
                pooled = acc / cnt - own
                pooled_ref[r0:r0 + ch, cols] = pooled
                b_ref[r0:r0 + ch, cols] = _dot(pooled.astype(bf16), wg) * sc_ref[:, cols]

    return pl.pallas_call(
        body, grid=(1,),
        in_specs=[pl.BlockSpec((None, s, wd), lambda i: (3, 0, 0)), pl.BlockSpec(pool_w.shape, lambda i: (0, 0, 0)),
                  _vec(wd)],
        out_specs=[pl.BlockSpec((s, wd), lambda i: (0, 0))] * 2, out_shape=[_sds((s, wd))] * 2,
        scratch_shapes=[pltpu.VMEM((POOL_HALO + s, wd), f32)], name="pool_fwd",
        compiler_params=_params())(hsm, pool_w, pool_scale)


def _pool_bwd(dcat, pooled, pool_w, pool_scale):
    s, wd = pooled.shape
    ch = min(256, s)

    def body(db_ref, p_ref, w_ref, sc_ref, du_ref, dw_ref, dsc_ref, pad_ref):
        pad_ref[s:s + POOL_HALO, :] = jnp.zeros((POOL_HALO, wd), f32)
        for g, win in enumerate(POOL_WINDOWS):
            cols = slice(g * GROUP_DIM, (g + 1) * GROUP_DIM)
            wg = w_ref[g].astype(bf16)
            pooled_g = p_ref[:, cols].astype(bf16)
            db = db_ref[:, cols]
            dmixed = (db * sc_ref[:, cols]).astype(bf16)
            dsc_ref[:, cols] = jnp.sum(db * _dot(pooled_g, wg), axis=0, keepdims=True)
            dw_ref[g] = _dot(pooled_g, dmixed, _TN)
            dpooled = _dot(dmixed, wg, _NT)
            t = lax.broadcasted_iota(jnp.int32, (s, 1), 0)
            pad_ref[0:s, cols] = dpooled / jnp.minimum(t + 1, win).astype(f32)
            for r0 in range(0, s, ch):
                acc = pad_ref[r0:r0 + ch, cols]
                for dlt in range(1, win):
                    acc = acc + pad_ref[r0 + dlt:r0 + dlt + ch, cols]
                du_ref[r0:r0 + ch, cols] = acc - dpooled[r0:r0 + ch]

    return pl.pallas_call(
        body, grid=(1,),
        in_specs=[pl.BlockSpec((s, wd), lambda i: (0, 1)), pl.BlockSpec((s, wd), lambda i: (0, 0)),
                  pl.BlockSpec(pool_w.shape, lambda i: (0, 0, 0)), _vec(wd)],
        out_specs=[pl.BlockSpec((s, wd), lambda i: (0, 0)), pl.BlockSpec(pool_w.shape, lambda i: (0, 0, 0)), _vec(wd)],
        out_shape=[_sds((s, wd)), _sds(pool_w.shape), _sds((1, wd))],
        scratch_shapes=[pltpu.VMEM((s + POOL_HALO, wd), f32)], name="pool_bwd",
        compiler_params=_params())(dcat, pooled, pool_w, pool_scale)


def _conv_fwd(hsm, dw, ln_g, ln_b):
    _, s, wd = hsm.shape
    rows, halo = CONV_ROWS, CONV_HALO

    def body(a_ref, g_ref, dw_ref, lg_ref, lb_ref, out_ref, hc_ref, xh_ref, rs_ref, pad_ref):
        hc = a_ref[...] * _sigmoid(g_ref[...])
        hc_ref[...] = hc
        pad_ref[0:halo, :] = jnp.zeros((halo, wd), f32)
        pad_ref[halo:halo + s, :] = hc
        taps = dw_ref[...]

        def chunk(c, _):
            base = pl.multiple_of(c * rows, rows)
            win = pad_ref[pl.ds(base, rows + halo), :]
            y = jnp.zeros((rows, wd), f32)
            for k in range(CONV_TAPS):
                lo = halo - (CONV_TAPS - 1) + k
                y = y + taps[k:k + 1, :] * win[lo:lo + rows]
            mu = jnp.mean(y, axis=-1, keepdims=True)
            cen = y - mu
            rstd = lax.rsqrt(jnp.mean(cen * cen, axis=-1, keepdims=True) + LN_EPS)
            xh = cen * rstd
            n = xh * lg_ref[...] + lb_ref[...]
            out_ref[pl.ds(base, rows), :] = n * _sigmoid(n)
            xh_ref[pl.ds(base, rows), :] = xh
            rs_ref[pl.ds(base, rows), :] = rstd
            return 0

        lax.fori_loop(0, s // rows, chunk, 0)

    full = pl.BlockSpec((s, wd), lambda i: (0, 0))
    return pl.pallas_call(
        body, grid=(1,),
        in_specs=[pl.BlockSpec((None, s, wd), lambda i: (0, 0, 0)), pl.BlockSpec((None, s, wd), lambda i: (1, 0, 0)),
                  pl.BlockSpec(dw.shape, lambda i: (0, 0)), _vec(wd), _vec(wd)],
        out_specs=[full, full, full, pl.BlockSpec((s, 1), lambda i: (0, 0))],
        out_shape=[_sds((s, wd))] * 3 + [_sds((s, 1))],
        scratch_shapes=[pltpu.VMEM((halo + s, wd), f32)], name="conv_fwd",
        compiler_params=_params())(hsm, hsm, dw, ln_g, ln_b)


def _conv_bwd(dcat, hsm, hc, xh, rstd, dw, ln_g, ln_b):
    s, wd = hc.shape
    rows, halo = CONV_ROWS, CONV_HALO

    def body(dc_ref, a_ref, g_ref, hc_ref, xh_ref, rs_ref, dw_ref, lg_ref, lb_ref,
             da_ref, dg_ref, ddw_ref, dlg_ref, dlb_ref, hpad_ref, ypad_ref):
        hpad_ref[0:halo, :] = jnp.zeros((halo, wd), f32)
        hpad_ref[halo:halo + s, :] = hc_ref[...]
        ypad_ref[s:s + halo, :] = jnp.zeros((halo, wd), f32)
        ddw_ref[...] = jnp.zeros_like(ddw_ref)
        dlg_ref[...] = jnp.zeros_like(dlg_ref)
        dlb_ref[...] = jnp.zeros_like(dlb_ref)
        taps = dw_ref[...]

        def norm_bwd(c, _):
            base = pl.multiple_of(c * rows, rows)
            xhv = xh_ref[pl.ds(base, rows), :]
            n = xhv * lg_ref[...] + lb_ref[...]
            sn = _sigmoid(n)
            dn = dc_ref[pl.ds(base, rows), :] * sn * (1.0 + n * (1.0 - sn))
            dlg_ref[...] += jnp.sum(dn * xhv, axis=0, keepdims=True)
            dlb_ref[...] += jnp.sum(dn, axis=0, keepdims=True)
            ypad_ref[pl.ds(base, rows), :] = _ln_bwd_rows(dn, xhv, rs_ref[pl.ds(base, rows), :], lg_ref[...])
            return 0

        lax.fori_loop(0, s // rows, norm_bwd, 0)

        def conv_bwd(c, _):
            base = pl.multiple_of(c * rows, rows)
            ywin = ypad_ref[pl.ds(base, rows + halo), :]
            hwin = hpad_ref[pl.ds(base, rows + halo), :]
            dy = ywin[0:rows]
            dhc = jnp.zeros((rows, wd), f32)
            for k in range(CONV_TAPS):
                fwd = CONV_TAPS - 1 - k
                dhc = dhc + taps[k:k + 1, :] * ywin[fwd:fwd + rows]
                lo = halo - (CONV_TAPS - 1) + k
                ddw_ref[k:k + 1, :] += jnp.sum(dy * hwin[lo:lo + rows], axis=0, keepdims=True)
            sg = _sigmoid(g_ref[pl.ds(base, rows), :])
            da_ref[pl.ds(base, rows), :] = dhc * sg
            dg_ref[pl.ds(base, rows), :] = dhc * a_ref[pl.ds(base, rows), :] * sg * (1.0 - sg)
            return 0

        lax.fori_loop(0, s // rows, conv_bwd, 0)

    full = pl.BlockSpec((s, wd), lambda i: (0, 0))
    tap_spec = pl.BlockSpec(dw.shape, lambda i: (0, 0))
    return pl.pallas_call(
        body, grid=(1,),
        in_specs=[full, pl.BlockSpec((None, s, wd), lambda i: (0, 0, 0)), pl.BlockSpec((None, s, wd), lambda i: (1, 0, 0)),
                  full, full, pl.BlockSpec((s, 1), lambda i: (0, 0)), tap_spec, _vec(wd), _vec(wd)],
        out_specs=[full, full, tap_spec, _vec(wd), _vec(wd)],
        out_shape=[_sds((s, wd)), _sds((s, wd)), _sds(dw.shape), _sds((1, wd)), _sds((1, wd))],
        scratch_shapes=[pltpu.VMEM((halo + s, wd), f32), pltpu.VMEM((s + halo, wd), f32)], name="conv_bwd",
        compiler_params=_params())(dcat, hsm, hsm, hc, xh, rstd, dw, ln_g, ln_b)


_GELU_C = 0.7978845608028654
_GELU_A = 0.044715


def _gelu(x):
    return 0.5 * x * (1.0 + jnp.tanh(_GELU_C * (x + _GELU_A * x * x * x)))


def _gelu_grad(x):
    th = jnp.tanh(_GELU_C * (x + _GELU_A * x * x * x))
    return 0.5 * (1.0 + th) + 0.5 * x * (1.0 - th * th) * _GELU_C * (1.0 + 3.0 * _GELU_A * x * x)


def _causal_sg_w(w_ref, g):
    row = lax.broadcasted_iota(jnp.int32, (SG_CHUNK, SG_CHUNK), 0)
    col = lax.broadcasted_iota(jnp.int32, (SG_CHUNK, SG_CHUNK), 1)
    return jnp.where(col <= row, w_ref[g], 0.0).astype(bf16), col <= row


def _gmlp_fwd(hsm, ln_g, ln_b, sg_w, sg_bt):
    _, s, wd = hsm.shape
    ck = SG_CHUNK

    def body(zu_ref, zv_ref, lg_ref, lb_ref, w_ref, bt_ref, out_ref, xh_ref, rs_ref):
        u = _gelu(zu_ref[...])
        vg = _gelu(zv_ref[...])
        mu = jnp.mean(vg, axis=-1, keepdims=True)
        cen = vg - mu
        rstd = lax.rsqrt(jnp.mean(cen * cen, axis=-1, keepdims=True) + LN_EPS)
        xh = cen * rstd
        xh_ref[...] = xh
        rs_ref[...] = rstd
        vn = (xh * lg_ref[...] + lb_ref[...]).astype(bf16)
        for g in range(4):
            cols = slice(g * GROUP_DIM, (g + 1) * GROUP_DIM)
            wm, _ = _causal_sg_w(w_ref, g)
            sv = _dot(wm, vn[:, cols]) + bt_ref[:, g:g + 1]
            out_ref[:, cols] = u[:, cols] * sv

    rows_spec = pl.BlockSpec((ck, wd), lambda i: (i, 0))
    return pl.pallas_call(
        body, grid=(s // ck,),
        in_specs=[pl.BlockSpec((None, ck, wd), lambda i: (2, i, 0)), pl.BlockSpec((None, ck, wd), lambda i: (3, i, 0)),
                  _vec(wd), _vec(wd), pl.BlockSpec(sg_w.shape, lambda i: (0, 0, 0)),
                  pl.BlockSpec(sg_bt.shape, lambda i: (0, 0))],
        out_specs=[rows_spec, rows_spec, pl.BlockSpec((ck, 1), lambda i: (i, 0))],
        out_shape=[_sds((s, wd)), _sds((s, wd)), _sds((s, 1))], name="gmlp_fwd",
        compiler_params=_params())(hsm, hsm, ln_g, ln_b, sg_w, sg_bt)


def _gmlp_bwd(dcat, hsm, xh, rstd, ln_g, ln_b, sg_w, sg_bt):
    s, wd = xh.shape
    ck = SG_CHUNK

    def body(dd_ref, zu_ref, zv_ref, xh_ref, rs_ref, lg_ref, lb_ref, w_ref, bt_ref,
             dzu_ref, dzv_ref, dw_ref, dbb_ref, dlg_ref, dlb_ref):
        @pl.when(pl.program_id(0) == 0)
        def _():
            dw_ref[...] = jnp.zeros_like(dw_ref)
            dbb_ref[...] = jnp.zeros_like(dbb_ref)
            dlg_ref[...] = jnp.zeros_like(dlg_ref)
            dlb_ref[...] = jnp.zeros_like(dlb_ref)

        zu, zv, dd, xhv = zu_ref[...], zv_ref[...], dd_ref[...], xh_ref[...]
        u = _gelu(zu)
        vn = (xhv * lg_ref[...] + lb_ref[...]).astype(bf16)
        du_parts, dvn_parts = [], []
        for g in range(4):
            cols = slice(g * GROUP_DIM, (g + 1) * GROUP_DIM)
            wm, keep = _causal_sg_w(w_ref, g)
            sv = _dot(wm, vn[:, cols]) + bt_ref[:, g:g + 1]
            du_parts.append(dd[:, cols] * sv)
            dsv = dd[:, cols] * u[:, cols]
            dsvb = dsv.astype(bf16)
            dbb_ref[g] += jnp.broadcast_to(jnp.sum(dsv, axis=1, keepdims=True), (ck, GROUP_DIM))
            dw_ref[g] += jnp.where(keep, _dot(dsvb, vn[:, cols], _NT), 0.0)
            dvn_parts.append(_dot(wm, dsvb, _TN))
        du = jnp.concatenate(du_parts, axis=1)
        dvn = jnp.concatenate(dvn_parts, axis=1)
        dlg_ref[...] += jnp.sum(dvn * xhv, axis=0, keepdims=True)
        dlb_ref[...] += jnp.sum(dvn, axis=0, keepdims=True)
        dzv_ref[...] = _ln_bwd_rows(dvn, xhv, rs_ref[...], lg_ref[...]) * _gelu_grad(zv)
        dzu_ref[...] = du * _gelu_grad(zu)

    rows_spec = pl.BlockSpec((ck, wd), lambda i: (i, 0))
    wspec = pl.BlockSpec(sg_w.shape, lambda i: (0, 0, 0))
    return pl.pallas_call(
        body, grid=(s // ck,),
        in_specs=[pl.BlockSpec((ck, wd), lambda i: (i, 1)), pl.BlockSpec((None, ck, wd), lambda i: (2, i, 0)),
                  pl.BlockSpec((None, ck, wd), lambda i: (3, i, 0)), rows_spec, pl.BlockSpec((ck, 1), lambda i: (i, 0)),
                  _vec(wd), _vec(wd), wspec, pl.BlockSpec(sg_bt.shape, lambda i: (0, 0))],
        out_specs=[rows_spec, rows_spec, wspec, wspec, _vec(wd), _vec(wd)],
        out_shape=[_sds((s, wd)), _sds((s, wd)), _sds(sg_w.shape), _sds(sg_w.shape), _sds((1, wd)), _sds((1, wd))],
        name="gmlp_bwd", compiler_params=_params())(dcat, hsm, hsm, xh, rstd, ln_g, ln_b, sg_w, sg_bt)


def _to_heads(x2d):
    s = x2d.shape[0]
    return jnp.transpose(x2d.reshape(s, N_HEADS, HEAD_DIM), (1, 0, 2))


def _from_heads(x3d):
    s = x3d.shape[1]
    return jnp.transpose(x3d, (1, 0, 2)).reshape(s, N_HEADS * HEAD_DIM)


def _local_step(x, p, target, wts, small):
    saved = []
    for i in range(DEPTH):
        tag = f"l{i}"
        if i % 2 == 0:
            hsm = _mm_nn_col(tag + "_in", x, wts["even_w_in"], 0)
            q, k, v = _to_heads(hsm[0]), _to_heads(hsm[1]), _to_heads(hsm[2])
            att = _from_heads(_sba_fwd(q, k, v))
            pool_out, pooled = _pool_fwd(hsm, small["pool_w"], small["pool_scale"])
            cat = jnp.concatenate([att, pool_out], axis=1)
            mix = _mm_nn_row(tag + "_out", cat, wts["even_w_out"], 0)
            mixer_saved = (hsm, q, k, v, pooled, cat)
        else:
            hsm = _mm_nn_col(tag + "_in", x, wts["odd_w_in"], 0)
            conv_out, hc, cxh, crs = _conv_fwd(hsm, small["conv_dw"], small["conv_ln_g"], small["conv_ln_b"])
            sg_out, sxh, srs = _gmlp_fwd(hsm, small["sg_ln_g"], small["sg_ln_b"], small["sg_w"], small["sg_bt"])
            cat = jnp.concatenate([conv_out, sg_out], axis=1)
            mix = _mm_nn_row(tag + "_out", cat, wts["odd_w_out"], 0)
            mixer_saved = (hsm, hc, cxh, crs, sxh, srs, cat)
        x1, xh1, rs1 = _ln_fwd(tag + "_ln_mix", x, mix, small["ln_mix_g"][i:i + 1], small["ln_mix_b"][i:i + 1])
        hg = _mm_nn_col(tag + "_gate", x1, wts["ffn_w_gate"], i)
        hu = _mm_nn_col(tag + "_up", x1, wts["ffn_w_up"], i)
        act = _swiglu_fwd(tag + "_swiglu", hg, hu)
        ffn = _mm_nn_row(tag + "_down", act, wts["ffn_w_down"], i)
        x2, xh2, rs2 = _ln_fwd(tag + "_ln_ffn", x1, ffn, small["ln_ffn_g"][i:i + 1], small["ln_ffn_b"][i:i + 1])
        gp = _mm_nn_row(tag + "_ple_gate", x2, wts["ple_w_gate"], i)
        pp = _mm_nn_col(tag + "_ple_proj", p[i], wts["ple_w_proj"], i, natural=True)
        x3 = _ple_fwd(tag + "_ple", x2, gp, small["ple_b_gate"][i:i + 1], pp)
        saved.append((x, mixer_saved, x1, xh1, rs1, hg, hu, act, x2, xh2, rs2, gp, pp))
        x = x3

    loss_part, dx = _loss_head("loss_head", x, target)

    big = {n: [None] * wts[n].shape[1] for n in wts}
    sm = {}
    per_layer = {n: [None] * DEPTH for n in ("ln_mix_g", "ln_mix_b", "ln_ffn_g", "ln_ffn_b", "ple_b_gate")}
    for i in reversed(range(DEPTH)):
        tag = f"l{i}b"
        x0, mixer_saved, x1, xh1, rs1, hg, hu, act, x2, xh2, rs2, gp, pp = saved[i]
        dgp, dpp, per_layer["ple_b_gate"][i] = _ple_bwd(tag + "_ple", dx, gp, small["ple_b_gate"][i:i + 1], pp)
        big["ple_w_proj"][i] = _mm_tn_col(tag + "_dproj", p[i], dpp)
        big["ple_w_gate"][i] = _mm_tn_row(tag + "_dgate", x2, dgp)
        dx2 = _mm_nt_row(tag + "_dx2", dgp, wts["ple_w_gate"], i, natural=True, add=dx)
        dr2, per_layer["ln_ffn_g"][i], per_layer["ln_ffn_b"][i] = _ln_bwd(tag + "_ln_ffn", dx2, xh2, rs2,
                                                                           small["ln_ffn_g"][i:i + 1])
        dact = _mm_nt_row(tag + "_dact", dr2, wts["ffn_w_down"], i)
        big["ffn_w_down"][i] = _mm_tn_row(tag + "_ddown", act, dr2)
        dhg, dhu = _swiglu_bwd(tag + "_swiglu", dact, hg, hu)
        big["ffn_w_gate"][i] = _mm_tn_col(tag + "_dgatew", x1, dhg)
        big["ffn_w_up"][i] = _mm_tn_col(tag + "_dupw", x1, dhu)
        part = _mm_nt_col(tag + "_dx1a", dhg, wts["ffn_w_gate"], i, dr2, ALPHA)
        dx1 = _mm_nt_col(tag + "_dx1b", dhu, wts["ffn_w_up"], i, part, 1.0)
        dr1, per_layer["ln_mix_g"][i], per_layer["ln_mix_b"][i] = _ln_bwd(tag + "_ln_mix", dx1, xh1, rs1,
                                                                           small["ln_mix_g"][i:i + 1])
        if i % 2 == 0:
            hsm, q, k, v, pooled, cat = mixer_saved
            big["even_w_out"][0] = _mm_tn_row(tag + "_dout", cat, dr1)
            dcat = _mm_nt_row(tag + "_dcat", dr1, wts["even_w_out"], 0, natural=True)
            dq, dk, dv = _sba_bwd(q, k, v, _to_heads(dcat[:, :N_HEADS * HEAD_DIM]))
            du, sm["pool_w"], sm["pool_scale"] = _pool_bwd(dcat, pooled, small["pool_w"], small["pool_scale"])
            dhsm = jnp.stack([_from_heads(dq), _from_heads(dk), _from_heads(dv), du])
            w_in = "even_w_in"
        else:
            hsm, hc, cxh, crs, sxh, srs, cat = mixer_saved
            big["odd_w_out"][0] = _mm_tn_row(tag + "_dout", cat, dr1)
            dcat = _mm_nt_row(tag + "_dcat", dr1, wts["odd_w_out"], 0, natural=True)
            da, dg, sm["conv_dw"], sm["conv_ln_g"], sm["conv_ln_b"] = _conv_bwd(
                dcat, hsm, hc, cxh, crs, small["conv_dw"], small["conv_ln_g"], small["conv_ln_b"])
            dzu, dzv, sm["sg_w"], dsgb, sm["sg_ln_g"], sm["sg_ln_b"] = _gmlp_bwd(
                dcat, hsm, sxh, srs, small["sg_ln_g"], small["sg_ln_b"], small["sg_w"], small["sg_bt"])
            sm["sg_b"] = dsgb[:, :, 0]
            dhsm = jnp.stack([da, dg, dzu, dzv])
            w_in = "odd_w_in"
        big[w_in][0] = _mm_tn_col(tag + "_din", x0, dhsm)
        dx = _mm_nt_col(tag + "_dx", dhsm, wts[w_in], 0, dr1, ALPHA)
    for n, parts in per_layer.items():
        sm[n] = jnp.concatenate(parts, axis=0)
    return loss_part, dx, big, sm


def _place():
    x, y, c = lax.axis_index("x"), lax.axis_index("y"), lax.axis_index("c")
    return x, y, c, [(1 - x, y), (x, 1 - y), (1 - x, 1 - y)]


def _gather_chips(shards):
    n = len(shards)

    def body(*refs):
        ins, outs = refs[:n], refs[n:2 * n]
        send, recv, loc = refs[2 * n:]
        x, y, c, chips = _place()
        mine = 2 * x + y
        started = []
        for t in range(n):
            cp = pltpu.make_async_copy(ins[t], outs[t].at[mine], loc.at[t])
            cp.start()
            started.append(cp)
            for j, (cx, cy) in enumerate(chips):
                rc = pltpu.make_async_remote_copy(src_ref=ins[t], dst_ref=outs[t].at[mine], send_sem=send.at[t, j],
                                                  recv_sem=recv.at[t, j], device_id=(cx, cy, c), device_id_type=_MESH)
                rc.start()
                started.append(rc)
        for cp in started:
            cp.wait()

    return pl.pallas_call(
        body, in_specs=[_ANY] * n, out_specs=[_ANY] * n,
        out_shape=[_sds((N_CHIPS,) + a.shape, a.dtype) for a in shards],
        scratch_shapes=[pltpu.SemaphoreType.DMA((n, 3)), pltpu.SemaphoreType.DMA((n, 3)), pltpu.SemaphoreType.DMA((n,))],
        name="gather_chips")(*shards)


def _scatter_chips(grads, layout):
    n_in = len(grads)
    shapes = {}
    for e, (pi, li) in enumerate(layout):
        r, cdim = grads[e].shape[1:]
        shapes[pi] = (N_CHIPS, max(li + 1, shapes.get(pi, (0, 0))[1]), r, cdim)
    n_out = len(shapes)

    def body(*refs):
        ins, outs = refs[:n_in], refs[n_in:n_in + n_out]
        send, recv, loc = refs[n_in + n_out:]
        x, y, c, chips = _place()
        mine = 2 * x + y
        started = []
        for e, (pi, li) in enumerate(layout):
            cp = pltpu.make_async_copy(ins[e].at[mine], outs[pi].at[mine, li], loc.at[e])
            cp.start()
            started.append(cp)
            for j, (cx, cy) in enumerate(chips):
                rc = pltpu.make_async_remote_copy(src_ref=ins[e].at[2 * cx + cy], dst_ref=outs[pi].at[mine, li],
                                                  send_sem=send.at[e, j], recv_sem=recv.at[e, j],
                                                  device_id=(cx, cy, c), device_id_type=_MESH)
                rc.start()
                started.append(rc)
        for cp in started:
            cp.wait()

    return pl.pallas_call(
        body, in_specs=[_ANY] * n_in, out_specs=[_ANY] * n_out,
        out_shape=[_sds(shapes[pi]) for pi in range(n_out)],
        scratch_shapes=[pltpu.SemaphoreType.DMA((n_in, 3)), pltpu.SemaphoreType.DMA((n_in, 3)),
                        pltpu.SemaphoreType.DMA((n_in,))],
        name="scatter_chips")(*grads)


def _swap_cores(arrays):
    n = len(arrays)

    def body(*refs):
        ins, outs = refs[:n], refs[n:2 * n]
        send, recv = refs[2 * n:]
        x, y, c, _ = _place()
        started = []
        for t in range(n):
            rc = pltpu.make_async_remote_copy(src_ref=ins[t], dst_ref=outs[t], send_sem=send.at[t], recv_sem=recv.at[t],
                                              device_id=(x, y, 1 - c), device_id_type=_MESH)
            rc.start()
            started.append(rc)
        for rc in started:
            rc.wait()

    return pl.pallas_call(
        body, in_specs=[_ANY] * n, out_specs=[_ANY] * n, out_shape=[_sds(a.shape, a.dtype) for a in arrays],
        scratch_shapes=[pltpu.SemaphoreType.DMA((n,)), pltpu.SemaphoreType.DMA((n,))], name="swap_cores")(*arrays)


def _gather_all(block):
    def body(in_ref, out_ref, send, recv, loc):
        x, y, c, _ = _place()
        mine = 4 * x + 2 * y + c
        cp = pltpu.make_async_copy(in_ref, out_ref.at[mine], loc)
        cp.start()
        started = [cp]
        for m in range(1, N_DEV):
            fx, fy, fc = (m >> 2) & 1, (m >> 1) & 1, m & 1
            peer = (x + fx - 2 * x * fx, y + fy - 2 * y * fy, c + fc - 2 * c * fc)
            rc = pltpu.make_async_remote_copy(src_ref=in_ref, dst_ref=out_ref.at[mine], send_sem=send.at[m - 1],
                                              recv_sem=recv.at[m - 1], device_id=peer, device_id_type=_MESH)
            rc.start()
            started.append(rc)
        for rc in started:
            rc.wait()

    return pl.pallas_call(
        body, in_specs=[_ANY], out_specs=_ANY, out_shape=_sds((N_DEV,) + block.shape),
        scratch_shapes=[pltpu.SemaphoreType.DMA((N_DEV - 1,)), pltpu.SemaphoreType.DMA((N_DEV - 1,)),
                        pltpu.SemaphoreType.DMA(())], name="gather_all")(block)


def _row_tile(r):
    for t in (256, 128, 64, 32, 16, 8):
        if r % t == 0:
            return t
    return r


def _sum_stack(name, stack):
    n, r, c = stack.shape
    tr = _row_tile(r)

    def body(s_ref, o_ref):
        acc = s_ref[0]
        for t in range(1, n):
            acc = acc + s_ref[t]
        o_ref[...] = acc

    return pl.pallas_call(body, grid=(r // tr,), in_specs=[pl.BlockSpec((n, tr, c), lambda i: (0, i, 0))],
                          out_specs=pl.BlockSpec((tr, c), lambda i: (i, 0)), out_shape=_sds((r, c)), name=name,
                          compiler_params=_params())(stack)


def _adamw(name, w, g_a, g_b, m, v):
    r, c = w.shape
    tr = _row_tile(r)
    two = g_b is not None
    bc1 = 1.0 - ADAM_B1 ** ADAM_STEP
    bc2 = 1.0 - ADAM_B2 ** ADAM_STEP

    def body(*refs):
        if two:
            w_ref, ga_ref, gb_ref, m_ref, v_ref, g_out, d_out, m_out, v_out = refs
            g = ga_ref[...] + gb_ref[...]
        else:
            w_ref, ga_ref, m_ref, v_ref, g_out, d_out, m_out, v_out = refs
            g = ga_ref[...]
        m_new = ADAM_B1 * m_ref[...] + (1.0 - ADAM_B1) * g
        v_new = ADAM_B2 * v_ref[...] + (1.0 - ADAM_B2) * (g * g)
        g_out[...] = g
        m_out[...] = m_new
        v_out[...] = v_new
        d_out[...] = -ADAM_LR * ((m_new / bc1) / (jnp.sqrt(v_new / bc2) + ADAM_EPS) + ADAM_WD * w_ref[...])

    spec = pl.BlockSpec((tr, c), lambda i: (i, 0))
    ins = [w, g_a] + ([g_b] if two else []) + [m, v]
    return pl.pallas_call(body, grid=(r // tr,), in_specs=[spec] * len(ins), out_specs=[spec] * 4,
                          out_shape=[_sds((r, c))] * 4, name=name, compiler_params=_params())(*ins)


_BIG = ("even_w_in", "even_w_out", "odd_w_in", "odd_w_out", "ffn_w_gate", "ffn_w_up", "ffn_w_down", "ple_w_proj",
        "ple_w_gate")
_SHARDED_SMALL = ("conv_dw", "conv_ln_g", "conv_ln_b", "sg_ln_g", "sg_ln_b")
_SMALL = ("pool_w", "pool_scale", "conv_dw", "conv_ln_g", "conv_ln_b", "sg_ln_g", "sg_ln_b", "sg_w", "sg_b",
          "ln_mix_g", "ln_mix_b", "ln_ffn_g", "ln_ffn_b", "ple_b_gate")
_WEIGHTS = ("even_w_in", "even_w_out", "pool_w", "pool_scale", "odd_w_in", "odd_w_out", "conv_dw", "conv_ln_g",
            "conv_ln_b", "sg_ln_g", "sg_ln_b", "sg_w", "sg_b", "ln_mix_g", "ln_mix_b", "ffn_w_gate", "ffn_w_up",
            "ffn_w_down", "ln_ffn_g", "ln_ffn_b", "ple_w_proj", "ple_w_gate", "ple_b_gate")


def _pack(arrays):
    flat = jnp.concatenate([a.reshape(-1) for a in arrays])
    pad = (-flat.shape[0]) % (8 * LANES)
    return jnp.pad(flat, (0, pad)).reshape(-1, LANES)


def _unpack(packed, shapes):
    flat = packed.reshape(-1)
    out, off = [], 0
    for shp in shapes:
        size = 1
        for dim in shp:
            size *= dim
        out.append(flat[off:off + size].reshape(shp))
        off += size
    return out


def _unshard_last(g4):
    return jnp.concatenate([g4[k] for k in range(N_CHIPS)], axis=-1)


def kernel(x, p, even_w_in, even_w_out, pool_w, pool_scale, odd_w_in, odd_w_out, conv_dw, conv_ln_g, conv_ln_b, sg_ln_g, sg_ln_b, sg_w, sg_b, ln_mix_g, ln_mix_b, ffn_w_gate, ffn_w_up, ffn_w_down, ln_ffn_g, ln_ffn_b, ple_w_proj, ple_w_gate, ple_b_gate, loss_target, m_even_w_in, m_even_w_out, m_pool_w, m_pool_scale, m_odd_w_in, m_odd_w_out, m_conv_dw, m_conv_ln_g, m_conv_ln_b, m_sg_ln_g, m_sg_ln_b, m_sg_w, m_sg_b, m_ln_mix_g, m_ln_mix_b, m_ffn_w_gate, m_ffn_w_up, m_ffn_w_down, m_ln_ffn_g, m_ln_ffn_b, m_ple_w_proj, m_ple_w_gate, m_ple_b_gate, v_even_w_in, v_even_w_out, v_pool_w, v_pool_scale, v_odd_w_in, v_odd_w_out, v_conv_dw, v_conv_ln_g, v_conv_ln_b, v_sg_ln_g, v_sg_ln_b, v_sg_w, v_sg_b, v_ln_mix_g, v_ln_mix_b, v_ffn_w_gate, v_ffn_w_up, v_ffn_w_down, v_ln_ffn_g, v_ln_ffn_b, v_ple_w_proj, v_ple_w_gate, v_ple_b_gate):
    w = dict(even_w_in=even_w_in, even_w_out=even_w_out, pool_w=pool_w, pool_scale=pool_scale, odd_w_in=odd_w_in,
             odd_w_out=odd_w_out, conv_dw=conv_dw, conv_ln_g=conv_ln_g, conv_ln_b=conv_ln_b, sg_ln_g=sg_ln_g,
             sg_ln_b=sg_ln_b, sg_w=sg_w, sg_b=sg_b, ln_mix_g=ln_mix_g, ln_mix_b=ln_mix_b, ffn_w_gate=ffn_w_gate,
             ffn_w_up=ffn_w_up, ffn_w_down=ffn_w_down, ln_ffn_g=ln_ffn_g, ln_ffn_b=ln_ffn_b, ple_w_proj=ple_w_proj,
             ple_w_gate=ple_w_gate, ple_b_gate=ple_b_gate)
    mom = dict(even_w_in=m_even_w_in, even_w_out=m_even_w_out, pool_w=m_pool_w, pool_scale=m_pool_scale,
               odd_w_in=m_odd_w_in, odd_w_out=m_odd_w_out, conv_dw=m_conv_dw, conv_ln_g=m_conv_ln_g,
               conv_ln_b=m_conv_ln_b, sg_ln_g=m_sg_ln_g, sg_ln_b=m_sg_ln_b, sg_w=m_sg_w, sg_b=m_sg_b,
               ln_mix_g=m_ln_mix_g, ln_mix_b=m_ln_mix_b, ffn_w_gate=m_ffn_w_gate, ffn_w_up=m_ffn_w_up,
               ffn_w_down=m_ffn_w_down, ln_ffn_g=m_ln_ffn_g, ln_ffn_b=m_ln_ffn_b, ple_w_proj=m_ple_w_proj,
               ple_w_gate=m_ple_w_gate, ple_b_gate=m_ple_b_gate)
    var = dict(even_w_in=v_even_w_in, even_w_out=v_even_w_out, pool_w=v_pool_w, pool_scale=v_pool_scale,
               odd_w_in=v_odd_w_in, odd_w_out=v_odd_w_out, conv_dw=v_conv_dw, conv_ln_g=v_conv_ln_g,
               conv_ln_b=v_conv_ln_b, sg_ln_g=v_sg_ln_g, sg_ln_b=v_sg_ln_b, sg_w=v_sg_w, sg_b=v_sg_b,
               ln_mix_g=v_ln_mix_g, ln_mix_b=v_ln_mix_b, ffn_w_gate=v_ffn_w_gate, ffn_w_up=v_ffn_w_up,
               ffn_w_down=v_ffn_w_down, ln_ffn_g=v_ln_ffn_g, ln_ffn_b=v_ln_ffn_b, ple_w_proj=v_ple_w_proj,
               ple_w_gate=v_ple_w_gate, ple_b_gate=v_ple_b_gate)

    gathered = _gather_chips([w[n].astype(bf16) for n in _BIG] + [w[n] for n in _SHARDED_SMALL])
    wts = dict(zip(_BIG, gathered[:len(_BIG)]))
    small = {n: w[n][0] for n in ("pool_w", "sg_w")}
    small.update({n: w[n] for n in ("pool_scale", "ln_mix_g", "ln_mix_b", "ln_ffn_g", "ln_ffn_b", "ple_b_gate")})
    small["sg_bt"] = jnp.transpose(w["sg_b"][0])
    for n, g4 in zip(_SHARDED_SMALL, gathered[len(_BIG):]):
        small[n] = _unshard_last(g4)[0]
        if n != "conv_dw":
            small[n] = small[n][None]

    loss_part, grad_x, big, sm = _local_step(x[0], p[:, 0], loss_target[0], wts, small)

    entries, layout = [], []
    for pi, n in enumerate(_BIG):
        for li, g in enumerate(big[n]):
            entries.append(g)
            layout.append((pi, li))
    received = _scatter_chips(entries, layout)
    chip_sums = []
    for n, r4 in zip(_BIG, received):
        _, nl, r, c = r4.shape
        chip_sums.append(_sum_stack("sum_" + n, r4.reshape(N_CHIPS, nl * r, c)))
    other = _swap_cores(chip_sums)
    results = {}
    for n, mine, theirs in zip(_BIG, chip_sums, other):
        shp = w[n].shape
        flat = (shp[0] * shp[1], shp[2])
        outs = _adamw("adamw_" + n, w[n].reshape(flat), mine, theirs, mom[n].reshape(flat), var[n].reshape(flat))
        results[n] = [o.reshape(shp) for o in outs]

    sm_shapes = [(1,) + sm[n].shape if n in ("pool_w", "sg_w", "conv_dw", "sg_b") else sm[n].shape for n in _SMALL]
    packed = _pack([sm[n] for n in _SMALL] + [loss_part[0, 0:1]])
    total = _sum_stack("sum_small", _gather_all(packed))
    parts = _unpack(total, sm_shapes + [(1,)])
    loss = parts[-1][0]
    chip = 2 * lax.axis_index("x") + lax.axis_index("y")
    g_small = {}
    for n, g in zip(_SMALL, parts[:-1]):
        if n in _SHARDED_SMALL:
            width = w[n].shape[-1]
            g = lax.dynamic_slice_in_dim(g, chip * width, width, axis=g.ndim - 1)
        g_small[n] = g
    shapes = [w[n].shape for n in _SMALL]
    outs = _adamw("adamw_small", _pack([w[n] for n in _SMALL]), _pack([g_small[n] for n in _SMALL]), None,
                  _pack([mom[n] for n in _SMALL]), _pack([var[n] for n in _SMALL]))
    unpacked = [_unpack(o, shapes) for o in outs]
    for idx, n in enumerate(_SMALL):
        results[n] = [u[idx] for u in unpacked]

    return (loss, grad_x[None], *[results[n][0] for n in _WEIGHTS], *[results[n][1] for n in _WEIGHTS],
            *[results[n][2] for n in _WEIGHTS], *[results[n][3] for n in _WEIGHTS])
```

```python
import jax
import jax.numpy as jnp
from jax import lax
from jax.experimental import pallas as pl
from jax.experimental.pallas import tpu as pltpu

f32 = jnp.float32
bf16 = jnp.bfloat16

D_MODEL = 1024
N_HEADS = 8
HEAD_DIM = 64
Q_BLOCK = 128
POOL_WINDOWS = (2, 4, 8, 16)
GROUP_DIM = 128
CONV_TAPS = 31
SG_CHUNK = 128
DEPTH = 2
ALPHA = (2 * DEPTH) ** 0.25
LN_EPS = 1e-5
SB_SCALE = HEAD_DIM ** -0.5
ADAM_LR, ADAM_B1, ADAM_B2, ADAM_EPS, ADAM_WD, ADAM_STEP = 0.001, 0.9, 0.999, 1e-08, 0.01, 10
N_CHIPS = 4
N_DEV = 8
LANES = 128
VMEM_LIMIT = 56 * 1024 * 1024
TM = 512
TR = 256
SBA_HEADS_PER_STEP = 4
CONV_ROWS = 64
CONV_HALO = 32
POOL_HALO = 16

_NN = (((1,), (0,)), ((), ()))
_NT = (((1,), (1,)), ((), ()))
_TN = (((0,), (0,)), ((), ()))
_ANY = pl.BlockSpec(memory_space=pl.ANY)
_MESH = pl.DeviceIdType.MESH


def _params():
    return pltpu.CompilerParams(vmem_limit_bytes=VMEM_LIMIT)


def _sds(shape, dtype=f32):
    return jax.ShapeDtypeStruct(tuple(shape), dtype)


def _dot(a, b, dims=_NN):
    return lax.dot_general(a, b, dims, preferred_element_type=f32)


def _sigmoid(x):
    return 1.0 / (1.0 + jnp.exp(-x))


def _mm(name, a, b, grid, a_spec, b_spec, out_shape, out_spec, dims, reduce=False, add=None, add_spec=None,
        add_scale=1.0, out_dtype=f32):
    has_add = add is not None
    k_axis = len(grid) - 1

    def body(*refs):
        if has_add:
            a_ref, b_ref, add_ref, o_ref = refs
        else:
            a_ref, b_ref, o_ref = refs
        r = _dot(a_ref[...].astype(bf16), b_ref[...].astype(bf16), dims)
        if reduce:
            k = pl.program_id(k_axis)

            @pl.when(k == 0)
            def _():
                o_ref[...] = r + add_scale * add_ref[...] if has_add else r

            @pl.when(k > 0)
            def _():
                o_ref[...] += r
        else:
            o_ref[...] = (r + add_scale * add_ref[...] if has_add else r).astype(out_dtype)

    ins = [a, b] + ([add] if has_add else [])
    specs = [a_spec, b_spec] + ([add_spec] if has_add else [])
    return pl.pallas_call(body, grid=grid, in_specs=specs, out_specs=out_spec, out_shape=_sds(out_shape, out_dtype),
                          name=name, compiler_params=_params())(*ins)


def _tm(s):
    return min(TM, s)


def _act_spec(a, tm, width):
    if a.ndim == 3:
        return pl.BlockSpec((None, tm, width), lambda i, k: (k, i, 0))
    return pl.BlockSpec((tm, width), lambda i, k: (i, k))


def _mm_nn_col(name, x, w4, layer, natural=False):
    s, kk = x.shape
    nq = w4.shape[3]
    tm = _tm(s)
    if natural:
        out_shape, out_spec = (s, 4 * nq), pl.BlockSpec((tm, nq), lambda i, k: (i, k))
    else:
        out_shape, out_spec = (4, s, nq), pl.BlockSpec((None, tm, nq), lambda i, k: (k, i, 0))
    return _mm(name, x, w4, (s // tm, 4), pl.BlockSpec((tm, kk), lambda i, k: (i, 0)),
               pl.BlockSpec((None, None, kk, nq), lambda i, k: (k, layer, 0, 0)), out_shape, out_spec, _NN)


def _mm_nn_row(name, a, w4, layer):
    s = a.shape[-2]
    kq, n = w4.shape[2], w4.shape[3]
    tm = _tm(s)
    return _mm(name, a, w4, (s // tm, 4), _act_spec(a, tm, kq),
               pl.BlockSpec((None, None, kq, n), lambda i, k: (k, layer, 0, 0)), (s, n),
               pl.BlockSpec((tm, n), lambda i, k: (i, 0)), _NN, reduce=True)


def _mm_nt_col(name, dh, w4, layer, add, add_scale):
    s = dh.shape[-2]
    kk, nq = w4.shape[2], w4.shape[3]
    tm = _tm(s)
    row = pl.BlockSpec((tm, kk), lambda i, k: (i, 0))
    return _mm(name, dh, w4, (s // tm, 4), _act_spec(dh, tm, nq),
               pl.BlockSpec((None, None, kk, nq), lambda i, k: (k, layer, 0, 0)), (s, kk), row, _NT,
               reduce=True, add=add, add_spec=row, add_scale=add_scale)


def _mm_nt_row(name, dy, w4, layer, natural=False, add=None):
    s, n = dy.shape
    kq = w4.shape[2]
    tm = _tm(s)
    if natural:
        out_shape, out_spec = (s, 4 * kq), pl.BlockSpec((tm, kq), lambda i, k: (i, k))
    else:
        out_shape, out_spec = (4, s, kq), pl.BlockSpec((None, tm, kq), lambda i, k: (k, i, 0))
    return _mm(name, dy, w4, (s // tm, 4), pl.BlockSpec((tm, n), lambda i, k: (i, 0)),
               pl.BlockSpec((None, None, kq, n), lambda i, k: (k, layer, 0, 0)), out_shape, out_spec, _NT,
               add=add, add_spec=out_spec if add is not None else None)


def _mm_tn_col(name, x, dh):
    s, kk = x.shape
    if dh.ndim == 3:
        nq = dh.shape[2]
        b_spec = pl.BlockSpec((None, s, nq), lambda k, j: (k, 0, 0))
    else:
        nq = dh.shape[1] // 4
        b_spec = pl.BlockSpec((s, nq), lambda k, j: (0, k))
    tk = min(256, kk)
    return _mm(name, x, dh, (4, kk // tk), pl.BlockSpec((s, tk), lambda k, j: (0, j)), b_spec, (4, kk, nq),
               pl.BlockSpec((None, tk, nq), lambda k, j: (k, j, 0)), _TN, out_dtype=bf16)


def _mm_tn_row(name, a, dy):
    s, n = dy.shape
    if a.ndim == 3:
        kq = a.shape[2]
        a_spec = pl.BlockSpec((None, s, kq), lambda k, j: (k, 0, 0))
    else:
        kq = a.shape[1] // 4
        a_spec = pl.BlockSpec((s, kq), lambda k, j: (0, k))
    tn = min(512, n)
    return _mm(name, a, dy, (4, n // tn), a_spec, pl.BlockSpec((s, tn), lambda k, j: (0, j)), (4, kq, n),
               pl.BlockSpec((None, kq, tn), lambda k, j: (k, 0, j)), _TN, out_dtype=bf16)


def _tr(s):
    return min(TR, s)


def _rows(tm, d):
    return pl.BlockSpec((tm, d), lambda i: (i, 0))


def _vec(d):
    return pl.BlockSpec((1, d), lambda i: (0, 0))


def _ln_fwd(name, x, mix, g, b):
    s, d = x.shape
    tm = _tr(s)

    def body(x_ref, m_ref, g_ref, b_ref, y_ref, xh_ref, rs_ref):
        r = ALPHA * x_ref[...] + m_ref[...]
        mu = jnp.mean(r, axis=-1, keepdims=True)
        c = r - mu
        rstd = lax.rsqrt(jnp.mean(c * c, axis=-1, keepdims=True) + LN_EPS)
        xh = c * rstd
        y_ref[...] = xh * g_ref[...] + b_ref[...]
        xh_ref[...] = xh
        rs_ref[...] = rstd

    return pl.pallas_call(
        body, grid=(s // tm,), in_specs=[_rows(tm, d), _rows(tm, d), _vec(d), _vec(d)],
        out_specs=[_rows(tm, d), _rows(tm, d), _rows(tm, 1)],
        out_shape=[_sds((s, d)), _sds((s, d)), _sds((s, 1))], name=name, compiler_params=_params())(x, mix, g, b)


def _ln_bwd_rows(dy, xh, rstd, g):
    dxh = dy * g
    m1 = jnp.mean(dxh, axis=-1, keepdims=True)
    m2 = jnp.mean(dxh * xh, axis=-1, keepdims=True)
    return rstd * (dxh - m1 - xh * m2)


def _ln_bwd(name, dy, xh, rstd, g):
    s, d = dy.shape
    tm = _tr(s)

    def body(dy_ref, xh_ref, rs_ref, g_ref, dr_ref, dg_ref, db_ref):
        @pl.when(pl.program_id(0) == 0)
        def _():
            dg_ref[...] = jnp.zeros_like(dg_ref)
            db_ref[...] = jnp.zeros_like(db_ref)

        dyv, xhv = dy_ref[...], xh_ref[...]
        dr_ref[...] = _ln_bwd_rows(dyv, xhv, rs_ref[...], g_ref[...])
        dg_ref[...] += jnp.sum(dyv * xhv, axis=0, keepdims=True)
        db_ref[...] += jnp.sum(dyv, axis=0, keepdims=True)

    return pl.pallas_call(
        body, grid=(s // tm,), in_specs=[_rows(tm, d), _rows(tm, d), _rows(tm, 1), _vec(d)],
        out_specs=[_rows(tm, d), _vec(d), _vec(d)],
        out_shape=[_sds((s, d)), _sds((1, d)), _sds((1, d))], name=name, compiler_params=_params())(dy, xh, rstd, g)


def _sm_spec(tm, w):
    return pl.BlockSpec((None, tm, w), lambda k, i: (k, i, 0))


def _swiglu_fwd(name, hg, hu):
    _, s, w = hg.shape
    tm = _tr(s)

    def body(g_ref, u_ref, a_ref):
        g = g_ref[...]
        a_ref[...] = g * _sigmoid(g) * u_ref[...]

    return pl.pallas_call(body, grid=(4, s // tm), in_specs=[_sm_spec(tm, w)] * 2, out_specs=_sm_spec(tm, w),
                          out_shape=_sds(hg.shape), name=name, compiler_params=_params())(hg, hu)


def _swiglu_bwd(name, da, hg, hu):
    _, s, w = hg.shape
    tm = _tr(s)

    def body(da_ref, g_ref, u_ref, dg_ref, du_ref):
        g, da_v = g_ref[...], da_ref[...]
        sg = _sigmoid(g)
        du_ref[...] = da_v * g * sg
        dg_ref[...] = da_v * u_ref[...] * sg * (1.0 + g * (1.0 - sg))

    return pl.pallas_call(body, grid=(4, s // tm), in_specs=[_sm_spec(tm, w)] * 3, out_specs=[_sm_spec(tm, w)] * 2,
                          out_shape=[_sds(hg.shape)] * 2, name=name, compiler_params=_params())(da, hg, hu)


def _ple_fwd(name, x2, gp, bias, pp):
    s, d = x2.shape
    tm = _tr(s)

    def body(x_ref, gp_ref, b_ref, pp_ref, y_ref):
        y_ref[...] = x_ref[...] + _sigmoid(gp_ref[...] + b_ref[...]) * pp_ref[...]

    return pl.pallas_call(body, grid=(s // tm,), in_specs=[_rows(tm, d), _rows(tm, d), _vec(d), _rows(tm, d)],
                          out_specs=_rows(tm, d), out_shape=_sds((s, d)), name=name,
                          compiler_params=_params())(x2, gp, bias, pp)


def _ple_bwd(name, dy, gp, bias, pp):
    s, d = dy.shape
    tm = _tr(s)

    def body(dy_ref, gp_ref, b_ref, pp_ref, dgp_ref, dpp_ref, db_ref):
        @pl.when(pl.program_id(0) == 0)
        def _():
            db_ref[...] = jnp.zeros_like(db_ref)

        dyv = dy_ref[...]
        gate = _sigmoid(gp_ref[...] + b_ref[...])
        dgp = dyv * pp_ref[...] * gate * (1.0 - gate)
        dgp_ref[...] = dgp
        dpp_ref[...] = dyv * gate
        db_ref[...] += jnp.sum(dgp, axis=0, keepdims=True)

    return pl.pallas_call(body, grid=(s // tm,), in_specs=[_rows(tm, d), _rows(tm, d), _vec(d), _rows(tm, d)],
                          out_specs=[_rows(tm, d), _rows(tm, d), _vec(d)],
                          out_shape=[_sds((s, d)), _sds((s, d)), _sds((1, d))], name=name,
                          compiler_params=_params())(dy, gp, bias, pp)


def _loss_head(name, y, target):
    s, d = y.shape
    tm = _tr(s)

    def body(y_ref, t_ref, l_ref, dy_ref):
        @pl.when(pl.program_id(0) == 0)
        def _():
            l_ref[...] = jnp.zeros_like(l_ref)

        e = y_ref[...] - t_ref[...]
        dy_ref[...] = e * (1.0 / d)
        tot = jnp.sum(jnp.sum(e * e, axis=1, keepdims=True), axis=0, keepdims=True) * (0.5 / d)
        l_ref[...] += jnp.broadcast_to(tot, l_ref.shape)

    return pl.pallas_call(body, grid=(s // tm,), in_specs=[_rows(tm, d), _rows(tm, d)],
                          out_specs=[pl.BlockSpec((1, LANES), lambda i: (0, 0)), _rows(tm, d)],
                          out_shape=[_sds((1, LANES)), _sds((s, d))], name=name, compiler_params=_params())(y, target)


def _split_dot(x, m01):
    hi = x.astype(bf16)
    lo = (x - hi.astype(f32)).astype(bf16)
    return _dot(hi, m01) + _dot(lo, m01)


def _softplus(z):
    return jnp.maximum(z, 0.0) + jnp.log(1.0 + jnp.exp(-jnp.abs(z)))


def _sba_specs(s, dh):
    hp, qb_n = SBA_HEADS_PER_STEP, Q_BLOCK
    blk = pl.BlockSpec((hp, qb_n, dh), lambda hh, i: (hh, i, 0))
    full = pl.BlockSpec((hp, s, dh), lambda hh, i: (hh, 0, 0))
    col1 = pl.BlockSpec((hp, qb_n, 1), lambda hh, i: (hh, i, 0))
    return blk, full, col1


def _sba_fwd(q, k, v):
    h, s, dh = q.shape
    hp, qb_n = SBA_HEADS_PER_STEP, Q_BLOCK

    def body(q_ref, k_ref, v_ref, o_ref, t_ref):
        i = pl.program_id(1)
        qbs = [q_ref[a].astype(bf16) for a in range(hp)]
        row = lax.broadcasted_iota(jnp.int32, (qb_n, qb_n), 0)
        col = lax.broadcasted_iota(jnp.int32, (qb_n, qb_n), 1)
        later = (row >= col).astype(bf16)

        def step(n, carry):
            tails, accs = carry
            off = pl.multiple_of((i - n) * qb_n, qb_n)
            mask = col < row + jnp.minimum(n, 1) * qb_n
            new_tails, new_accs = [], []
            for a in range(hp):
                kb = k_ref[a, pl.ds(off, qb_n), :].astype(bf16)
                vb = v_ref[a, pl.ds(off, qb_n), :].astype(bf16)
                z = _dot(qbs[a], kb, _NT) * SB_SCALE
                sp = _softplus(z)
                lk = jnp.where(mask, -sp, 0.0)
                cum = _split_dot(lk, later)
                w = jnp.where(mask, jnp.exp(z - sp + cum - lk + tails[a]), 0.0)
                new_tails.append(tails[a] + cum[:, 0:1])
                new_accs.append(accs[a] + _dot(w.astype(bf16), vb))
            return tuple(new_tails), tuple(new_accs)

        init = (tuple(jnp.zeros((qb_n, 1), f32) for _ in range(hp)), tuple(jnp.zeros((qb_n, dh), f32) for _ in range(hp)))
        tails, accs = lax.fori_loop(0, i + 1, step, init)
        for a in range(hp):
            o_ref[a] = accs[a]
            t_ref[a] = tails[a]

    blk, full, col1 = _sba_specs(s, dh)
    return pl.pallas_call(body, grid=(h // hp, s // qb_n), in_specs=[blk, full, full], out_specs=[blk, col1],
                          out_shape=[_sds((h, s, dh)), _sds((h, s, 1))], name="sba_fwd",
                          compiler_params=_params())(q, k, v)


def _sba_bwd(q, k, v, tot, do):
    h, s, dh = q.shape
    hp, qb_n = SBA_HEADS_PER_STEP, Q_BLOCK

    def body(q_ref, k_ref, v_ref, t_ref, do_ref, dq_ref, dk_ref, dv_ref):
        i = pl.program_id(1)

        @pl.when(i == 0)
        def _():
            dk_ref[...] = jnp.zeros_like(dk_ref)
            dv_ref[...] = jnp.zeros_like(dv_ref)

        qbs = [q_ref[a].astype(bf16) for a in range(hp)]
        dobs = [do_ref[a].astype(bf16) for a in range(hp)]
        tots = [t_ref[a] for a in range(hp)]
        row = lax.broadcasted_iota(jnp.int32, (qb_n, qb_n), 0)
        col = lax.broadcasted_iota(jnp.int32, (qb_n, qb_n), 1)
        upto = (row <= col).astype(bf16)
        before = (row < col).astype(bf16)

        def step(j, carry):
            heads, eheads, dqs = carry
            off = pl.multiple_of(j * qb_n, qb_n)
            mask = col < row + jnp.minimum(i - j, 1) * qb_n
            new_heads, new_eheads, new_dqs = [], [], []
            for a in range(hp):
                kb = k_ref[a, pl.ds(off, qb_n), :].astype(bf16)
                vb = v_ref[a, pl.ds(off, qb_n), :].astype(bf16)
                z = _dot(qbs[a], kb, _NT) * SB_SCALE
                sp = _softplus(z)
                lk = jnp.where(mask, -sp, 0.0)
                pre = _split_dot(lk, upto)
                w = jnp.where(mask, jnp.exp(z - sp + (tots[a] - heads[a] - pre)), 0.0)
                e = _dot(dobs[a], vb, _NT) * w
                epre = eheads[a] + _split_dot(e, before)
                dz = jnp.where(mask, e * jnp.exp(-sp) - epre * jnp.exp(z - sp), 0.0) * SB_SCALE
                dzb = dz.astype(bf16)
                dk_ref[a, pl.ds(off, qb_n), :] += _dot(dzb, qbs[a], _TN)
                dv_ref[a, pl.ds(off, qb_n), :] += _dot(w.astype(bf16), dobs[a], _TN)
                new_heads.append(heads[a] + pre[:, qb_n - 1:qb_n])
                new_eheads.append(eheads[a] + jnp.sum(e, axis=1, keepdims=True))
                new_dqs.append(dqs[a] + _dot(dzb, kb))
            return tuple(new_heads), tuple(new_eheads), tuple(new_dqs)

        zeros = tuple(jnp.zeros((qb_n, 1), f32) for _ in range(hp))
        _, _, dqs = lax.fori_loop(0, i + 1, step, (zeros, zeros, tuple(jnp.zeros((qb_n, dh), f32) for _ in range(hp))))
        for a in range(hp):
            dq_ref[a] = dqs[a]

    blk, full, col1 = _sba_specs(s, dh)
    return pl.pallas_call(body, grid=(h // hp, s // qb_n), in_specs=[blk, full, full, col1, blk],
                          out_specs=[blk, full, full], out_shape=[_sds((h, s, dh))] * 3, name="sba_bwd",
                          compiler_params=_params())(q, k, v, tot, do)


def _pool_fwd(hsm, pool_w, pool_scale):
    _, s, wd = hsm.shape
    ch = min(256, s)

    def body(u_ref, w_ref, sc_ref, b_ref, pooled_ref, pad_ref):
        pad_ref[0:POOL_HALO, :] = jnp.zeros((POOL_HALO, wd), f32)
        pad_ref[POOL_HALO:POOL_HALO + s, :] = u_ref[...]
        for g, win in enumerate(POOL_WINDOWS):
            cols = slice(g * GROUP_DIM, (g + 1) * GROUP_DIM)
            wg = w_ref[g].astype(bf16)
            for r0 in range(0, s, ch):
                acc = pad_ref[POOL_HALO + r0:POOL_HALO + r0 + ch, cols]
                own = acc
                for dlt in range(1, win):
                    acc = acc + pad_ref[POOL_HALO + r0 - dlt:POOL_HALO + r0 - dlt + ch, cols]
                t = r0 + lax.broadcasted_iota(jnp.int32, (ch, 1), 0)
                cnt = jnp.minimum(t + 1, win).astype(f32)
                pooled = acc / cnt - own
                pooled_ref[r0:r0 + ch, cols] = pooled
                b_ref[r0:r0 + ch, cols] = _dot(pooled.astype(bf16), wg) * sc_ref[:, cols]

    return pl.pallas_call(
        body, grid=(1,),
        in_specs=[pl.BlockSpec((None, s, wd), lambda i: (3, 0, 0)), pl.BlockSpec(pool_w.shape, lambda i: (0, 0, 0)),
                  _vec(wd)],
        out_specs=[pl.BlockSpec((s, wd), lambda i: (0, 0))] * 2, out_shape=[_sds((s, wd))] * 2,
        scratch_shapes=[pltpu.VMEM((POOL_HALO + s, wd), f32)], name="pool_fwd",
        compiler_params=_params())(hsm, pool_w, pool_scale)


def _pool_bwd(dcat, pooled, pool_w, pool_scale):
    s, wd = pooled.shape
    ch = min(256, s)

    def body(db_ref, p_ref, w_ref, sc_ref, du_ref, dw_ref, dsc_ref, pad_ref):
        pad_ref[s:s + POOL_HALO, :] = jnp.zeros((POOL_HALO, wd), f32)
        for g, win in enumerate(POOL_WINDOWS):
            cols = slice(g * GROUP_DIM, (g + 1) * GROUP_DIM)
            wg = w_ref[g].astype(bf16)
            pooled_g = p_ref[:, cols].astype(bf16)
            db = db_ref[:, cols]
            dmixed = (db * sc_ref[:, cols]).astype(bf16)
            dsc_ref[:, cols] = jnp.sum(db * _dot(pooled_g, wg), axis=0, keepdims=True)
            dw_ref[g] = _dot(pooled_g, dmixed, _TN)
            dpooled = _dot(dmixed, wg, _NT)
            t = lax.broadcasted_iota(jnp.int32, (s, 1), 0)
            pad_ref[0:s, cols] = dpooled / jnp.minimum(t + 1, win).astype(f32)
            for r0 in range(0, s, ch):
                acc = pad_ref[r0:r0 + ch, cols]
                for dlt in range(1, win):
                    acc = acc + pad_ref[r0 + dlt:r0 + dlt + ch, cols]
                du_ref[r0:r0 + ch, cols] = acc - dpooled[r0:r0 + ch]

    return pl.pallas_call(
        body, grid=(1,),
        in_specs=[pl.BlockSpec((s, wd), lambda i: (0, 1)), pl.BlockSpec((s, wd), lambda i: (0, 0)),
                  pl.BlockSpec(pool_w.shape, lambda i: (0, 0, 0)), _vec(wd)],
        out_specs=[pl.BlockSpec((s, wd), lambda i: (0, 0)), pl.BlockSpec(pool_w.shape, lambda i: (0, 0, 0)), _vec(wd)],
        out_shape=[_sds((s, wd)), _sds(pool_w.shape), _sds((1, wd))],
        scratch_shapes=[pltpu.VMEM((s + POOL_HALO, wd), f32)], name="pool_bwd",
        compiler_params=_params())(dcat, pooled, pool_w, pool_scale)


def _conv_fwd(hsm, dw, ln_g, ln_b):
    _, s, wd = hsm.shape
    rows, halo = CONV_ROWS, CONV_HALO

    def body(a_ref, g_ref, dw_ref, lg_ref, lb_ref, out_ref, hc_ref, xh_ref, rs_ref, pad_ref):
        hc = a_ref[...] * _sigmoid(g_ref[...])
        hc_ref[...] = hc
        pad_ref[0:halo, :] = jnp.zeros((halo, wd), f32)
        pad_ref[halo:halo + s, :] = hc
        taps = dw_ref[...]

        def chunk(c, _):
            base = pl.multiple_of(c * rows, rows)
            win = pad_ref[pl.ds(base, rows + halo), :]
            y = jnp.zeros((rows, wd), f32)
            for k in range(CONV_TAPS):
                lo = halo - (CONV_TAPS - 1) + k
                y = y + taps[k:k + 1, :] * win[lo:lo + rows]
            mu = jnp.mean(y, axis=-1, keepdims=True)
            cen = y - mu
            rstd = lax.rsqrt(jnp.mean(cen * cen, axis=-1, keepdims=True) + LN_EPS)
            xh = cen * rstd
            n = xh * lg_ref[...] + lb_ref[...]
            out_ref[pl.ds(base, rows), :] = n * _sigmoid(n)
            xh_ref[pl.ds(base, rows), :] = xh
            rs_ref[pl.ds(base, rows), :] = rstd
            return 0

        lax.fori_loop(0, s // rows, chunk, 0)

    full = pl.BlockSpec((s, wd), lambda i: (0, 0))
    return pl.pallas_call(
        body, grid=(1,),
        in_specs=[pl.BlockSpec((None, s, wd), lambda i: (0, 0, 0)), pl.BlockSpec((None, s, wd), lambda i: (1, 0, 0)),
                  pl.BlockSpec(dw.shape, lambda i: (0, 0)), _vec(wd), _vec(wd)],
        out_specs=[full, full, full, pl.BlockSpec((s, 1), lambda i: (0, 0))],
        out_shape=[_sds((s, wd))] * 3 + [_sds((s, 1))],
        scratch_shapes=[pltpu.VMEM((halo + s, wd), f32)], name="conv_fwd",
        compiler_params=_params())(hsm, hsm, dw, ln_g, ln_b)


def _conv_bwd(dcat, hsm, hc, xh, rstd, dw, ln_g, ln_b):
    s, wd = hc.shape
    rows, halo = CONV_ROWS, CONV_HALO

    def body(dc_ref, a_ref, g_ref, hc_ref, xh_ref, rs_ref, dw_ref, lg_ref, lb_ref,
             da_ref, dg_ref, ddw_ref, dlg_ref, dlb_ref, hpad_ref, ypad_ref):
        hpad_ref[0:halo, :] = jnp.zeros((halo, wd), f32)
        hpad_ref[halo:halo + s, :] = hc_ref[...]
        ypad_ref[s:s + halo, :] = jnp.zeros((halo, wd), f32)
        ddw_ref[...] = jnp.zeros_like(ddw_ref)
        dlg_ref[...] = jnp.zeros_like(dlg_ref)
        dlb_ref[...] = jnp.zeros_like(dlb_ref)
        taps = dw_ref[...]

        def norm_bwd(c, _):
            base = pl.multiple_of(c * rows, rows)
            xhv = xh_ref[pl.ds(base, rows), :]
            n = xhv * lg_ref[...] + lb_ref[...]
            sn = _sigmoid(n)
            dn = dc_ref[pl.ds(base, rows), :] * sn * (1.0 + n * (1.0 - sn))
            dlg_ref[...] += jnp.sum(dn * xhv, axis=0, keepdims=True)
            dlb_ref[...] += jnp.sum(dn, axis=0, keepdims=True)
            ypad_ref[pl.ds(base, rows), :] = _ln_bwd_rows(dn, xhv, rs_ref[pl.ds(base, rows), :], lg_ref[...])
            return 0

        lax.fori_loop(0, s // rows, norm_bwd, 0)

        def conv_bwd(c, _):
            base = pl.multiple_of(c * rows, rows)
            ywin = ypad_ref[pl.ds(base, rows + halo), :]
            hwin = hpad_ref[pl.ds(base, rows + halo), :]
            dy = ywin[0:rows]
            dhc = jnp.zeros((rows, wd), f32)
            for k in range(CONV_TAPS):
                fwd = CONV_TAPS - 1 - k
                dhc = dhc + taps[k:k + 1, :] * ywin[fwd:fwd + rows]
                lo = halo - (CONV_TAPS - 1) + k
                ddw_ref[k:k + 1, :] += jnp.sum(dy * hwin[lo:lo + rows], axis=0, keepdims=True)
            sg = _sigmoid(g_ref[pl.ds(base, rows), :])
            da_ref[pl.ds(base, rows), :] = dhc * sg
            dg_ref[pl.ds(base, rows), :] = dhc * a_ref[pl.ds(base, rows), :] * sg * (1.0 - sg)
            return 0

        lax.fori_loop(0, s // rows, conv_bwd, 0)

    full = pl.BlockSpec((s, wd), lambda i: (0, 0))
    tap_spec = pl.BlockSpec(dw.shape, lambda i: (0, 0))
    return pl.pallas_call(
        body, grid=(1,),
        in_specs=[full, pl.BlockSpec((None, s, wd), lambda i: (0, 0, 0)), pl.BlockSpec((None, s, wd), lambda i: (1, 0, 0)),
                  full, full, pl.BlockSpec((s, 1), lambda i: (0, 0)), tap_spec, _vec(wd), _vec(wd)],
        out_specs=[full, full, tap_spec, _vec(wd), _vec(wd)],
        out_shape=[_sds((s, wd)), _sds((s, wd)), _sds(dw.shape), _sds((1, wd)), _sds((1, wd))],
        scratch_shapes=[pltpu.VMEM((halo + s, wd), f32), pltpu.VMEM((s + halo, wd), f32)], name="conv_bwd",
        compiler_params=_params())(dcat, hsm, hsm, hc, xh, rstd, dw, ln_g, ln_b)


_GELU_C = 0.7978845608028654
_GELU_A = 0.044715


def _gelu(x):
    return 0.5 * x * (1.0 + jnp.tanh(_GELU_C * (x + _GELU_A * x * x * x)))


def _gelu_grad(x):
    th = jnp.tanh(_GELU_C * (x + _GELU_A * x * x * x))
    return 0.5 * (1.0 + th) + 0.5 * x * (1.0 - th * th) * _GELU_C * (1.0 + 3.0 * _GELU_A * x * x)


def _causal_sg_w(w_ref, g):
    row = lax.broadcasted_iota(jnp.int32, (SG_CHUNK, SG_CHUNK), 0)
    col = lax.broadcasted_iota(jnp.int32, (SG_CHUNK, SG_CHUNK), 1)
    return jnp.where(col <= row, w_ref[g], 0.0).astype(bf16), col <= row


def _gmlp_fwd(hsm, ln_g, ln_b, sg_w, sg_bt):
    _, s, wd = hsm.shape
    ck = SG_CHUNK

    def body(zu_ref, zv_ref, lg_ref, lb_ref, w_ref, bt_ref, out_ref, xh_ref, rs_ref):
        u = _gelu(zu_ref[...])
        vg = _gelu(zv_ref[...])
        mu = jnp.mean(vg, axis=-1, keepdims=True)
        cen = vg - mu
        rstd = lax.rsqrt(jnp.mean(cen * cen, axis=-1, keepdims=True) + LN_EPS)
        xh = cen * rstd
        xh_ref[...] = xh
        rs_ref[...] = rstd
        vn = (xh * lg_ref[...] + lb_ref[...]).astype(bf16)
        for g in range(4):
            cols = slice(g * GROUP_DIM, (g + 1) * GROUP_DIM)
            wm, _ = _causal_sg_w(w_ref, g)
            sv = _dot(wm, vn[:, cols]) + bt_ref[:, g:g + 1]
            out_ref[:, cols] = u[:, cols] * sv

    rows_spec = pl.BlockSpec((ck, wd), lambda i: (i, 0))
    return pl.pallas_call(
        body, grid=(s // ck,),
        in_specs=[pl.BlockSpec((None, ck, wd), lambda i: (2, i, 0)), pl.BlockSpec((None, ck, wd), lambda i: (3, i, 0)),
                  _vec(wd), _vec(wd), pl.BlockSpec(sg_w.shape, lambda i: (0, 0, 0)),
                  pl.BlockSpec(sg_bt.shape, lambda i: (0, 0))],
        out_specs=[rows_spec, rows_spec, pl.BlockSpec((ck, 1), lambda i: (i, 0))],
        out_shape=[_sds((s, wd)), _sds((s, wd)), _sds((s, 1))], name="gmlp_fwd",
        compiler_params=_params())(hsm, hsm, ln_g, ln_b, sg_w, sg_bt)


def _gmlp_bwd(dcat, hsm, xh, rstd, ln_g, ln_b, sg_w, sg_bt):
    s, wd = xh.shape
    ck = SG_CHUNK

    def body(dd_ref, zu_ref, zv_ref, xh_ref, rs_ref, lg_ref, lb_ref, w_ref, bt_ref,
             dzu_ref, dzv_ref, dw_ref, dbb_ref, dlg_ref, dlb_ref):
        @pl.when(pl.program_id(0) == 0)
        def _():
            dw_ref[...] = jnp.zeros_like(dw_ref)
            dbb_ref[...] = jnp.zeros_like(dbb_ref)
            dlg_ref[...] = jnp.zeros_like(dlg_ref)
            dlb_ref[...] = jnp.zeros_like(dlb_ref)

        zu, zv, dd, xhv = zu_ref[...], zv_ref[...], dd_ref[...], xh_ref[...]
        u = _gelu(zu)
        vn = (xhv * lg_ref[...] + lb_ref[...]).astype(bf16)
        du_parts, dvn_parts = [], []
        for g in range(4):
            cols = slice(g * GROUP_DIM, (g + 1) * GROUP_DIM)
            wm, keep = _causal_sg_w(w_ref, g)
            sv = _dot(wm, vn[:, cols]) + bt_ref[:, g:g + 1]
            du_parts.append(dd[:, cols] * sv)
            dsv = dd[:, cols] * u[:, cols]
            dsvb = dsv.astype(bf16)
            dbb_ref[g] += jnp.broadcast_to(jnp.sum(dsv, axis=1, keepdims=True), (ck, GROUP_DIM))
            dw_ref[g] += jnp.where(keep, _dot(dsvb, vn[:, cols], _NT), 0.0)
            dvn_parts.append(_dot(wm, dsvb, _TN))
        du = jnp.concatenate(du_parts, axis=1)
        dvn = jnp.concatenate(dvn_parts, axis=1)
        dlg_ref[...] += jnp.sum(dvn * xhv, axis=0, keepdims=True)
        dlb_ref[...] += jnp.sum(dvn, axis=0, keepdims=True)
        dzv_ref[...] = _ln_bwd_rows(dvn, xhv, rs_ref[...], lg_ref[...]) * _gelu_grad(zv)
        dzu_ref[...] = du * _gelu_grad(zu)

    rows_spec = pl.BlockSpec((ck, wd), lambda i: (i, 0))
    wspec = pl.BlockSpec(sg_w.shape, lambda i: (0, 0, 0))
    return pl.pallas_call(
        body, grid=(s // ck,),
        in_specs=[pl.BlockSpec((ck, wd), lambda i: (i, 1)), pl.BlockSpec((None, ck, wd), lambda i: (2, i, 0)),
                  pl.BlockSpec((None, ck, wd), lambda i: (3, i, 0)), rows_spec, pl.BlockSpec((ck, 1), lambda i: (i, 0)),
                  _vec(wd), _vec(wd), wspec, pl.BlockSpec(sg_bt.shape, lambda i: (0, 0))],
        out_specs=[rows_spec, rows_spec, wspec, wspec, _vec(wd), _vec(wd)],
        out_shape=[_sds((s, wd)), _sds((s, wd)), _sds(sg_w.shape), _sds(sg_w.shape), _sds((1, wd)), _sds((1, wd))],
        name="gmlp_bwd", compiler_params=_params())(dcat, hsm, hsm, xh, rstd, ln_g, ln_b, sg_w, sg_bt)


def _to_heads(x2d):
    s = x2d.shape[0]
    return jnp.transpose(x2d.reshape(s, N_HEADS, HEAD_DIM), (1, 0, 2))


def _from_heads(x3d):
    s = x3d.shape[1]
    return jnp.transpose(x3d, (1, 0, 2)).reshape(s, N_HEADS * HEAD_DIM)


def _local_step(x, p, target, wts, small):
    saved = []
    for i in range(DEPTH):
        tag = f"l{i}"
        if i % 2 == 0:
            hsm = _mm_nn_col(tag + "_in", x, wts["even_w_in"], 0)
            q, k, v = _to_heads(hsm[0]), _to_heads(hsm[1]), _to_heads(hsm[2])
            att_heads, sba_tot = _sba_fwd(q, k, v)
            att = _from_heads(att_heads)
            pool_out, pooled = _pool_fwd(hsm, small["pool_w"], small["pool_scale"])
            cat = jnp.concatenate([att, pool_out], axis=1)
            mix = _mm_nn_row(tag + "_out", cat, wts["even_w_out"], 0)
            mixer_saved = (hsm, q, k, v, sba_tot, pooled, cat)
        else:
            hsm = _mm_nn_col(tag + "_in", x, wts["odd_w_in"], 0)
            conv_out, hc, cxh, crs = _conv_fwd(hsm, small["conv_dw"], small["conv_ln_g"], small["conv_ln_b"])
            sg_out, sxh, srs = _gmlp_fwd(hsm, small["sg_ln_g"], small["sg_ln_b"], small["sg_w"], small["sg_bt"])
            cat = jnp.concatenate([conv_out, sg_out], axis=1)
            mix = _mm_nn_row(tag + "_out", cat, wts["odd_w_out"], 0)
            mixer_saved = (hsm, hc, cxh, crs, sxh, srs, cat)
        x1, xh1, rs1 = _ln_fwd(tag + "_ln_mix", x, mix, small["ln_mix_g"][i:i + 1], small["ln_mix_b"][i:i + 1])
        hg = _mm_nn_col(tag + "_gate", x1, wts["ffn_w_gate"], i)
        hu = _mm_nn_col(tag + "_up", x1, wts["ffn_w_up"], i)
        act = _swiglu_fwd(tag + "_swiglu", hg, hu)
        ffn = _mm_nn_row(tag + "_down", act, wts["ffn_w_down"], i)
        x2, xh2, rs2 = _ln_fwd(tag + "_ln_ffn", x1, ffn, small["ln_ffn_g"][i:i + 1], small["ln_ffn_b"][i:i + 1])
        gp = _mm_nn_row(tag + "_ple_gate", x2, wts["ple_w_gate"], i)
        pp = _mm_nn_col(tag + "_ple_proj", p[i], wts["ple_w_proj"], i, natural=True)
        x3 = _ple_fwd(tag + "_ple", x2, gp, small["ple_b_gate"][i:i + 1], pp)
        saved.append((x, mixer_saved, x1, xh1, rs1, hg, hu, act, x2, xh2, rs2, gp, pp))
        x = x3

    loss_part, dx = _loss_head("loss_head", x, target)

    big = {n: [None] * wts[n].shape[1] for n in wts}
    sm = {}
    per_layer = {n: [None] * DEPTH for n in ("ln_mix_g", "ln_mix_b", "ln_ffn_g", "ln_ffn_b", "ple_b_gate")}
    for i in reversed(range(DEPTH)):
        tag = f"l{i}b"
        x0, mixer_saved, x1, xh1, rs1, hg, hu, act, x2, xh2, rs2, gp, pp = saved[i]
        dgp, dpp, per_layer["ple_b_gate"][i] = _ple_bwd(tag + "_ple", dx, gp, small["ple_b_gate"][i:i + 1], pp)
        big["ple_w_proj"][i] = _mm_tn_col(tag + "_dproj", p[i], dpp)
        big["ple_w_gate"][i] = _mm_tn_row(tag + "_dgate", x2, dgp)
        dx2 = _mm_nt_row(tag + "_dx2", dgp, wts["ple_w_gate"], i, natural=True, add=dx)
        dr2, per_layer["ln_ffn_g"][i], per_layer["ln_ffn_b"][i] = _ln_bwd(tag + "_ln_ffn", dx2, xh2, rs2,
                                                                           small["ln_ffn_g"][i:i + 1])
        dact = _mm_nt_row(tag + "_dact", dr2, wts["ffn_w_down"], i)
        big["ffn_w_down"][i] = _mm_tn_row(tag + "_ddown", act, dr2)
        dhg, dhu = _swiglu_bwd(tag + "_swiglu", dact, hg, hu)
        big["ffn_w_gate"][i] = _mm_tn_col(tag + "_dgatew", x1, dhg)
        big["ffn_w_up"][i] = _mm_tn_col(tag + "_dupw", x1, dhu)
        part = _mm_nt_col(tag + "_dx1a", dhg, wts["ffn_w_gate"], i, dr2, ALPHA)
        dx1 = _mm_nt_col(tag + "_dx1b", dhu, wts["ffn_w_up"], i, part, 1.0)
        dr1, per_layer["ln_mix_g"][i], per_layer["ln_mix_b"][i] = _ln_bwd(tag + "_ln_mix", dx1, xh1, rs1,
                                                                           small["ln_mix_g"][i:i + 1])
        if i % 2 == 0:
            hsm, q, k, v, sba_tot, pooled, cat = mixer_saved
            big["even_w_out"][0] = _mm_tn_row(tag + "_dout", cat, dr1)
            dcat = _mm_nt_row(tag + "_dcat", dr1, wts["even_w_out"], 0, natural=True)
            dq, dk, dv = _sba_bwd(q, k, v, sba_tot, _to_heads(dcat[:, :N_HEADS * HEAD_DIM]))
            du, sm["pool_w"], sm["pool_scale"] = _pool_bwd(dcat, pooled, small["pool_w"], small["pool_scale"])
            dhsm = jnp.stack([_from_heads(dq), _from_heads(dk), _from_heads(dv), du])
            w_in = "even_w_in"
        else:
            hsm, hc, cxh, crs, sxh, srs, cat = mixer_saved
            big["odd_w_out"][0] = _mm_tn_row(tag + "_dout", cat, dr1)
            dcat = _mm_nt_row(tag + "_dcat", dr1, wts["odd_w_out"], 0, natural=True)
            da, dg, sm["conv_dw"], sm["conv_ln_g"], sm["conv_ln_b"] = _conv_bwd(
                dcat, hsm, hc, cxh, crs, small["conv_dw"], small["conv_ln_g"], small["conv_ln_b"])
            dzu, dzv, sm["sg_w"], dsgb, sm["sg_ln_g"], sm["sg_ln_b"] = _gmlp_bwd(
                dcat, hsm, sxh, srs, small["sg_ln_g"], small["sg_ln_b"], small["sg_w"], small["sg_bt"])
            sm["sg_b"] = dsgb[:, :, 0]
            dhsm = jnp.stack([da, dg, dzu, dzv])
            w_in = "odd_w_in"
        big[w_in][0] = _mm_tn_col(tag + "_din", x0, dhsm)
        dx = _mm_nt_col(tag + "_dx", dhsm, wts[w_in], 0, dr1, ALPHA)
    for n, parts in per_layer.items():
        sm[n] = jnp.concatenate(parts, axis=0)
    return loss_part, dx, big, sm


def _place():
    x, y, c = lax.axis_index("x"), lax.axis_index("y"), lax.axis_index("c")
    return x, y, c, [(1 - x, y), (x, 1 - y), (1 - x, 1 - y)]


def _gather_chips(shards):
    n = len(shards)

    def body(*refs):
        ins, outs = refs[:n], refs[n:2 * n]
        send, recv, loc = refs[2 * n:]
        x, y, c, chips = _place()
        mine = 2 * x + y
        started = []
        for t in range(n):
            cp = pltpu.make_async_copy(ins[t], outs[t].at[mine], loc.at[t])
            cp.start()
            started.append(cp)
            for j, (cx, cy) in enumerate(chips):
                rc = pltpu.make_async_remote_copy(src_ref=ins[t], dst_ref=outs[t].at[mine], send_sem=send.at[t, j],
                                                  recv_sem=recv.at[t, j], device_id=(cx, cy, c), device_id_type=_MESH)
                rc.start()
                started.append(rc)
        for cp in started:
            cp.wait()

    return pl.pallas_call(
        body, in_specs=[_ANY] * n, out_specs=[_ANY] * n,
        out_shape=[_sds((N_CHIPS,) + a.shape, a.dtype) for a in shards],
        scratch_shapes=[pltpu.SemaphoreType.DMA((n, 3)), pltpu.SemaphoreType.DMA((n, 3)), pltpu.SemaphoreType.DMA((n,))],
        name="gather_chips")(*shards)


def _scatter_chips(grads, layout):
    n_in = len(grads)
    shapes = {}
    for e, (pi, li) in enumerate(layout):
        r, cdim = grads[e].shape[1:]
        shapes[pi] = (N_CHIPS, max(li + 1, shapes.get(pi, (0, 0))[1]), r, cdim)
    n_out = len(shapes)

    def body(*refs):
        ins, outs = refs[:n_in], refs[n_in:n_in + n_out]
        send, recv, loc = refs[n_in + n_out:]
        x, y, c, chips = _place()
        mine = 2 * x + y
        started = []
        for e, (pi, li) in enumerate(layout):
            cp = pltpu.make_async_copy(ins[e].at[mine], outs[pi].at[mine, li], loc.at[e])
            cp.start()
            started.append(cp)
            for j, (cx, cy) in enumerate(chips):
                rc = pltpu.make_async_remote_copy(src_ref=ins[e].at[2 * cx + cy], dst_ref=outs[pi].at[mine, li],
                                                  send_sem=send.at[e, j], recv_sem=recv.at[e, j],
                                                  device_id=(cx, cy, c), device_id_type=_MESH)
                rc.start()
                started.append(rc)
        for cp in started:
            cp.wait()

    return pl.pallas_call(
        body, in_specs=[_ANY] * n_in, out_specs=[_ANY] * n_out,
        out_shape=[_sds(shapes[pi], grads[0].dtype) for pi in range(n_out)],
        scratch_shapes=[pltpu.SemaphoreType.DMA((n_in, 3)), pltpu.SemaphoreType.DMA((n_in, 3)),
                        pltpu.SemaphoreType.DMA((n_in,))],
        name="scatter_chips")(*grads)


def _swap_cores(arrays):
    n = len(arrays)

    def body(*refs):
        ins, outs = refs[:n], refs[n:2 * n]
        send, recv = refs[2 * n:]
        x, y, c, _ = _place()
        started = []
        for t in range(n):
            rc = pltpu.make_async_remote_copy(src_ref=ins[t], dst_ref=outs[t], send_sem=send.at[t], recv_sem=recv.at[t],
                                              device_id=(x, y, 1 - c), device_id_type=_MESH)
            rc.start()
            started.append(rc)
        for rc in started:
            rc.wait()

    return pl.pallas_call(
        body, in_specs=[_ANY] * n, out_specs=[_ANY] * n, out_shape=[_sds(a.shape, a.dtype) for a in arrays],
        scratch_shapes=[pltpu.SemaphoreType.DMA((n,)), pltpu.SemaphoreType.DMA((n,))], name="swap_cores")(*arrays)


def _gather_all(block):
    def body(in_ref, out_ref, send, recv, loc):
        x, y, c, _ = _place()
        mine = 4 * x + 2 * y + c
        cp = pltpu.make_async_copy(in_ref, out_ref.at[mine], loc)
        cp.start()
        started = [cp]
        for m in range(1, N_DEV):
            fx, fy, fc = (m >> 2) & 1, (m >> 1) & 1, m & 1
            peer = (x + fx - 2 * x * fx, y + fy - 2 * y * fy, c + fc - 2 * c * fc)
            rc = pltpu.make_async_remote_copy(src_ref=in_ref, dst_ref=out_ref.at[mine], send_sem=send.at[m - 1],
                                              recv_sem=recv.at[m - 1], device_id=peer, device_id_type=_MESH)
            rc.start()
            started.append(rc)
        for rc in started:
            rc.wait()

    return pl.pallas_call(
        body, in_specs=[_ANY], out_specs=_ANY, out_shape=_sds((N_DEV,) + block.shape),
        scratch_shapes=[pltpu.SemaphoreType.DMA((N_DEV - 1,)), pltpu.SemaphoreType.DMA((N_DEV - 1,)),
                        pltpu.SemaphoreType.DMA(())], name="gather_all")(block)


def _row_tile(r):
    for t in (256, 128, 64, 32, 16, 8):
        if r % t == 0:
            return t
    return r


def _sum_stack(name, stack):
    n, r, c = stack.shape
    tr = _row_tile(r)

    def body(s_ref, o_ref):
        acc = s_ref[0].astype(f32)
        for t in range(1, n):
            acc = acc + s_ref[t].astype(f32)
        o_ref[...] = acc

    return pl.pallas_call(body, grid=(r // tr,), in_specs=[pl.BlockSpec((n, tr, c), lambda i: (0, i, 0))],
                          out_specs=pl.BlockSpec((tr, c), lambda i: (i, 0)), out_shape=_sds((r, c)), name=name,
                          compiler_params=_params())(stack)


def _adamw(name, w, g_a, g_b, m, v):
    r, c = w.shape
    tr = _row_tile(r)
    two = g_b is not None
    bc1 = 1.0 - ADAM_B1 ** ADAM_STEP
    bc2 = 1.0 - ADAM_B2 ** ADAM_STEP

    def body(*refs):
        if two:
            w_ref, ga_ref, gb_ref, m_ref, v_ref, g_out, d_out, m_out, v_out = refs
            g = ga_ref[...] + gb_ref[...]
        else:
            w_ref, ga_ref, m_ref, v_ref, g_out, d_out, m_out, v_out = refs
            g = ga_ref[...]
        m_new = ADAM_B1 * m_ref[...] + (1.0 - ADAM_B1) * g
        v_new = ADAM_B2 * v_ref[...] + (1.0 - ADAM_B2) * (g * g)
        g_out[...] = g
        m_out[...] = m_new
        v_out[...] = v_new
        d_out[...] = -ADAM_LR * ((m_new / bc1) / (jnp.sqrt(v_new / bc2) + ADAM_EPS) + ADAM_WD * w_ref[...])

    spec = pl.BlockSpec((tr, c), lambda i: (i, 0))
    ins = [w, g_a] + ([g_b] if two else []) + [m, v]
    return pl.pallas_call(body, grid=(r // tr,), in_specs=[spec] * len(ins), out_specs=[spec] * 4,
                          out_shape=[_sds((r, c))] * 4, name=name, compiler_params=_params())(*ins)


_BIG = ("even_w_in", "even_w_out", "odd_w_in", "odd_w_out", "ffn_w_gate", "ffn_w_up", "ffn_w_down", "ple_w_proj",
        "ple_w_gate")
_SHARDED_SMALL = ("conv_dw", "conv_ln_g", "conv_ln_b", "sg_ln_g", "sg_ln_b")
_SMALL = ("pool_w", "pool_scale", "conv_dw", "conv_ln_g", "conv_ln_b", "sg_ln_g", "sg_ln_b", "sg_w", "sg_b",
          "ln_mix_g", "ln_mix_b", "ln_ffn_g", "ln_ffn_b", "ple_b_gate")
_WEIGHTS = ("even_w_in", "even_w_out", "pool_w", "pool_scale", "odd_w_in", "odd_w_out", "conv_dw", "conv_ln_g",
            "conv_ln_b", "sg_ln_g", "sg_ln_b", "sg_w", "sg_b", "ln_mix_g", "ln_mix_b", "ffn_w_gate", "ffn_w_up",
            "ffn_w_down", "ln_ffn_g", "ln_ffn_b", "ple_w_proj", "ple_w_gate", "ple_b_gate")


def _pack(arrays):
    flat = jnp.concatenate([a.reshape(-1) for a in arrays])
    pad = (-flat.shape[0]) % (256 * LANES)
    return jnp.pad(flat, (0, pad)).reshape(-1, LANES)


def _unpack(packed, shapes):
    flat = packed.reshape(-1)
    out, off = [], 0
    for shp in shapes:
        size = 1
        for dim in shp:
            size *= dim
        out.append(flat[off:off + size].reshape(shp))
        off += size
    return out


def _unshard_last(g4):
    return jnp.concatenate([g4[k] for k in range(N_CHIPS)], axis=-1)


def kernel(x, p, even_w_in, even_w_out, pool_w, pool_scale, odd_w_in, odd_w_out, conv_dw, conv_ln_g, conv_ln_b, sg_ln_g, sg_ln_b, sg_w, sg_b, ln_mix_g, ln_mix_b, ffn_w_gate, ffn_w_up, ffn_w_down, ln_ffn_g, ln_ffn_b, ple_w_proj, ple_w_gate, ple_b_gate, loss_target, m_even_w_in, m_even_w_out, m_pool_w, m_pool_scale, m_odd_w_in, m_odd_w_out, m_conv_dw, m_conv_ln_g, m_conv_ln_b, m_sg_ln_g, m_sg_ln_b, m_sg_w, m_sg_b, m_ln_mix_g, m_ln_mix_b, m_ffn_w_gate, m_ffn_w_up, m_ffn_w_down, m_ln_ffn_g, m_ln_ffn_b, m_ple_w_proj, m_ple_w_gate, m_ple_b_gate, v_even_w_in, v_even_w_out, v_pool_w, v_pool_scale, v_odd_w_in, v_odd_w_out, v_conv_dw, v_conv_ln_g, v_conv_ln_b, v_sg_ln_g, v_sg_ln_b, v_sg_w, v_sg_b, v_ln_mix_g, v_ln_mix_b, v_ffn_w_gate, v_ffn_w_up, v_ffn_w_down, v_ln_ffn_g, v_ln_ffn_b, v_ple_w_proj, v_ple_w_gate, v_ple_b_gate):
    w = dict(even_w_in=even_w_in, even_w_out=even_w_out, pool_w=pool_w, pool_scale=pool_scale, odd_w_in=odd_w_in,
             odd_w_out=odd_w_out, conv_dw=conv_dw, conv_ln_g=conv_ln_g, conv_ln_b=conv_ln_b, sg_ln_g=sg_ln_g,
             sg_ln_b=sg_ln_b, sg_w=sg_w, sg_b=sg_b, ln_mix_g=ln_mix_g, ln_mix_b=ln_mix_b, ffn_w_gate=ffn_w_gate,
             ffn_w_up=ffn_w_up, ffn_w_down=ffn_w_down, ln_ffn_g=ln_ffn_g, ln_ffn_b=ln_ffn_b, ple_w_proj=ple_w_proj,
             ple_w_gate=ple_w_gate, ple_b_gate=ple_b_gate)
    mom = dict(even_w_in=m_even_w_in, even_w_out=m_even_w_out, pool_w=m_pool_w, pool_scale=m_pool_scale,
               odd_w_in=m_odd_w_in, odd_w_out=m_odd_w_out, conv_dw=m_conv_dw, conv_ln_g=m_conv_ln_g,
               conv_ln_b=m_conv_ln_b, sg_ln_g=m_sg_ln_g, sg_ln_b=m_sg_ln_b, sg_w=m_sg_w, sg_b=m_sg_b,
               ln_mix_g=m_ln_mix_g, ln_mix_b=m_ln_mix_b, ffn_w_gate=m_ffn_w_gate, ffn_w_up=m_ffn_w_up,
               ffn_w_down=m_ffn_w_down, ln_ffn_g=m_ln_ffn_g, ln_ffn_b=m_ln_ffn_b, ple_w_proj=m_ple_w_proj,
               ple_w_gate=m_ple_w_gate, ple_b_gate=m_ple_b_gate)
    var = dict(even_w_in=v_even_w_in, even_w_out=v_even_w_out, pool_w=v_pool_w, pool_scale=v_pool_scale,
               odd_w_in=v_odd_w_in, odd_w_out=v_odd_w_out, conv_dw=v_conv_dw, conv_ln_g=v_conv_ln_g,
               conv_ln_b=v_conv_ln_b, sg_ln_g=v_sg_ln_g, sg_ln_b=v_sg_ln_b, sg_w=v_sg_w, sg_b=v_sg_b,
               ln_mix_g=v_ln_mix_g, ln_mix_b=v_ln_mix_b, ffn_w_gate=v_ffn_w_gate, ffn_w_up=v_ffn_w_up,
               ffn_w_down=v_ffn_w_down, ln_ffn_g=v_ln_ffn_g, ln_ffn_b=v_ln_ffn_b, ple_w_proj=v_ple_w_proj,
               ple_w_gate=v_ple_w_gate, ple_b_gate=v_ple_b_gate)

    gathered = _gather_chips([w[n].astype(bf16) for n in _BIG] + [w[n] for n in _SHARDED_SMALL])
    wts = dict(zip(_BIG, gathered[:len(_BIG)]))
    small = {n: w[n][0] for n in ("pool_w", "sg_w")}
    small.update({n: w[n] for n in ("pool_scale", "ln_mix_g", "ln_mix_b", "ln_ffn_g", "ln_ffn_b", "ple_b_gate")})
    small["sg_bt"] = jnp.transpose(w["sg_b"][0])
    for n, g4 in zip(_SHARDED_SMALL, gathered[len(_BIG):]):
        small[n] = _unshard_last(g4)[0]
        if n != "conv_dw":
            small[n] = small[n][None]

    loss_part, grad_x, big, sm = _local_step(x[0], p[:, 0], loss_target[0], wts, small)

    entries, layout = [], []
    for pi, n in enumerate(_BIG):
        for li, g in enumerate(big[n]):
            entries.append(g)
            layout.append((pi, li))
    received = _scatter_chips(entries, layout)
    chip_sums = []
    for n, r4 in zip(_BIG, received):
        _, nl, r, c = r4.shape
        chip_sums.append(_sum_stack("sum_" + n, r4.reshape(N_CHIPS, nl * r, c)))
    other = _swap_cores(chip_sums)
    results = {}
    for n, mine, theirs in zip(_BIG, chip_sums, other):
        shp = w[n].shape
        flat = (shp[0] * shp[1], shp[2])
        outs = _adamw("adamw_" + n, w[n].reshape(flat), mine, theirs, mom[n].reshape(flat), var[n].reshape(flat))
        results[n] = [o.reshape(shp) for o in outs]

    sm_shapes = [(1,) + sm[n].shape if n in ("pool_w", "sg_w", "conv_dw", "sg_b") else sm[n].shape for n in _SMALL]
    packed = _pack([sm[n] for n in _SMALL] + [loss_part[0, 0:1]])
    total = _sum_stack("sum_small", _gather_all(packed))
    parts = _unpack(total, sm_shapes + [(1,)])
    loss = parts[-1][0]
    chip = 2 * lax.axis_index("x") + lax.axis_index("y")
    g_small = {}
    for n, g in zip(_SMALL, parts[:-1]):
        if n in _SHARDED_SMALL:
            width = w[n].shape[-1]
            g = lax.dynamic_slice_in_dim(g, chip * width, width, axis=g.ndim - 1)
        g_small[n] = g
    shapes = [w[n].shape for n in _SMALL]
    outs = _adamw("adamw_small", _pack([w[n] for n in _SMALL]), _pack([g_small[n] for n in _SMALL]), None,
                  _pack([mom[n] for n in _SMALL]), _pack([var[n] for n in _SMALL]))
    unpacked = [_unpack(o, shapes) for o in outs]
    for idx, n in enumerate(_SMALL):
        results[n] = [u[idx] for u in unpacked]

    return (loss, grad_x[None], *[results[n][0] for n in _WEIGHTS], *[results[n][1] for n in _WEIGHTS],
            *[results[n][2] for n in _WEIGHTS], *[results[n][3] for n in _WEIGHTS])
```

```python
import jax
import jax.numpy as jnp
from jax import lax
from jax.experimental import pallas as pl
from jax.experimental.pallas import tpu as pltpu

f32 = jnp.float32
bf16 = jnp.bfloat16

D_MODEL = 1024
N_HEADS = 8
HEAD_DIM = 64
Q_BLOCK = 128
POOL_WINDOWS = (2, 4, 8, 16)
GROUP_DIM = 128
CONV_TAPS = 31
SG_CHUNK = 128
DEPTH = 2
ALPHA = (2 * DEPTH) ** 0.25
LN_EPS = 1e-5
SB_SCALE = HEAD_DIM ** -0.5
ADAM_LR, ADAM_B1, ADAM_B2, ADAM_EPS, ADAM_WD, ADAM_STEP = 0.001, 0.9, 0.999, 1e-08, 0.01, 10
N_CHIPS = 4
N_DEV = 8
LANES = 128
VMEM_LIMIT = 56 * 1024 * 1024
TM = 512
TR = 256
SBA_HEADS_PER_STEP = 4
CONV_ROWS = 64
CONV_HALO = 32
POOL_HALO = 16

_NN = (((1,), (0,)), ((), ()))
_NT = (((1,), (1,)), ((), ()))
_TN = (((0,), (0,)), ((), ()))
_ANY = pl.BlockSpec(memory_space=pl.ANY)
_MESH = pl.DeviceIdType.MESH


def _params():
    return pltpu.CompilerParams(vmem_limit_bytes=VMEM_LIMIT)


def _sds(shape, dtype=f32):
    return jax.ShapeDtypeStruct(tuple(shape), dtype)


def _dot(a, b, dims=_NN):
    return lax.dot_general(a, b, dims, preferred_element_type=f32)


def _sigmoid(x):
    return 1.0 / (1.0 + jnp.exp(-x))


def _mm(name, a, b, grid, a_spec, b_spec, out_shape, out_spec, dims, reduce=False, add=None, add_spec=None,
        add_scale=1.0, out_dtype=f32):
    has_add = add is not None
    k_axis = len(grid) - 1

    def body(*refs):
        if has_add:
            a_ref, b_ref, add_ref, o_ref = refs
        else:
            a_ref, b_ref, o_ref = refs
        r = _dot(a_ref[...].astype(bf16), b_ref[...].astype(bf16), dims)
        if reduce:
            k = pl.program_id(k_axis)

            @pl.when(k == 0)
            def _():
                o_ref[...] = r + add_scale * add_ref[...] if has_add else r

            @pl.when(k > 0)
            def _():
                o_ref[...] += r
        else:
            o_ref[...] = (r + add_scale * add_ref[...] if has_add else r).astype(out_dtype)

    ins = [a, b] + ([add] if has_add else [])
    specs = [a_spec, b_spec] + ([add_spec] if has_add else [])
    return pl.pallas_call(body, grid=grid, in_specs=specs, out_specs=out_spec, out_shape=_sds(out_shape, out_dtype),
                          name=name, compiler_params=_params())(*ins)


def _tm(s):
    return min(TM, s)


def _act_spec(a, tm, width):
    if a.ndim == 3:
        return pl.BlockSpec((None, tm, width), lambda i, k: (k, i, 0))
    return pl.BlockSpec((tm, width), lambda i, k: (i, k))


def _mm_nn_col(name, x, w4, layer, natural=False):
    s, kk = x.shape
    nq = w4.shape[3]
    tm = _tm(s)
    if natural:
        out_shape, out_spec = (s, 4 * nq), pl.BlockSpec((tm, nq), lambda i, k: (i, k))
    else:
        out_shape, out_spec = (4, s, nq), pl.BlockSpec((None, tm, nq), lambda i, k: (k, i, 0))
    return _mm(name, x, w4, (s // tm, 4), pl.BlockSpec((tm, kk), lambda i, k: (i, 0)),
               pl.BlockSpec((None, None, kk, nq), lambda i, k: (k, layer, 0, 0)), out_shape, out_spec, _NN)


def _mm_nn_row(name, a, w4, layer):
    s = a.shape[-2]
    kq, n = w4.shape[2], w4.shape[3]
    tm = _tm(s)
    return _mm(name, a, w4, (s // tm, 4), _act_spec(a, tm, kq),
               pl.BlockSpec((None, None, kq, n), lambda i, k: (k, layer, 0, 0)), (s, n),
               pl.BlockSpec((tm, n), lambda i, k: (i, 0)), _NN, reduce=True)


def _mm_nt_col(name, dh, w4, layer, add, add_scale):
    s = dh.shape[-2]
    kk, nq = w4.shape[2], w4.shape[3]
    tm = _tm(s)
    row = pl.BlockSpec((tm, kk), lambda i, k: (i, 0))
    return _mm(name, dh, w4, (s // tm, 4), _act_spec(dh, tm, nq),
               pl.BlockSpec((None, None, kk, nq), lambda i, k: (k, layer, 0, 0)), (s, kk), row, _NT,
               reduce=True, add=add, add_spec=row, add_scale=add_scale)


def _mm_nt_row(name, dy, w4, layer, natural=False, add=None):
    s, n = dy.shape
    kq = w4.shape[2]
    tm = _tm(s)
    if natural:
        out_shape, out_spec = (s, 4 * kq), pl.BlockSpec((tm, kq), lambda i, k: (i, k))
    else:
        out_shape, out_spec = (4, s, kq), pl.BlockSpec((None, tm, kq), lambda i, k: (k, i, 0))
    return _mm(name, dy, w4, (s // tm, 4), pl.BlockSpec((tm, n), lambda i, k: (i, 0)),
               pl.BlockSpec((None, None, kq, n), lambda i, k: (k, layer, 0, 0)), out_shape, out_spec, _NT,
               add=add, add_spec=out_spec if add is not None else None)


def _mm_tn_col(name, x, dh):
    s, kk = x.shape
    if dh.ndim == 3:
        nq = dh.shape[2]
        b_spec = pl.BlockSpec((None, s, nq), lambda k, j: (k, 0, 0))
    else:
        nq = dh.shape[1] // 4
        b_spec = pl.BlockSpec((s, nq), lambda k, j: (0, k))
    tk = min(256, kk)
    return _mm(name, x, dh, (4, kk // tk), pl.BlockSpec((s, tk), lambda k, j: (0, j)), b_spec, (4, kk, nq),
               pl.BlockSpec((None, tk, nq), lambda k, j: (k, j, 0)), _TN, out_dtype=bf16)


def _mm_tn_row(name, a, dy):
    s, n = dy.shape
    if a.ndim == 3:
        kq = a.shape[2]
        a_spec = pl.BlockSpec((None, s, kq), lambda k, j: (k, 0, 0))
    else:
        kq = a.shape[1] // 4
        a_spec = pl.BlockSpec((s, kq), lambda k, j: (0, k))
    tn = min(512, n)
    return _mm(name, a, dy, (4, n // tn), a_spec, pl.BlockSpec((s, tn), lambda k, j: (0, j)), (4, kq, n),
               pl.BlockSpec((None, kq, tn), lambda k, j: (k, 0, j)), _TN, out_dtype=bf16)


def _tr(s):
    return min(TR, s)


def _rows(tm, d):
    return pl.BlockSpec((tm, d), lambda i: (i, 0))


def _vec(d):
    return pl.BlockSpec((1, d), lambda i: (0, 0))


def _ln_fwd(name, x, mix, g, b):
    s, d = x.shape
    tm = _tr(s)

    def body(x_ref, m_ref, g_ref, b_ref, y_ref, xh_ref, rs_ref):
        r = ALPHA * x_ref[...] + m_ref[...]
        mu = jnp.mean(r, axis=-1, keepdims=True)
        c = r - mu
        rstd = lax.rsqrt(jnp.mean(c * c, axis=-1, keepdims=True) + LN_EPS)
        xh = c * rstd
        y_ref[...] = xh * g_ref[...] + b_ref[...]
        xh_ref[...] = xh
        rs_ref[...] = rstd

    return pl.pallas_call(
        body, grid=(s // tm,), in_specs=[_rows(tm, d), _rows(tm, d), _vec(d), _vec(d)],
        out_specs=[_rows(tm, d), _rows(tm, d), _rows(tm, 1)],
        out_shape=[_sds((s, d)), _sds((s, d)), _sds((s, 1))], name=name, compiler_params=_params())(x, mix, g, b)


def _ln_bwd_rows(dy, xh, rstd, g):
    dxh = dy * g
    m1 = jnp.mean(dxh, axis=-1, keepdims=True)
    m2 = jnp.mean(dxh * xh, axis=-1, keepdims=True)
    return rstd * (dxh - m1 - xh * m2)


def _ln_bwd(name, dy, xh, rstd, g):
    s, d = dy.shape
    tm = _tr(s)

    def body(dy_ref, xh_ref, rs_ref, g_ref, dr_ref, dg_ref, db_ref):
        @pl.when(pl.program_id(0) == 0)
        def _():
            dg_ref[...] = jnp.zeros_like(dg_ref)
            db_ref[...] = jnp.zeros_like(db_ref)

        dyv, xhv = dy_ref[...], xh_ref[...]
        dr_ref[...] = _ln_bwd_rows(dyv, xhv, rs_ref[...], g_ref[...])
        dg_ref[...] += jnp.sum(dyv * xhv, axis=0, keepdims=True)
        db_ref[...] += jnp.sum(dyv, axis=0, keepdims=True)

    return pl.pallas_call(
        body, grid=(s // tm,), in_specs=[_rows(tm, d), _rows(tm, d), _rows(tm, 1), _vec(d)],
        out_specs=[_rows(tm, d), _vec(d), _vec(d)],
        out_shape=[_sds((s, d)), _sds((1, d)), _sds((1, d))], name=name, compiler_params=_params())(dy, xh, rstd, g)


def _sm_spec(tm, w):
    return pl.BlockSpec((None, tm, w), lambda k, i: (k, i, 0))


def _swiglu_fwd(name, hg, hu):
    _, s, w = hg.shape
    tm = _tr(s)

    def body(g_ref, u_ref, a_ref):
        g = g_ref[...]
        a_ref[...] = g * _sigmoid(g) * u_ref[...]

    return pl.pallas_call(body, grid=(4, s // tm), in_specs=[_sm_spec(tm, w)] * 2, out_specs=_sm_spec(tm, w),
                          out_shape=_sds(hg.shape), name=name, compiler_params=_params())(hg, hu)


def _swiglu_bwd(name, da, hg, hu):
    _, s, w = hg.shape
    tm = _tr(s)

    def body(da_ref, g_ref, u_ref, dg_ref, du_ref):
        g, da_v = g_ref[...], da_ref[...]
        sg = _sigmoid(g)
        du_ref[...] = da_v * g * sg
        dg_ref[...] = da_v * u_ref[...] * sg * (1.0 + g * (1.0 - sg))

    return pl.pallas_call(body, grid=(4, s // tm), in_specs=[_sm_spec(tm, w)] * 3, out_specs=[_sm_spec(tm, w)] * 2,
                          out_shape=[_sds(hg.shape)] * 2, name=name, compiler_params=_params())(da, hg, hu)


def _ple_fwd(name, x2, gp, bias, pp):
    s, d = x2.shape
    tm = _tr(s)

    def body(x_ref, gp_ref, b_ref, pp_ref, y_ref):
        y_ref[...] = x_ref[...] + _sigmoid(gp_ref[...] + b_ref[...]) * pp_ref[...]

    return pl.pallas_call(body, grid=(s // tm,), in_specs=[_rows(tm, d), _rows(tm, d), _vec(d), _rows(tm, d)],
                          out_specs=_rows(tm, d), out_shape=_sds((s, d)), name=name,
                          compiler_params=_params())(x2, gp, bias, pp)


def _ple_bwd(name, dy, gp, bias, pp):
    s, d = dy.shape
    tm = _tr(s)

    def body(dy_ref, gp_ref, b_ref, pp_ref, dgp_ref, dpp_ref, db_ref):
        @pl.when(pl.program_id(0) == 0)
        def _():
            db_ref[...] = jnp.zeros_like(db_ref)

        dyv = dy_ref[...]
        gate = _sigmoid(gp_ref[...] + b_ref[...])
        dgp = dyv * pp_ref[...] * gate * (1.0 - gate)
        dgp_ref[...] = dgp
        dpp_ref[...] = dyv * gate
        db_ref[...] += jnp.sum(dgp, axis=0, keepdims=True)

    return pl.pallas_call(body, grid=(s // tm,), in_specs=[_rows(tm, d), _rows(tm, d), _vec(d), _rows(tm, d)],
                          out_specs=[_rows(tm, d), _rows(tm, d), _vec(d)],
                          out_shape=[_sds((s, d)), _sds((s, d)), _sds((1, d))], name=name,
                          compiler_params=_params())(dy, gp, bias, pp)


def _loss_head(name, y, target):
    s, d = y.shape
    tm = _tr(s)

    def body(y_ref, t_ref, l_ref, dy_ref):
        @pl.when(pl.program_id(0) == 0)
        def _():
            l_ref[...] = jnp.zeros_like(l_ref)

        e = y_ref[...] - t_ref[...]
        dy_ref[...] = e * (1.0 / d)
        tot = jnp.sum(jnp.sum(e * e, axis=1, keepdims=True), axis=0, keepdims=True) * (0.5 / d)
        l_ref[...] += jnp.broadcast_to(tot, l_ref.shape)

    return pl.pallas_call(body, grid=(s // tm,), in_specs=[_rows(tm, d), _rows(tm, d)],
                          out_specs=[pl.BlockSpec((1, LANES), lambda i: (0, 0)), _rows(tm, d)],
                          out_shape=[_sds((1, LANES)), _sds((s, d))], name=name, compiler_params=_params())(y, target)


def _split_dot(x, m01):
    hi = x.astype(bf16)
    lo = (x - hi.astype(f32)).astype(bf16)
    return _dot(hi, m01) + _dot(lo, m01)


def _softplus(z):
    return jnp.maximum(z, 0.0) + jnp.log(1.0 + jnp.exp(-jnp.abs(z)))


def _sba_specs(s, dh):
    hp, qb_n = SBA_HEADS_PER_STEP, Q_BLOCK
    blk = pl.BlockSpec((hp, qb_n, dh), lambda hh, i: (hh, i, 0))
    full = pl.BlockSpec((hp, s, dh), lambda hh, i: (hh, 0, 0))
    col1 = pl.BlockSpec((hp, qb_n, 1), lambda hh, i: (hh, i, 0))
    return blk, full, col1


def _sba_fwd(q, k, v, shards):
    h, s, dh = q.shape
    hp, qb_n = SBA_HEADS_PER_STEP, Q_BLOCK
    n = len(shards)
    steps = (h // hp, s // qb_n)

    def body(q_ref, k_ref, v_ref, *rest):
        o_ref, t_ref = rest[n:n + 2]
        copies = _gather_copies(rest[:n], rest[n + 2:2 * n + 2], *rest[2 * n + 2:])
        i = pl.program_id(1)

        @pl.when(jnp.logical_and(pl.program_id(0) == 0, i == 0))
        def _():
            for cp in copies:
                cp.start()

        qbs = [q_ref[a].astype(bf16) for a in range(hp)]
        row = lax.broadcasted_iota(jnp.int32, (qb_n, qb_n), 0)
        col = lax.broadcasted_iota(jnp.int32, (qb_n, qb_n), 1)
        later = (row >= col).astype(bf16)

        def step(n, carry):
            tails, accs = carry
            off = pl.multiple_of((i - n) * qb_n, qb_n)
            mask = col < row + jnp.minimum(n, 1) * qb_n
            new_tails, new_accs = [], []
            for a in range(hp):
                kb = k_ref[a, pl.ds(off, qb_n), :].astype(bf16)
                vb = v_ref[a, pl.ds(off, qb_n), :].astype(bf16)
                z = _dot(qbs[a], kb, _NT) * SB_SCALE
                sp = _softplus(z)
                lk = jnp.where(mask, -sp, 0.0)
                cum = _split_dot(lk, later)
                w = jnp.where(mask, jnp.exp(z - sp + cum - lk + tails[a]), 0.0)
                new_tails.append(tails[a] + cum[:, 0:1])
                new_accs.append(accs[a] + _dot(w.astype(bf16), vb))
            return tuple(new_tails), tuple(new_accs)

        init = (tuple(jnp.zeros((qb_n, 1), f32) for _ in range(hp)), tuple(jnp.zeros((qb_n, dh), f32) for _ in range(hp)))
        tails, accs = lax.fori_loop(0, i + 1, step, init)
        for a in range(hp):
            o_ref[a] = accs[a]
            t_ref[a] = tails[a]

        @pl.when(jnp.logical_and(pl.program_id(0) == steps[0] - 1, i == steps[1] - 1))
        def _():
            for cp in copies:
                cp.wait()

    blk, full, col1 = _sba_specs(s, dh)
    outs = pl.pallas_call(body, grid=steps, in_specs=[blk, full, full] + [_ANY] * n,
                          out_specs=[blk, col1] + [_ANY] * n,
                          out_shape=[_sds((h, s, dh)), _sds((h, s, 1))] + _gather_shapes(shards),
                          scratch_shapes=_comm_sems(n), name="sba_fwd", compiler_params=_params())(q, k, v, *shards)
    return outs[0], outs[1], outs[2:]


def _sba_bwd(q, k, v, tot, do, grads, layout):
    h, s, dh = q.shape
    hp, qb_n = SBA_HEADS_PER_STEP, Q_BLOCK
    n_in = len(grads)
    scatter_shape = _scatter_shapes(grads, layout)
    n_out = len(scatter_shape)
    steps = (h // hp, s // qb_n)

    def body(q_ref, k_ref, v_ref, t_ref, do_ref, *rest):
        dq_ref, dk_ref, dv_ref = rest[n_in:n_in + 3]
        copies = _scatter_copies(rest[:n_in], rest[n_in + 3:n_in + 3 + n_out], layout, *rest[n_in + 3 + n_out:])
        i = pl.program_id(1)

        @pl.when(jnp.logical_and(pl.program_id(0) == 0, i == 0))
        def _():
            for cp in copies:
                cp.start()

        @pl.when(i == 0)
        def _():
            dk_ref[...] = jnp.zeros_like(dk_ref)
            dv_ref[...] = jnp.zeros_like(dv_ref)

        qbs = [q_ref[a].astype(bf16) for a in range(hp)]
        dobs = [do_ref[a].astype(bf16) for a in range(hp)]
        tots = [t_ref[a] for a in range(hp)]
        row = lax.broadcasted_iota(jnp.int32, (qb_n, qb_n), 0)
        col = lax.broadcasted_iota(jnp.int32, (qb_n, qb_n), 1)
        upto = (row <= col).astype(bf16)
        before = (row < col).astype(bf16)

        def step(j, carry):
            heads, eheads, dqs = carry
            off = pl.multiple_of(j * qb_n, qb_n)
            mask = col < row + jnp.minimum(i - j, 1) * qb_n
            new_heads, new_eheads, new_dqs = [], [], []
            for a in range(hp):
                kb = k_ref[a, pl.ds(off, qb_n), :].astype(bf16)
                vb = v_ref[a, pl.ds(off, qb_n), :].astype(bf16)
                z = _dot(qbs[a], kb, _NT) * SB_SCALE
                sp = _softplus(z)
                lk = jnp.where(mask, -sp, 0.0)
                pre = _split_dot(lk, upto)
                w = jnp.where(mask, jnp.exp(z - sp + (tots[a] - heads[a] - pre)), 0.0)
                e = _dot(dobs[a], vb, _NT) * w
                epre = eheads[a] + _split_dot(e, before)
                dz = jnp.where(mask, e * jnp.exp(-sp) - epre * jnp.exp(z - sp), 0.0) * SB_SCALE
                dzb = dz.astype(bf16)
                dk_ref[a, pl.ds(off, qb_n), :] += _dot(dzb, qbs[a], _TN)
                dv_ref[a, pl.ds(off, qb_n), :] += _dot(w.astype(bf16), dobs[a], _TN)
                new_heads.append(heads[a] + pre[:, qb_n - 1:qb_n])
                new_eheads.append(eheads[a] + jnp.sum(e, axis=1, keepdims=True))
                new_dqs.append(dqs[a] + _dot(dzb, kb))
            return tuple(new_heads), tuple(new_eheads), tuple(new_dqs)

        zeros = tuple(jnp.zeros((qb_n, 1), f32) for _ in range(hp))
        _, _, dqs = lax.fori_loop(0, i + 1, step, (zeros, zeros, tuple(jnp.zeros((qb_n, dh), f32) for _ in range(hp))))
        for a in range(hp):
            dq_ref[a] = dqs[a]

        @pl.when(jnp.logical_and(pl.program_id(0) == steps[0] - 1, i == steps[1] - 1))
        def _():
            for cp in copies:
                cp.wait()

    blk, full, col1 = _sba_specs(s, dh)
    outs = pl.pallas_call(body, grid=steps, in_specs=[blk, full, full, col1, blk] + [_ANY] * n_in,
                          out_specs=[blk, full, full] + [_ANY] * n_out,
                          out_shape=[_sds((h, s, dh))] * 3 + scatter_shape, scratch_shapes=_comm_sems(n_in),
                          name="sba_bwd", compiler_params=_params())(q, k, v, tot, do, *grads)
    return outs[0], outs[1], outs[2], outs[3:]


def _pool_fwd(hsm, pool_w, pool_scale):
    _, s, wd = hsm.shape
    ch = min(256, s)

    def body(u_ref, w_ref, sc_ref, b_ref, pooled_ref, pad_ref):
        pad_ref[0:POOL_HALO, :] = jnp.zeros((POOL_HALO, wd), f32)
        pad_ref[POOL_HALO:POOL_HALO + s, :] = u_ref[...]
        for g, win in enumerate(POOL_WINDOWS):
            cols = slice(g * GROUP_DIM, (g + 1) * GROUP_DIM)
            wg = w_ref[g].astype(bf16)
            for r0 in range(0, s, ch):
                acc = pad_ref[POOL_HALO + r0:POOL_HALO + r0 + ch, cols]
                own = acc
                for dlt in range(1, win):
                    acc = acc + pad_ref[POOL_HALO + r0 - dlt:POOL_HALO + r0 - dlt + ch, cols]
                t = r0 + lax.broadcasted_iota(jnp.int32, (ch, 1), 0)
                cnt = jnp.minimum(t + 1, win).astype(f32)
                pooled = acc / cnt - own
                pooled_ref[r0:r0 + ch, cols] = pooled
                b_ref[r0:r0 + ch, cols] = _dot(pooled.astype(bf16), wg) * sc_ref[:, cols]

    return pl.pallas_call(
        body, grid=(1,),
        in_specs=[pl.BlockSpec((None, s, wd), lambda i: (3, 0, 0)), pl.BlockSpec(pool_w.shape, lambda i: (0, 0, 0)),
                  _vec(wd)],
        out_specs=[pl.BlockSpec((s, wd), lambda i: (0, 0))] * 2, out_shape=[_sds((s, wd))] * 2,
        scratch_shapes=[pltpu.VMEM((POOL_HALO + s, wd), f32)], name="pool_fwd",
        compiler_params=_params())(hsm, pool_w, pool_scale)


def _pool_bwd(dcat, pooled, pool_w, pool_scale):
    s, wd = pooled.shape
    ch = min(256, s)

    def body(db_ref, p_ref, w_ref, sc_ref, du_ref, dw_ref, dsc_ref, pad_ref):
        pad_ref[s:s + POOL_HALO, :] = jnp.zeros((POOL_HALO, wd), f32)
        for g, win in enumerate(POOL_WINDOWS):
            cols = slice(g * GROUP_DIM, (g + 1) * GROUP_DIM)
            wg = w_ref[g].astype(bf16)
            pooled_g = p_ref[:, cols].astype(bf16)
            db = db_ref[:, cols]
            dmixed = (db * sc_ref[:, cols]).astype(bf16)
            dsc_ref[:, cols] = jnp.sum(db * _dot(pooled_g, wg), axis=0, keepdims=True)
            dw_ref[g] = _dot(pooled_g, dmixed, _TN)
            dpooled = _dot(dmixed, wg, _NT)
            t = lax.broadcasted_iota(jnp.int32, (s, 1), 0)
            pad_ref[0:s, cols] = dpooled / jnp.minimum(t + 1, win).astype(f32)
            for r0 in range(0, s, ch):
                acc = pad_ref[r0:r0 + ch, cols]
                for dlt in range(1, win):
                    acc = acc + pad_ref[r0 + dlt:r0 + dlt + ch, cols]
                du_ref[r0:r0 + ch, cols] = acc - dpooled[r0:r0 + ch]

    return pl.pallas_call(
        body, grid=(1,),
        in_specs=[pl.BlockSpec((s, wd), lambda i: (0, 1)), pl.BlockSpec((s, wd), lambda i: (0, 0)),
                  pl.BlockSpec(pool_w.shape, lambda i: (0, 0, 0)), _vec(wd)],
        out_specs=[pl.BlockSpec((s, wd), lambda i: (0, 0)), pl.BlockSpec(pool_w.shape, lambda i: (0, 0, 0)), _vec(wd)],
        out_shape=[_sds((s, wd)), _sds(pool_w.shape), _sds((1, wd))],
        scratch_shapes=[pltpu.VMEM((s + POOL_HALO, wd), f32)], name="pool_bwd",
        compiler_params=_params())(dcat, pooled, pool_w, pool_scale)


def _conv_fwd(hsm, dw, ln_g, ln_b):
    _, s, wd = hsm.shape
    rows, halo = CONV_ROWS, CONV_HALO

    def body(a_ref, g_ref, dw_ref, lg_ref, lb_ref, out_ref, hc_ref, xh_ref, rs_ref, pad_ref):
        hc = a_ref[...] * _sigmoid(g_ref[...])
        hc_ref[...] = hc
        pad_ref[0:halo, :] = jnp.zeros((halo, wd), f32)
        pad_ref[halo:halo + s, :] = hc
        taps = dw_ref[...]

        def chunk(c, _):
            base = pl.multiple_of(c * rows, rows)
            win = pad_ref[pl.ds(base, rows + halo), :]
            y = jnp.zeros((rows, wd), f32)
            for k in range(CONV_TAPS):
                lo = halo - (CONV_TAPS - 1) + k
                y = y + taps[k:k + 1, :] * win[lo:lo + rows]
            mu = jnp.mean(y, axis=-1, keepdims=True)
            cen = y - mu
            rstd = lax.rsqrt(jnp.mean(cen * cen, axis=-1, keepdims=True) + LN_EPS)
            xh = cen * rstd
            n = xh * lg_ref[...] + lb_ref[...]
            out_ref[pl.ds(base, rows), :] = n * _sigmoid(n)
            xh_ref[pl.ds(base, rows), :] = xh
            rs_ref[pl.ds(base, rows), :] = rstd
            return 0

        lax.fori_loop(0, s // rows, chunk, 0)

    full = pl.BlockSpec((s, wd), lambda i: (0, 0))
    return pl.pallas_call(
        body, grid=(1,),
        in_specs=[pl.BlockSpec((None, s, wd), lambda i: (0, 0, 0)), pl.BlockSpec((None, s, wd), lambda i: (1, 0, 0)),
                  pl.BlockSpec(dw.shape, lambda i: (0, 0)), _vec(wd), _vec(wd)],
        out_specs=[full, full, full, pl.BlockSpec((s, 1), lambda i: (0, 0))],
        out_shape=[_sds((s, wd))] * 3 + [_sds((s, 1))],
        scratch_shapes=[pltpu.VMEM((halo + s, wd), f32)], name="conv_fwd",
        compiler_params=_params())(hsm, hsm, dw, ln_g, ln_b)


def _conv_bwd(dcat, hsm, hc, xh, rstd, dw, ln_g, ln_b):
    s, wd = hc.shape
    rows, halo = CONV_ROWS, CONV_HALO

    def body(dc_ref, a_ref, g_ref, hc_ref, xh_ref, rs_ref, dw_ref, lg_ref, lb_ref,
             da_ref, dg_ref, ddw_ref, dlg_ref, dlb_ref, hpad_ref, ypad_ref):
        hpad_ref[0:halo, :] = jnp.zeros((halo, wd), f32)
        hpad_ref[halo:halo + s, :] = hc_ref[...]
        ypad_ref[s:s + halo, :] = jnp.zeros((halo, wd), f32)
        ddw_ref[...] = jnp.zeros_like(ddw_ref)
        dlg_ref[...] = jnp.zeros_like(dlg_ref)
        dlb_ref[...] = jnp.zeros_like(dlb_ref)
        taps = dw_ref[...]

        def norm_bwd(c, _):
            base = pl.multiple_of(c * rows, rows)
            xhv = xh_ref[pl.ds(base, rows), :]
            n = xhv * lg_ref[...] + lb_ref[...]
            sn = _sigmoid(n)
            dn = dc_ref[pl.ds(base, rows), :] * sn * (1.0 + n * (1.0 - sn))
            dlg_ref[...] += jnp.sum(dn * xhv, axis=0, keepdims=True)
            dlb_ref[...] += jnp.sum(dn, axis=0, keepdims=True)
            ypad_ref[pl.ds(base, rows), :] = _ln_bwd_rows(dn, xhv, rs_ref[pl.ds(base, rows), :], lg_ref[...])
            return 0

        lax.fori_loop(0, s // rows, norm_bwd, 0)

        def conv_bwd(c, _):
            base = pl.multiple_of(c * rows, rows)
            ywin = ypad_ref[pl.ds(base, rows + halo), :]
            hwin = hpad_ref[pl.ds(base, rows + halo), :]
            dy = ywin[0:rows]
            dhc = jnp.zeros((rows, wd), f32)
            for k in range(CONV_TAPS):
                fwd = CONV_TAPS - 1 - k
                dhc = dhc + taps[k:k + 1, :] * ywin[fwd:fwd + rows]
                lo = halo - (CONV_TAPS - 1) + k
                ddw_ref[k:k + 1, :] += jnp.sum(dy * hwin[lo:lo + rows], axis=0, keepdims=True)
            sg = _sigmoid(g_ref[pl.ds(base, rows), :])
            da_ref[pl.ds(base, rows), :] = dhc * sg
            dg_ref[pl.ds(base, rows), :] = dhc * a_ref[pl.ds(base, rows), :] * sg * (1.0 - sg)
            return 0

        lax.fori_loop(0, s // rows, conv_bwd, 0)

    full = pl.BlockSpec((s, wd), lambda i: (0, 0))
    tap_spec = pl.BlockSpec(dw.shape, lambda i: (0, 0))
    return pl.pallas_call(
        body, grid=(1,),
        in_specs=[full, pl.BlockSpec((None, s, wd), lambda i: (0, 0, 0)), pl.BlockSpec((None, s, wd), lambda i: (1, 0, 0)),
                  full, full, pl.BlockSpec((s, 1), lambda i: (0, 0)), tap_spec, _vec(wd), _vec(wd)],
        out_specs=[full, full, tap_spec, _vec(wd), _vec(wd)],
        out_shape=[_sds((s, wd)), _sds((s, wd)), _sds(dw.shape), _sds((1, wd)), _sds((1, wd))],
        scratch_shapes=[pltpu.VMEM((halo + s, wd), f32), pltpu.VMEM((s + halo, wd), f32)], name="conv_bwd",
        compiler_params=_params())(dcat, hsm, hsm, hc, xh, rstd, dw, ln_g, ln_b)


_GELU_C = 0.7978845608028654
_GELU_A = 0.044715


def _gelu(x):
    return 0.5 * x * (1.0 + jnp.tanh(_GELU_C * (x + _GELU_A * x * x * x)))


def _gelu_grad(x):
    th = jnp.tanh(_GELU_C * (x + _GELU_A * x * x * x))
    return 0.5 * (1.0 + th) + 0.5 * x * (1.0 - th * th) * _GELU_C * (1.0 + 3.0 * _GELU_A * x * x)


def _causal_sg_w(w_ref, g):
    row = lax.broadcasted_iota(jnp.int32, (SG_CHUNK, SG_CHUNK), 0)
    col = lax.broadcasted_iota(jnp.int32, (SG_CHUNK, SG_CHUNK), 1)
    return jnp.where(col <= row, w_ref[g], 0.0).astype(bf16), col <= row


def _gmlp_fwd(hsm, ln_g, ln_b, sg_w, sg_bt):
    _, s, wd = hsm.shape
    ck = SG_CHUNK

    def body(zu_ref, zv_ref, lg_ref, lb_ref, w_ref, bt_ref, out_ref, xh_ref, rs_ref):
        u = _gelu(zu_ref[...])
        vg = _gelu(zv_ref[...])
        mu = jnp.mean(vg, axis=-1, keepdims=True)
        cen = vg - mu
        rstd = lax.rsqrt(jnp.mean(cen * cen, axis=-1, keepdims=True) + LN_EPS)
        xh = cen * rstd
        xh_ref[...] = xh
        rs_ref[...] = rstd
        vn = (xh * lg_ref[...] + lb_ref[...]).astype(bf16)
        for g in range(4):
            cols = slice(g * GROUP_DIM, (g + 1) * GROUP_DIM)
            wm, _ = _causal_sg_w(w_ref, g)
            sv = _dot(wm, vn[:, cols]) + bt_ref[:, g:g + 1]
            out_ref[:, cols] = u[:, cols] * sv

    rows_spec = pl.BlockSpec((ck, wd), lambda i: (i, 0))
    return pl.pallas_call(
        body, grid=(s // ck,),
        in_specs=[pl.BlockSpec((None, ck, wd), lambda i: (2, i, 0)), pl.BlockSpec((None, ck, wd), lambda i: (3, i, 0)),
                  _vec(wd), _vec(wd), pl.BlockSpec(sg_w.shape, lambda i: (0, 0, 0)),
                  pl.BlockSpec(sg_bt.shape, lambda i: (0, 0))],
        out_specs=[rows_spec, rows_spec, pl.BlockSpec((ck, 1), lambda i: (i, 0))],
        out_shape=[_sds((s, wd)), _sds((s, wd)), _sds((s, 1))], name="gmlp_fwd",
        compiler_params=_params())(hsm, hsm, ln_g, ln_b, sg_w, sg_bt)


def _gmlp_bwd(dcat, hsm, xh, rstd, ln_g, ln_b, sg_w, sg_bt):
    s, wd = xh.shape
    ck = SG_CHUNK

    def body(dd_ref, zu_ref, zv_ref, xh_ref, rs_ref, lg_ref, lb_ref, w_ref, bt_ref,
             dzu_ref, dzv_ref, dw_ref, dbb_ref, dlg_ref, dlb_ref):
        @pl.when(pl.program_id(0) == 0)
        def _():
            dw_ref[...] = jnp.zeros_like(dw_ref)
            dbb_ref[...] = jnp.zeros_like(dbb_ref)
            dlg_ref[...] = jnp.zeros_like(dlg_ref)
            dlb_ref[...] = jnp.zeros_like(dlb_ref)

        zu, zv, dd, xhv = zu_ref[...], zv_ref[...], dd_ref[...], xh_ref[...]
        u = _gelu(zu)
        vn = (xhv * lg_ref[...] + lb_ref[...]).astype(bf16)
        du_parts, dvn_parts = [], []
        for g in range(4):
            cols = slice(g * GROUP_DIM, (g + 1) * GROUP_DIM)
            wm, keep = _causal_sg_w(w_ref, g)
            sv = _dot(wm, vn[:, cols]) + bt_ref[:, g:g + 1]
            du_parts.append(dd[:, cols] * sv)
            dsv = dd[:, cols] * u[:, cols]
            dsvb = dsv.astype(bf16)
            dbb_ref[g] += jnp.broadcast_to(jnp.sum(dsv, axis=1, keepdims=True), (ck, GROUP_DIM))
            dw_ref[g] += jnp.where(keep, _dot(dsvb, vn[:, cols], _NT), 0.0)
            dvn_parts.append(_dot(wm, dsvb, _TN))
        du = jnp.concatenate(du_parts, axis=1)
        dvn = jnp.concatenate(dvn_parts, axis=1)
        dlg_ref[...] += jnp.sum(dvn * xhv, axis=0, keepdims=True)
        dlb_ref[...] += jnp.sum(dvn, axis=0, keepdims=True)
        dzv_ref[...] = _ln_bwd_rows(dvn, xhv, rs_ref[...], lg_ref[...]) * _gelu_grad(zv)
        dzu_ref[...] = du * _gelu_grad(zu)

    rows_spec = pl.BlockSpec((ck, wd), lambda i: (i, 0))
    wspec = pl.BlockSpec(sg_w.shape, lambda i: (0, 0, 0))
    return pl.pallas_call(
        body, grid=(s // ck,),
        in_specs=[pl.BlockSpec((ck, wd), lambda i: (i, 1)), pl.BlockSpec((None, ck, wd), lambda i: (2, i, 0)),
                  pl.BlockSpec((None, ck, wd), lambda i: (3, i, 0)), rows_spec, pl.BlockSpec((ck, 1), lambda i: (i, 0)),
                  _vec(wd), _vec(wd), wspec, pl.BlockSpec(sg_bt.shape, lambda i: (0, 0))],
        out_specs=[rows_spec, rows_spec, wspec, wspec, _vec(wd), _vec(wd)],
        out_shape=[_sds((s, wd)), _sds((s, wd)), _sds(sg_w.shape), _sds(sg_w.shape), _sds((1, wd)), _sds((1, wd))],
        name="gmlp_bwd", compiler_params=_params())(dcat, hsm, hsm, xh, rstd, ln_g, ln_b, sg_w, sg_bt)


def _to_heads(x2d):
    s = x2d.shape[0]
    return jnp.transpose(x2d.reshape(s, N_HEADS, HEAD_DIM), (1, 0, 2))


def _from_heads(x3d):
    s = x3d.shape[1]
    return jnp.transpose(x3d, (1, 0, 2)).reshape(s, N_HEADS * HEAD_DIM)


def _local_step(x, p, target, wts, late_shards, small):
    wts = dict(wts)
    saved = []
    for i in range(DEPTH):
        tag = f"l{i}"
        if i % 2 == 0:
            hsm = _mm_nn_col(tag + "_in", x, wts["even_w_in"], 0)
            q, k, v = _to_heads(hsm[0]), _to_heads(hsm[1]), _to_heads(hsm[2])
            att_heads, sba_tot, late = _sba_fwd(q, k, v, late_shards)
            wts.update(zip(_LATE, late))
            att = _from_heads(att_heads)
            pool_out, pooled = _pool_fwd(hsm, small["pool_w"], small["pool_scale"])
            cat = jnp.concatenate([att, pool_out], axis=1)
            mix = _mm_nn_row(tag + "_out", cat, wts["even_w_out"], 0)
            mixer_saved = (hsm, q, k, v, sba_tot, pooled, cat)
        else:
            hsm = _mm_nn_col(tag + "_in", x, wts["odd_w_in"], 0)
            conv_out, hc, cxh, crs = _conv_fwd(hsm, small["conv_dw"], small["conv_ln_g"], small["conv_ln_b"])
            sg_out, sxh, srs = _gmlp_fwd(hsm, small["sg_ln_g"], small["sg_ln_b"], small["sg_w"], small["sg_bt"])
            cat = jnp.concatenate([conv_out, sg_out], axis=1)
            mix = _mm_nn_row(tag + "_out", cat, wts["odd_w_out"], 0)
            mixer_saved = (hsm, hc, cxh, crs, sxh, srs, cat)
        x1, xh1, rs1 = _ln_fwd(tag + "_ln_mix", x, mix, small["ln_mix_g"][i:i + 1], small["ln_mix_b"][i:i + 1])
        hg = _mm_nn_col(tag + "_gate", x1, wts["ffn_w_gate"], i)
        hu = _mm_nn_col(tag + "_up", x1, wts["ffn_w_up"], i)
        act = _swiglu_fwd(tag + "_swiglu", hg, hu)
        ffn = _mm_nn_row(tag + "_down", act, wts["ffn_w_down"], i)
        x2, xh2, rs2 = _ln_fwd(tag + "_ln_ffn", x1, ffn, small["ln_ffn_g"][i:i + 1], small["ln_ffn_b"][i:i + 1])
        gp = _mm_nn_row(tag + "_ple_gate", x2, wts["ple_w_gate"], i)
        pp = _mm_nn_col(tag + "_ple_proj", p[i], wts["ple_w_proj"], i, natural=True)
        x3 = _ple_fwd(tag + "_ple", x2, gp, small["ple_b_gate"][i:i + 1], pp)
        saved.append((x, mixer_saved, x1, xh1, rs1, hg, hu, act, x2, xh2, rs2, gp, pp))
        x = x3

    loss_part, dx = _loss_head("loss_head", x, target)

    big = {n: [None] * wts[n].shape[1] for n in wts}
    received = {}
    sm = {}
    per_layer = {n: [None] * DEPTH for n in ("ln_mix_g", "ln_mix_b", "ln_ffn_g", "ln_ffn_b", "ple_b_gate")}
    for i in reversed(range(DEPTH)):
        tag = f"l{i}b"
        x0, mixer_saved, x1, xh1, rs1, hg, hu, act, x2, xh2, rs2, gp, pp = saved[i]
        dgp, dpp, per_layer["ple_b_gate"][i] = _ple_bwd(tag + "_ple", dx, gp, small["ple_b_gate"][i:i + 1], pp)
        big["ple_w_proj"][i] = _mm_tn_col(tag + "_dproj", p[i], dpp)
        big["ple_w_gate"][i] = _mm_tn_row(tag + "_dgate", x2, dgp)
        dx2 = _mm_nt_row(tag + "_dx2", dgp, wts["ple_w_gate"], i, natural=True, add=dx)
        dr2, per_layer["ln_ffn_g"][i], per_layer["ln_ffn_b"][i] = _ln_bwd(tag + "_ln_ffn", dx2, xh2, rs2,
                                                                           small["ln_ffn_g"][i:i + 1])
        dact = _mm_nt_row(tag + "_dact", dr2, wts["ffn_w_down"], i)
        big["ffn_w_down"][i] = _mm_tn_row(tag + "_ddown", act, dr2)
        dhg, dhu = _swiglu_bwd(tag + "_swiglu", dact, hg, hu)
        big["ffn_w_gate"][i] = _mm_tn_col(tag + "_dgatew", x1, dhg)
        big["ffn_w_up"][i] = _mm_tn_col(tag + "_dupw", x1, dhu)
        part = _mm_nt_col(tag + "_dx1a", dhg, wts["ffn_w_gate"], i, dr2, ALPHA)
        dx1 = _mm_nt_col(tag + "_dx1b", dhu, wts["ffn_w_up"], i, part, 1.0)
        dr1, per_layer["ln_mix_g"][i], per_layer["ln_mix_b"][i] = _ln_bwd(tag + "_ln_mix", dx1, xh1, rs1,
                                                                           small["ln_mix_g"][i:i + 1])
        if i % 2 == 0:
            hsm, q, k, v, sba_tot, pooled, cat = mixer_saved
            big["even_w_out"][0] = _mm_tn_row(tag + "_dout", cat, dr1)
            dcat = _mm_nt_row(tag + "_dcat", dr1, wts["even_w_out"], 0, natural=True)
            entries, layout = _scatter_plan(big, _LATE)
            dq, dk, dv, got = _sba_bwd(q, k, v, sba_tot, _to_heads(dcat[:, :N_HEADS * HEAD_DIM]), entries, layout)
            received.update(zip(_LATE, got))
            du, sm["pool_w"], sm["pool_scale"] = _pool_bwd(dcat, pooled, small["pool_w"], small["pool_scale"])
            dhsm = jnp.stack([_from_heads(dq), _from_heads(dk), _from_heads(dv), du])
            w_in = "even_w_in"
        else:
            hsm, hc, cxh, crs, sxh, srs, cat = mixer_saved
            big["odd_w_out"][0] = _mm_tn_row(tag + "_dout", cat, dr1)
            dcat = _mm_nt_row(tag + "_dcat", dr1, wts["odd_w_out"], 0, natural=True)
            da, dg, sm["conv_dw"], sm["conv_ln_g"], sm["conv_ln_b"] = _conv_bwd(
                dcat, hsm, hc, cxh, crs, small["conv_dw"], small["conv_ln_g"], small["conv_ln_b"])
            dzu, dzv, sm["sg_w"], dsgb, sm["sg_ln_g"], sm["sg_ln_b"] = _gmlp_bwd(
                dcat, hsm, sxh, srs, small["sg_ln_g"], small["sg_ln_b"], small["sg_w"], small["sg_bt"])
            sm["sg_b"] = dsgb[:, :, 0]
            dhsm = jnp.stack([da, dg, dzu, dzv])
            w_in = "odd_w_in"
        big[w_in][0] = _mm_tn_col(tag + "_din", x0, dhsm)
        dx = _mm_nt_col(tag + "_dx", dhsm, wts[w_in], 0, dr1, ALPHA)
    for n, parts in per_layer.items():
        sm[n] = jnp.concatenate(parts, axis=0)
    received.update(zip(_EARLY, _scatter_chips(*_scatter_plan(big, _EARLY))))
    return loss_part, dx, received, sm


def _scatter_plan(big, names):
    entries, layout = [], []
    for pi, n in enumerate(names):
        for li, g in enumerate(big[n]):
            entries.append(g)
            layout.append((pi, li))
    return entries, layout


def _place():
    x, y, c = lax.axis_index("x"), lax.axis_index("y"), lax.axis_index("c")
    return x, y, c, [(1 - x, y), (x, 1 - y), (1 - x, 1 - y)]


def _gather_chips(shards):
    n = len(shards)

    def body(*refs):
        copies = _gather_copies(refs[:n], refs[n:2 * n], *refs[2 * n:])
        for cp in copies:
            cp.start()
        for cp in copies:
            cp.wait()

    return pl.pallas_call(body, in_specs=[_ANY] * n, out_specs=[_ANY] * n, out_shape=_gather_shapes(shards),
                          scratch_shapes=_comm_sems(n), name="gather_chips")(*shards)


def _comm_sems(n):
    return [pltpu.SemaphoreType.DMA((n, 3)), pltpu.SemaphoreType.DMA((n, 3)), pltpu.SemaphoreType.DMA((n,))]


def _gather_shapes(shards):
    return [_sds((N_CHIPS,) + a.shape, a.dtype) for a in shards]


def _gather_copies(ins, outs, send, recv, loc):
    x, y, c, chips = _place()
    mine = 2 * x + y
    copies = []
    for t in range(len(ins)):
        copies.append(pltpu.make_async_copy(ins[t], outs[t].at[mine], loc.at[t]))
        for j, (cx, cy) in enumerate(chips):
            copies.append(pltpu.make_async_remote_copy(
                src_ref=ins[t], dst_ref=outs[t].at[mine], send_sem=send.at[t, j], recv_sem=recv.at[t, j],
                device_id=(cx, cy, c), device_id_type=_MESH))
    return copies


def _scatter_shapes(grads, layout):
    shapes = {}
    for e, (pi, li) in enumerate(layout):
        r, cdim = grads[e].shape[1:]
        shapes[pi] = (N_CHIPS, max(li + 1, shapes.get(pi, (0, 0))[1]), r, cdim)
    return [_sds(shapes[pi], grads[0].dtype) for pi in range(len(shapes))]


def _scatter_copies(ins, outs, layout, send, recv, loc):
    x, y, c, chips = _place()
    mine = 2 * x + y
    copies = []
    for e, (pi, li) in enumerate(layout):
        copies.append(pltpu.make_async_copy(ins[e].at[mine], outs[pi].at[mine, li], loc.at[e]))
        for j, (cx, cy) in enumerate(chips):
            copies.append(pltpu.make_async_remote_copy(
                src_ref=ins[e].at[2 * cx + cy], dst_ref=outs[pi].at[mine, li], send_sem=send.at[e, j],
                recv_sem=recv.at[e, j], device_id=(cx, cy, c), device_id_type=_MESH))
    return copies


def _scatter_chips(grads, layout):
    n_in = len(grads)
    out_shape = _scatter_shapes(grads, layout)
    n_out = len(out_shape)

    def body(*refs):
        copies = _scatter_copies(refs[:n_in], refs[n_in:n_in + n_out], layout, *refs[n_in + n_out:])
        for cp in copies:
            cp.start()
        for cp in copies:
            cp.wait()

    return pl.pallas_call(body, in_specs=[_ANY] * n_in, out_specs=[_ANY] * n_out, out_shape=out_shape,
                          scratch_shapes=_comm_sems(n_in), name="scatter_chips")(*grads)


def _swap_cores(arrays):
    n = len(arrays)

    def body(*refs):
        ins, outs = refs[:n], refs[n:2 * n]
        send, recv = refs[2 * n:]
        x, y, c, _ = _place()
        started = []
        for t in range(n):
            rc = pltpu.make_async_remote_copy(src_ref=ins[t], dst_ref=outs[t], send_sem=send.at[t], recv_sem=recv.at[t],
                                              device_id=(x, y, 1 - c), device_id_type=_MESH)
            rc.start()
            started.append(rc)
        for rc in started:
            rc.wait()

    return pl.pallas_call(
        body, in_specs=[_ANY] * n, out_specs=[_ANY] * n, out_shape=[_sds(a.shape, a.dtype) for a in arrays],
        scratch_shapes=[pltpu.SemaphoreType.DMA((n,)), pltpu.SemaphoreType.DMA((n,))], name="swap_cores")(*arrays)


def _gather_all(block):
    def body(in_ref, out_ref, send, recv, loc):
        x, y, c, _ = _place()
        mine = 4 * x + 2 * y + c
        cp = pltpu.make_async_copy(in_ref, out_ref.at[mine], loc)
        cp.start()
        started = [cp]
        for m in range(1, N_DEV):
            fx, fy, fc = (m >> 2) & 1, (m >> 1) & 1, m & 1
            peer = (x + fx - 2 * x * fx, y + fy - 2 * y * fy, c + fc - 2 * c * fc)
            rc = pltpu.make_async_remote_copy(src_ref=in_ref, dst_ref=out_ref.at[mine], send_sem=send.at[m - 1],
                                              recv_sem=recv.at[m - 1], device_id=peer, device_id_type=_MESH)
            rc.start()
            started.append(rc)
        for rc in started:
            rc.wait()

    return pl.pallas_call(
        body, in_specs=[_ANY], out_specs=_ANY, out_shape=_sds((N_DEV,) + block.shape),
        scratch_shapes=[pltpu.SemaphoreType.DMA((N_DEV - 1,)), pltpu.SemaphoreType.DMA((N_DEV - 1,)),
                        pltpu.SemaphoreType.DMA(())], name="gather_all")(block)


def _row_tile(r):
    for t in (256, 128, 64, 32, 16, 8):
        if r % t == 0:
            return t
    return r


def _sum_stack(name, stack):
    n, r, c = stack.shape
    tr = _row_tile(r)

    def body(s_ref, o_ref):
        acc = s_ref[0].astype(f32)
        for t in range(1, n):
            acc = acc + s_ref[t].astype(f32)
        o_ref[...] = acc

    return pl.pallas_call(body, grid=(r // tr,), in_specs=[pl.BlockSpec((n, tr, c), lambda i: (0, i, 0))],
                          out_specs=pl.BlockSpec((tr, c), lambda i: (i, 0)), out_shape=_sds((r, c)), name=name,
                          compiler_params=_params())(stack)


def _adamw(name, w, g_a, g_b, m, v):
    r, c = w.shape
    tr = _row_tile(r)
    two = g_b is not None
    bc1 = 1.0 - ADAM_B1 ** ADAM_STEP
    bc2 = 1.0 - ADAM_B2 ** ADAM_STEP

    def body(*refs):
        if two:
            w_ref, ga_ref, gb_ref, m_ref, v_ref, g_out, d_out, m_out, v_out = refs
            g = ga_ref[...] + gb_ref[...]
        else:
            w_ref, ga_ref, m_ref, v_ref, g_out, d_out, m_out, v_out = refs
            g = ga_ref[...]
        m_new = ADAM_B1 * m_ref[...] + (1.0 - ADAM_B1) * g
        v_new = ADAM_B2 * v_ref[...] + (1.0 - ADAM_B2) * (g * g)
        g_out[...] = g
        m_out[...] = m_new
        v_out[...] = v_new
        d_out[...] = -ADAM_LR * ((m_new / bc1) / (jnp.sqrt(v_new / bc2) + ADAM_EPS) + ADAM_WD * w_ref[...])

    spec = pl.BlockSpec((tr, c), lambda i: (i, 0))
    ins = [w, g_a] + ([g_b] if two else []) + [m, v]
    return pl.pallas_call(body, grid=(r // tr,), in_specs=[spec] * len(ins), out_specs=[spec] * 4,
                          out_shape=[_sds((r, c))] * 4, name=name, compiler_params=_params())(*ins)


_EARLY = ("even_w_in", "even_w_out")
_LATE = ("odd_w_in", "odd_w_out", "ffn_w_gate", "ffn_w_up", "ffn_w_down", "ple_w_proj", "ple_w_gate")
_BIG = _EARLY + _LATE
_SHARDED_SMALL = ("conv_dw", "conv_ln_g", "conv_ln_b", "sg_ln_g", "sg_ln_b")
_SMALL = ("pool_w", "pool_scale", "conv_dw", "conv_ln_g", "conv_ln_b", "sg_ln_g", "sg_ln_b", "sg_w", "sg_b",
          "ln_mix_g", "ln_mix_b", "ln_ffn_g", "ln_ffn_b", "ple_b_gate")
_WEIGHTS = ("even_w_in", "even_w_out", "pool_w", "pool_scale", "odd_w_in", "odd_w_out", "conv_dw", "conv_ln_g",
            "conv_ln_b", "sg_ln_g", "sg_ln_b", "sg_w", "sg_b", "ln_mix_g", "ln_mix_b", "ffn_w_gate", "ffn_w_up",
            "ffn_w_down", "ln_ffn_g", "ln_ffn_b", "ple_w_proj", "ple_w_gate", "ple_b_gate")


def _pack(arrays):
    flat = jnp.concatenate([a.reshape(-1) for a in arrays])
    pad = (-flat.shape[0]) % (256 * LANES)
    return jnp.pad(flat, (0, pad)).reshape(-1, LANES)


def _unpack(packed, shapes):
    flat = packed.reshape(-1)
    out, off = [], 0
    for shp in shapes:
        size = 1
        for dim in shp:
            size *= dim
        out.append(flat[off:off + size].reshape(shp))
        off += size
    return out


def _unshard_last(g4):
    return jnp.concatenate([g4[k] for k in range(N_CHIPS)], axis=-1)


def kernel(x, p, even_w_in, even_w_out, pool_w, pool_scale, odd_w_in, odd_w_out, conv_dw, conv_ln_g, conv_ln_b, sg_ln_g, sg_ln_b, sg_w, sg_b, ln_mix_g, ln_mix_b, ffn_w_gate, ffn_w_up, ffn_w_down, ln_ffn_g, ln_ffn_b, ple_w_proj, ple_w_gate, ple_b_gate, loss_target, m_even_w_in, m_even_w_out, m_pool_w, m_pool_scale, m_odd_w_in, m_odd_w_out, m_conv_dw, m_conv_ln_g, m_conv_ln_b, m_sg_ln_g, m_sg_ln_b, m_sg_w, m_sg_b, m_ln_mix_g, m_ln_mix_b, m_ffn_w_gate, m_ffn_w_up, m_ffn_w_down, m_ln_ffn_g, m_ln_ffn_b, m_ple_w_proj, m_ple_w_gate, m_ple_b_gate, v_even_w_in, v_even_w_out, v_pool_w, v_pool_scale, v_odd_w_in, v_odd_w_out, v_conv_dw, v_conv_ln_g, v_conv_ln_b, v_sg_ln_g, v_sg_ln_b, v_sg_w, v_sg_b, v_ln_mix_g, v_ln_mix_b, v_ffn_w_gate, v_ffn_w_up, v_ffn_w_down, v_ln_ffn_g, v_ln_ffn_b, v_ple_w_proj, v_ple_w_gate, v_ple_b_gate):
    w = dict(even_w_in=even_w_in, even_w_out=even_w_out, pool_w=pool_w, pool_scale=pool_scale, odd_w_in=odd_w_in,
             odd_w_out=odd_w_out, conv_dw=conv_dw, conv_ln_g=conv_ln_g, conv_ln_b=conv_ln_b, sg_ln_g=sg_ln_g,
             sg_ln_b=sg_ln_b, sg_w=sg_w, sg_b=sg_b, ln_mix_g=ln_mix_g, ln_mix_b=ln_mix_b, ffn_w_gate=ffn_w_gate,
             ffn_w_up=ffn_w_up, ffn_w_down=ffn_w_down, ln_ffn_g=ln_ffn_g, ln_ffn_b=ln_ffn_b, ple_w_proj=ple_w_proj,
             ple_w_gate=ple_w_gate, ple_b_gate=ple_b_gate)
    mom = dict(even_w_in=m_even_w_in, even_w_out=m_even_w_out, pool_w=m_pool_w, pool_scale=m_pool_scale,
               odd_w_in=m_odd_w_in, odd_w_out=m_odd_w_out, conv_dw=m_conv_dw, conv_ln_g=m_conv_ln_g,
               conv_ln_b=m_conv_ln_b, sg_ln_g=m_sg_ln_g, sg_ln_b=m_sg_ln_b, sg_w=m_sg_w, sg_b=m_sg_b,
               ln_mix_g=m_ln_mix_g, ln_mix_b=m_ln_mix_b, ffn_w_gate=m_ffn_w_gate, ffn_w_up=m_ffn_w_up,
               ffn_w_down=m_ffn_w_down, ln_ffn_g=m_ln_ffn_g, ln_ffn_b=m_ln_ffn_b, ple_w_proj=m_ple_w_proj,
               ple_w_gate=m_ple_w_gate, ple_b_gate=m_ple_b_gate)
    var = dict(even_w_in=v_even_w_in, even_w_out=v_even_w_out, pool_w=v_pool_w, pool_scale=v_pool_scale,
               odd_w_in=v_odd_w_in, odd_w_out=v_odd_w_out, conv_dw=v_conv_dw, conv_ln_g=v_conv_ln_g,
               conv_ln_b=v_conv_ln_b, sg_ln_g=v_sg_ln_g, sg_ln_b=v_sg_ln_b, sg_w=v_sg_w, sg_b=v_sg_b,
               ln_mix_g=v_ln_mix_g, ln_mix_b=v_ln_mix_b, ffn_w_gate=v_ffn_w_gate, ffn_w_up=v_ffn_w_up,
               ffn_w_down=v_ffn_w_down, ln_ffn_g=v_ln_ffn_g, ln_ffn_b=v_ln_ffn_b, ple_w_proj=v_ple_w_proj,
               ple_w_gate=v_ple_w_gate, ple_b_gate=v_ple_b_gate)

    gathered = _gather_chips([w[n].astype(bf16) for n in _EARLY] + [w[n] for n in _SHARDED_SMALL])
    wts = dict(zip(_EARLY, gathered[:len(_EARLY)]))
    small = {n: w[n][0] for n in ("pool_w", "sg_w")}
    small.update({n: w[n] for n in ("pool_scale", "ln_mix_g", "ln_mix_b", "ln_ffn_g", "ln_ffn_b", "ple_b_gate")})
    small["sg_bt"] = jnp.transpose(w["sg_b"][0])
    for n, g4 in zip(_SHARDED_SMALL, gathered[len(_EARLY):]):
        small[n] = _unshard_last(g4)[0]
        if n != "conv_dw":
            small[n] = small[n][None]

    loss_part, grad_x, received, sm = _local_step(x[0], p[:, 0], loss_target[0], wts,
                                                  [w[n].astype(bf16) for n in _LATE], small)

    chip_sums = []
    for n in _BIG:
        _, nl, r, c = received[n].shape
        chip_sums.append(_sum_stack("sum_" + n, received[n].reshape(N_CHIPS, nl * r, c)))
    other = _swap_cores(chip_sums)
    results = {}
    for n, mine, theirs in zip(_BIG, chip_sums, other):
        shp = w[n].shape
        flat = (shp[0] * shp[1], shp[2])
        outs = _adamw("adamw_" + n, w[n].reshape(flat), mine, theirs, mom[n].reshape(flat), var[n].reshape(flat))
        results[n] = [o.reshape(shp) for o in outs]

    sm_shapes = [(1,) + sm[n].shape if n in ("pool_w", "sg_w", "conv_dw", "sg_b") else sm[n].shape for n in _SMALL]
    packed = _pack([sm[n] for n in _SMALL] + [loss_part[0, 0:1]])
    total = _sum_stack("sum_small", _gather_all(packed))
    parts = _unpack(total, sm_shapes + [(1,)])
    loss = parts[-1][0]
    chip = 2 * lax.axis_index("x") + lax.axis_index("y")
    g_small = {}
    for n, g in zip(_SMALL, parts[:-1]):
        if n in _SHARDED_SMALL:
            width = w[n].shape[-1]
            g = lax.dynamic_slice_in_dim(g, chip * width, width, axis=g.ndim - 1)
        g_small[n] = g
    shapes = [w[n].shape for n in _SMALL]
    outs = _adamw("adamw_small", _pack([w[n] for n in _SMALL]), _pack([g_small[n] for n in _SMALL]), None,
                  _pack([mom[n] for n in _SMALL]), _pack([var[n] for n in _SMALL]))
    unpacked = [_unpack(o, shapes) for o in outs]
    for idx, n in enumerate(_SMALL):
        results[n] = [u[idx] for u in unpacked]

    return (loss, grad_x[None], *[results[n][0] for n in _WEIGHTS], *[results[n][1] for n in _WEIGHTS],
            *[results[n][2] for n in _WEIGHTS], *[results[n][3] for n in _WEIGHTS])
```

```python
import jax
import jax.numpy as jnp
from jax import lax
from jax.experimental import pallas as pl
from jax.experimental.pallas import tpu as pltpu

f32 = jnp.float32
bf16 = jnp.bfloat16

D_MODEL = 1024
N_HEADS = 8
HEAD_DIM = 64
Q_BLOCK = 128
POOL_WINDOWS = (2, 4, 8, 16)
GROUP_DIM = 128
CONV_TAPS = 31
SG_CHUNK = 128
DEPTH = 2
ALPHA = (2 * DEPTH) ** 0.25
LN_EPS = 1e-5
SB_SCALE = HEAD_DIM ** -0.5
ADAM_LR, ADAM_B1, ADAM_B2, ADAM_EPS, ADAM_WD, ADAM_STEP = 0.001, 0.9, 0.999, 1e-08, 0.01, 10
N_CHIPS = 4
N_DEV = 8
LANES = 128
VMEM_LIMIT = 56 * 1024 * 1024
TM = 512
TR = 256
SBA_FWD_HEADS = 8
SBA_BWD_HEADS = 8
CONV_ROWS = 64
CONV_HALO = 32
POOL_HALO = 16

_NN = (((1,), (0,)), ((), ()))
_NT = (((1,), (1,)), ((), ()))
_TN = (((0,), (0,)), ((), ()))
_ANY = pl.BlockSpec(memory_space=pl.ANY)
_MESH = pl.DeviceIdType.MESH


def _params():
    return pltpu.CompilerParams(vmem_limit_bytes=VMEM_LIMIT)


def _sds(shape, dtype=f32):
    return jax.ShapeDtypeStruct(tuple(shape), dtype)


def _dot(a, b, dims=_NN):
    return lax.dot_general(a, b, dims, preferred_element_type=f32)


def _sigmoid(x):
    return 1.0 / (1.0 + jnp.exp(-x))


def _mm(name, a, b, grid, a_spec, b_spec, out_shape, out_spec, dims, reduce=False, add=None, add_spec=None,
        add_scale=1.0, out_dtype=f32):
    has_add = add is not None
    k_axis = len(grid) - 1

    def body(*refs):
        if has_add:
            a_ref, b_ref, add_ref, o_ref = refs
        else:
            a_ref, b_ref, o_ref = refs
        r = _dot(a_ref[...].astype(bf16), b_ref[...].astype(bf16), dims)
        if reduce:
            k = pl.program_id(k_axis)

            @pl.when(k == 0)
            def _():
                o_ref[...] = r + add_scale * add_ref[...] if has_add else r

            @pl.when(k > 0)
            def _():
                o_ref[...] += r
        else:
            o_ref[...] = (r + add_scale * add_ref[...] if has_add else r).astype(out_dtype)

    ins = [a, b] + ([add] if has_add else [])
    specs = [a_spec, b_spec] + ([add_spec] if has_add else [])
    return pl.pallas_call(body, grid=grid, in_specs=specs, out_specs=out_spec, out_shape=_sds(out_shape, out_dtype),
                          name=name, compiler_params=_params())(*ins)


def _tm(s):
    return min(TM, s)


def _act_spec(a, tm, width):
    if a.ndim == 3:
        return pl.BlockSpec((None, tm, width), lambda i, k: (k, i, 0))
    return pl.BlockSpec((tm, width), lambda i, k: (i, k))


def _mm_nn_col(name, x, w4, layer, natural=False):
    s, kk = x.shape
    nq = w4.shape[3]
    tm = _tm(s)
    if natural:
        out_shape, out_spec = (s, 4 * nq), pl.BlockSpec((tm, nq), lambda i, k: (i, k))
    else:
        out_shape, out_spec = (4, s, nq), pl.BlockSpec((None, tm, nq), lambda i, k: (k, i, 0))
    return _mm(name, x, w4, (s // tm, 4), pl.BlockSpec((tm, kk), lambda i, k: (i, 0)),
               pl.BlockSpec((None, None, kk, nq), lambda i, k: (k, layer, 0, 0)), out_shape, out_spec, _NN)


def _mm_nn_row(name, a, w4, layer):
    s = a.shape[-2]
    kq, n = w4.shape[2], w4.shape[3]
    tm = _tm(s)
    return _mm(name, a, w4, (s // tm, 4), _act_spec(a, tm, kq),
               pl.BlockSpec((None, None, kq, n), lambda i, k: (k, layer, 0, 0)), (s, n),
               pl.BlockSpec((tm, n), lambda i, k: (i, 0)), _NN, reduce=True)


def _mm_nt_col(name, dh, w4, layer, add, add_scale):
    s = dh.shape[-2]
    kk, nq = w4.shape[2], w4.shape[3]
    tm = _tm(s)
    row = pl.BlockSpec((tm, kk), lambda i, k: (i, 0))
    return _mm(name, dh, w4, (s // tm, 4), _act_spec(dh, tm, nq),
               pl.BlockSpec((None, None, kk, nq), lambda i, k: (k, layer, 0, 0)), (s, kk), row, _NT,
               reduce=True, add=add, add_spec=row, add_scale=add_scale)


def _mm_nt_row(name, dy, w4, layer, natural=False, add=None):
    s, n = dy.shape
    kq = w4.shape[2]
    tm = _tm(s)
    if natural:
        out_shape, out_spec = (s, 4 * kq), pl.BlockSpec((tm, kq), lambda i, k: (i, k))
    else:
        out_shape, out_spec = (4, s, kq), pl.BlockSpec((None, tm, kq), lambda i, k: (k, i, 0))
    return _mm(name, dy, w4, (s // tm, 4), pl.BlockSpec((tm, n), lambda i, k: (i, 0)),
               pl.BlockSpec((None, None, kq, n), lambda i, k: (k, layer, 0, 0)), out_shape, out_spec, _NT,
               add=add, add_spec=out_spec if add is not None else None)


def _mm_tn_col(name, x, dh):
    s, kk = x.shape
    if dh.ndim == 3:
        nq = dh.shape[2]
        b_spec = pl.BlockSpec((None, s, nq), lambda k, j: (k, 0, 0))
    else:
        nq = dh.shape[1] // 4
        b_spec = pl.BlockSpec((s, nq), lambda k, j: (0, k))
    tk = min(256, kk)
    return _mm(name, x, dh, (4, kk // tk), pl.BlockSpec((s, tk), lambda k, j: (0, j)), b_spec, (4, kk, nq),
               pl.BlockSpec((None, tk, nq), lambda k, j: (k, j, 0)), _TN, out_dtype=bf16)


def _mm_tn_row(name, a, dy):
    s, n = dy.shape
    if a.ndim == 3:
        kq = a.shape[2]
        a_spec = pl.BlockSpec((None, s, kq), lambda k, j: (k, 0, 0))
    else:
        kq = a.shape[1] // 4
        a_spec = pl.BlockSpec((s, kq), lambda k, j: (0, k))
    tn = min(512, n)
    return _mm(name, a, dy, (4, n // tn), a_spec, pl.BlockSpec((s, tn), lambda k, j: (0, j)), (4, kq, n),
               pl.BlockSpec((None, kq, tn), lambda k, j: (k, 0, j)), _TN, out_dtype=bf16)


def _tr(s):
    return min(TR, s)


def _rows(tm, d):
    return pl.BlockSpec((tm, d), lambda i: (i, 0))


def _vec(d):
    return pl.BlockSpec((1, d), lambda i: (0, 0))


def _ln_fwd(name, x, mix, g, b):
    s, d = x.shape
    tm = _tr(s)

    def body(x_ref, m_ref, g_ref, b_ref, y_ref, xh_ref, rs_ref):
        r = ALPHA * x_ref[...] + m_ref[...]
        mu = jnp.mean(r, axis=-1, keepdims=True)
        c = r - mu
        rstd = lax.rsqrt(jnp.mean(c * c, axis=-1, keepdims=True) + LN_EPS)
        xh = c * rstd
        y_ref[...] = xh * g_ref[...] + b_ref[...]
        xh_ref[...] = xh
        rs_ref[...] = rstd

    return pl.pallas_call(
        body, grid=(s // tm,), in_specs=[_rows(tm, d), _rows(tm, d), _vec(d), _vec(d)],
        out_specs=[_rows(tm, d), _rows(tm, d), _rows(tm, 1)],
        out_shape=[_sds((s, d)), _sds((s, d)), _sds((s, 1))], name=name, compiler_params=_params())(x, mix, g, b)


def _ln_bwd_rows(dy, xh, rstd, g):
    dxh = dy * g
    m1 = jnp.mean(dxh, axis=-1, keepdims=True)
    m2 = jnp.mean(dxh * xh, axis=-1, keepdims=True)
    return rstd * (dxh - m1 - xh * m2)


def _ln_bwd(name, dy, xh, rstd, g):
    s, d = dy.shape
    tm = _tr(s)

    def body(dy_ref, xh_ref, rs_ref, g_ref, dr_ref, dg_ref, db_ref):
        @pl.when(pl.program_id(0) == 0)
        def _():
            dg_ref[...] = jnp.zeros_like(dg_ref)
            db_ref[...] = jnp.zeros_like(db_ref)

        dyv, xhv = dy_ref[...], xh_ref[...]
        dr_ref[...] = _ln_bwd_rows(dyv, xhv, rs_ref[...], g_ref[...])
        dg_ref[...] += jnp.sum(dyv * xhv, axis=0, keepdims=True)
        db_ref[...] += jnp.sum(dyv, axis=0, keepdims=True)

    return pl.pallas_call(
        body, grid=(s // tm,), in_specs=[_rows(tm, d), _rows(tm, d), _rows(tm, 1), _vec(d)],
        out_specs=[_rows(tm, d), _vec(d), _vec(d)],
        out_shape=[_sds((s, d)), _sds((1, d)), _sds((1, d))], name=name, compiler_params=_params())(dy, xh, rstd, g)


def _sm_spec(tm, w):
    return pl.BlockSpec((None, tm, w), lambda k, i: (k, i, 0))


def _swiglu_fwd(name, hg, hu):
    _, s, w = hg.shape
    tm = _tr(s)

    def body(g_ref, u_ref, a_ref):
        g = g_ref[...]
        a_ref[...] = g * _sigmoid(g) * u_ref[...]

    return pl.pallas_call(body, grid=(4, s // tm), in_specs=[_sm_spec(tm, w)] * 2, out_specs=_sm_spec(tm, w),
                          out_shape=_sds(hg.shape), name=name, compiler_params=_params())(hg, hu)


def _swiglu_bwd(name, da, hg, hu):
    _, s, w = hg.shape
    tm = _tr(s)

    def body(da_ref, g_ref, u_ref, dg_ref, du_ref):
        g, da_v = g_ref[...], da_ref[...]
        sg = _sigmoid(g)
        du_ref[...] = da_v * g * sg
        dg_ref[...] = da_v * u_ref[...] * sg * (1.0 + g * (1.0 - sg))

    return pl.pallas_call(body, grid=(4, s // tm), in_specs=[_sm_spec(tm, w)] * 3, out_specs=[_sm_spec(tm, w)] * 2,
                          out_shape=[_sds(hg.shape)] * 2, name=name, compiler_params=_params())(da, hg, hu)


def _ple_fwd(name, x2, gp, bias, pp):
    s, d = x2.shape
    tm = _tr(s)

    def body(x_ref, gp_ref, b_ref, pp_ref, y_ref):
        y_ref[...] = x_ref[...] + _sigmoid(gp_ref[...] + b_ref[...]) * pp_ref[...]

    return pl.pallas_call(body, grid=(s // tm,), in_specs=[_rows(tm, d), _rows(tm, d), _vec(d), _rows(tm, d)],
                          out_specs=_rows(tm, d), out_shape=_sds((s, d)), name=name,
                          compiler_params=_params())(x2, gp, bias, pp)


def _ple_bwd(name, dy, gp, bias, pp):
    s, d = dy.shape
    tm = _tr(s)

    def body(dy_ref, gp_ref, b_ref, pp_ref, dgp_ref, dpp_ref, db_ref):
        @pl.when(pl.program_id(0) == 0)
        def _():
            db_ref[...] = jnp.zeros_like(db_ref)

        dyv = dy_ref[...]
        gate = _sigmoid(gp_ref[...] + b_ref[...])
        dgp = dyv * pp_ref[...] * gate * (1.0 - gate)
        dgp_ref[...] = dgp
        dpp_ref[...] = dyv * gate
        db_ref[...] += jnp.sum(dgp, axis=0, keepdims=True)

    return pl.pallas_call(body, grid=(s // tm,), in_specs=[_rows(tm, d), _rows(tm, d), _vec(d), _rows(tm, d)],
                          out_specs=[_rows(tm, d), _rows(tm, d), _vec(d)],
                          out_shape=[_sds((s, d)), _sds((s, d)), _sds((1, d))], name=name,
                          compiler_params=_params())(dy, gp, bias, pp)


def _loss_head(name, y, target):
    s, d = y.shape
    tm = _tr(s)

    def body(y_ref, t_ref, l_ref, dy_ref):
        @pl.when(pl.program_id(0) == 0)
        def _():
            l_ref[...] = jnp.zeros_like(l_ref)

        e = y_ref[...] - t_ref[...]
        dy_ref[...] = e * (1.0 / d)
        tot = jnp.sum(jnp.sum(e * e, axis=1, keepdims=True), axis=0, keepdims=True) * (0.5 / d)
        l_ref[...] += jnp.broadcast_to(tot, l_ref.shape)

    return pl.pallas_call(body, grid=(s // tm,), in_specs=[_rows(tm, d), _rows(tm, d)],
                          out_specs=[pl.BlockSpec((1, LANES), lambda i: (0, 0)), _rows(tm, d)],
                          out_shape=[_sds((1, LANES)), _sds((s, d))], name=name, compiler_params=_params())(y, target)


def _split_dot(x, m01):
    hi = x.astype(bf16)
    lo = (x - hi.astype(f32)).astype(bf16)
    return _dot(hi, m01) + _dot(lo, m01)


def _softplus(z):
    return jnp.maximum(z, 0.0) + jnp.log(1.0 + jnp.exp(-jnp.abs(z)))


def _sba_specs(s, dh, hp):
    qb_n = Q_BLOCK
    blk = pl.BlockSpec((hp, qb_n, dh), lambda hh, i: (hh, i, 0))
    full = pl.BlockSpec((hp, s, dh), lambda hh, i: (hh, 0, 0))
    col1 = pl.BlockSpec((hp, qb_n, 1), lambda hh, i: (hh, i, 0))
    return blk, full, col1


def _sba_fwd(q, k, v, shards):
    h, s, dh = q.shape
    hp, qb_n = SBA_FWD_HEADS, Q_BLOCK
    n = len(shards)
    steps = (h // hp, s // qb_n)

    def body(q_ref, k_ref, v_ref, *rest):
        o_ref, t_ref = rest[n:n + 2]
        local, over_ici, to_sibling = _gather_halves_copies(rest[:n], rest[n + 2:2 * n + 2], *rest[2 * n + 2:])
        i = pl.program_id(1)

        @pl.when(jnp.logical_and(pl.program_id(0) == 0, i == 0))
        def _():
            for cp in local + over_ici:
                cp.start()

        qbs = [q_ref[a].astype(bf16) for a in range(hp)]
        row = lax.broadcasted_iota(jnp.int32, (qb_n, qb_n), 0)
        col = lax.broadcasted_iota(jnp.int32, (qb_n, qb_n), 1)
        later = (row >= col).astype(bf16)

        def step(n, carry):
            tails, accs = carry
            off = pl.multiple_of((i - n) * qb_n, qb_n)
            mask = col < row + jnp.minimum(n, 1) * qb_n
            new_tails, new_accs = [], []
            for a in range(hp):
                kb = k_ref[a, pl.ds(off, qb_n), :].astype(bf16)
                vb = v_ref[a, pl.ds(off, qb_n), :].astype(bf16)
                z = _dot(qbs[a], kb, _NT) * SB_SCALE
                sp = _softplus(z)
                lk = jnp.where(mask, -sp, 0.0)
                cum = _split_dot(lk, later)
                w = jnp.where(mask, jnp.exp(z - sp + cum - lk + tails[a]), 0.0)
                new_tails.append(tails[a] + cum[:, 0:1])
                new_accs.append(accs[a] + _dot(w.astype(bf16), vb))
            return tuple(new_tails), tuple(new_accs)

        init = (tuple(jnp.zeros((qb_n, 1), f32) for _ in range(hp)), tuple(jnp.zeros((qb_n, dh), f32) for _ in range(hp)))
        tails, accs = lax.fori_loop(0, i + 1, step, init)
        for a in range(hp):
            o_ref[a] = accs[a]
            t_ref[a] = tails[a]

        @pl.when(jnp.logical_and(pl.program_id(0) == steps[0] - 1, i == steps[1] - 1))
        def _():
            for arrived, onward in zip(over_ici, to_sibling):
                arrived.wait_recv()
                onward.start()
            for cp in over_ici:
                cp.wait_send()
            for cp in to_sibling + local:
                cp.wait()

    blk, full, col1 = _sba_specs(s, dh, hp)
    sems = _comm_sems(n)
    outs = pl.pallas_call(body, grid=steps, in_specs=[blk, full, full] + [_ANY] * n,
                          out_specs=[blk, col1] + [_ANY] * n,
                          out_shape=[_sds((h, s, dh)), _sds((h, s, 1))] + _gather_shapes(shards),
                          scratch_shapes=sems + sems[:2], name="sba_fwd", compiler_params=_params())(q, k, v, *shards)
    return outs[0], outs[1], outs[2:]


def _sba_bwd(q, k, v, tot, do, grads, layout):
    h, s, dh = q.shape
    hp, qb_n = SBA_BWD_HEADS, Q_BLOCK
    n_in = len(grads)
    scatter_shape = _scatter_shapes(grads, layout)
    n_out = len(scatter_shape)
    steps = (h // hp, s // qb_n)

    def body(q_ref, k_ref, v_ref, t_ref, do_ref, *rest):
        dq_ref, dk_ref, dv_ref = rest[n_in:n_in + 3]
        copies = _scatter_copies(rest[:n_in], rest[n_in + 3:n_in + 3 + n_out], layout, *rest[n_in + 3 + n_out:])
        i = pl.program_id(1)

        @pl.when(jnp.logical_and(pl.program_id(0) == 0, i == 0))
        def _():
            for cp in copies:
                cp.start()

        @pl.when(i == 0)
        def _():
            dk_ref[...] = jnp.zeros_like(dk_ref)
            dv_ref[...] = jnp.zeros_like(dv_ref)

        qbs = [q_ref[a].astype(bf16) for a in range(hp)]
        dobs = [do_ref[a].astype(bf16) for a in range(hp)]
        tots = [t_ref[a] for a in range(hp)]
        row = lax.broadcasted_iota(jnp.int32, (qb_n, qb_n), 0)
        col = lax.broadcasted_iota(jnp.int32, (qb_n, qb_n), 1)
        upto = (row <= col).astype(bf16)
        before = (row < col).astype(bf16)

        def step(j, carry):
            heads, eheads, dqs = carry
            off = pl.multiple_of(j * qb_n, qb_n)
            mask = col < row + jnp.minimum(i - j, 1) * qb_n
            new_heads, new_eheads, new_dqs = [], [], []
            for a in range(hp):
                kb = k_ref[a, pl.ds(off, qb_n), :].astype(bf16)
                vb = v_ref[a, pl.ds(off, qb_n), :].astype(bf16)
                z = _dot(qbs[a], kb, _NT) * SB_SCALE
                sp = _softplus(z)
                lk = jnp.where(mask, -sp, 0.0)
                pre = _split_dot(lk, upto)
                w = jnp.where(mask, jnp.exp(z - sp + (tots[a] - heads[a] - pre)), 0.0)
                e = _dot(dobs[a], vb, _NT) * w
                epre = eheads[a] + _split_dot(e, before)
                dz = jnp.where(mask, e * jnp.exp(-sp) - epre * jnp.exp(z - sp), 0.0) * SB_SCALE
                dzb = dz.astype(bf16)
                dk_ref[a, pl.ds(off, qb_n), :] += _dot(dzb, qbs[a], _TN)
                dv_ref[a, pl.ds(off, qb_n), :] += _dot(w.astype(bf16), dobs[a], _TN)
                new_heads.append(heads[a] + pre[:, qb_n - 1:qb_n])
                new_eheads.append(eheads[a] + jnp.sum(e, axis=1, keepdims=True))
                new_dqs.append(dqs[a] + _dot(dzb, kb))
            return tuple(new_heads), tuple(new_eheads), tuple(new_dqs)

        zeros = tuple(jnp.zeros((qb_n, 1), f32) for _ in range(hp))
        _, _, dqs = lax.fori_loop(0, i + 1, step, (zeros, zeros, tuple(jnp.zeros((qb_n, dh), f32) for _ in range(hp))))
        for a in range(hp):
            dq_ref[a] = dqs[a]

        @pl.when(jnp.logical_and(pl.program_id(0) == steps[0] - 1, i == steps[1] - 1))
        def _():
            for cp in copies:
                cp.wait()

    blk, full, col1 = _sba_specs(s, dh, hp)
    outs = pl.pallas_call(body, grid=steps, in_specs=[blk, full, full, col1, blk] + [_ANY] * n_in,
                          out_specs=[blk, full, full] + [_ANY] * n_out,
                          out_shape=[_sds((h, s, dh))] * 3 + scatter_shape, scratch_shapes=_comm_sems(n_in),
                          name="sba_bwd", compiler_params=_params())(q, k, v, tot, do, *grads)
    return outs[0], outs[1], outs[2], outs[3:]


def _pool_fwd(hsm, pool_w, pool_scale):
    _, s, wd = hsm.shape
    ch = min(256, s)

    def body(u_ref, w_ref, sc_ref, b_ref, pooled_ref, pad_ref):
        pad_ref[0:POOL_HALO, :] = jnp.zeros((POOL_HALO, wd), f32)
        pad_ref[POOL_HALO:POOL_HALO + s, :] = u_ref[...]
        for g, win in enumerate(POOL_WINDOWS):
            cols = slice(g * GROUP_DIM, (g + 1) * GROUP_DIM)
            wg = w_ref[g].astype(bf16)
            for r0 in range(0, s, ch):
                acc = pad_ref[POOL_HALO + r0:POOL_HALO + r0 + ch, cols]
                own = acc
                for dlt in range(1, win):
                    acc = acc + pad_ref[POOL_HALO + r0 - dlt:POOL_HALO + r0 - dlt + ch, cols]
                t = r0 + lax.broadcasted_iota(jnp.int32, (ch, 1), 0)
                cnt = jnp.minimum(t + 1, win).astype(f32)
                pooled = acc / cnt - own
                pooled_ref[r0:r0 + ch, cols] = pooled
                b_ref[r0:r0 + ch, cols] = _dot(pooled.astype(bf16), wg) * sc_ref[:, cols]

    return pl.pallas_call(
        body, grid=(1,),
        in_specs=[pl.BlockSpec((None, s, wd), lambda i: (3, 0, 0)), pl.BlockSpec(pool_w.shape, lambda i: (0, 0, 0)),
                  _vec(wd)],
        out_specs=[pl.BlockSpec((s, wd), lambda i: (0, 0))] * 2, out_shape=[_sds((s, wd))] * 2,
        scratch_shapes=[pltpu.VMEM((POOL_HALO + s, wd), f32)], name="pool_fwd",
        compiler_params=_params())(hsm, pool_w, pool_scale)


def _pool_bwd(dcat, pooled, pool_w, pool_scale):
    s, wd = pooled.shape
    ch = min(256, s)

    def body(db_ref, p_ref, w_ref, sc_ref, du_ref, dw_ref, dsc_ref, pad_ref):
        pad_ref[s:s + POOL_HALO, :] = jnp.zeros((POOL_HALO, wd), f32)
        for g, win in enumerate(POOL_WINDOWS):
            cols = slice(g * GROUP_DIM, (g + 1) * GROUP_DIM)
            wg = w_ref[g].astype(bf16)
            pooled_g = p_ref[:, cols].astype(bf16)
            db = db_ref[:, cols]
            dmixed = (db * sc_ref[:, cols]).astype(bf16)
            dsc_ref[:, cols] = jnp.sum(db * _dot(pooled_g, wg), axis=0, keepdims=True)
            dw_ref[g] = _dot(pooled_g, dmixed, _TN)
            dpooled = _dot(dmixed, wg, _NT)
            t = lax.broadcasted_iota(jnp.int32, (s, 1), 0)
            pad_ref[0:s, cols] = dpooled / jnp.minimum(t + 1, win).astype(f32)
            for r0 in range(0, s, ch):
                acc = pad_ref[r0:r0 + ch, cols]
                for dlt in range(1, win):
                    acc = acc + pad_ref[r0 + dlt:r0 + dlt + ch, cols]
                du_ref[r0:r0 + ch, cols] = acc - dpooled[r0:r0 + ch]

    return pl.pallas_call(
        body, grid=(1,),
        in_specs=[pl.BlockSpec((s, wd), lambda i: (0, 1)), pl.BlockSpec((s, wd), lambda i: (0, 0)),
                  pl.BlockSpec(pool_w.shape, lambda i: (0, 0, 0)), _vec(wd)],
        out_specs=[pl.BlockSpec((s, wd), lambda i: (0, 0)), pl.BlockSpec(pool_w.shape, lambda i: (0, 0, 0)), _vec(wd)],
        out_shape=[_sds((s, wd)), _sds(pool_w.shape), _sds((1, wd))],
        scratch_shapes=[pltpu.VMEM((s + POOL_HALO, wd), f32)], name="pool_bwd",
        compiler_params=_params())(dcat, pooled, pool_w, pool_scale)


def _conv_fwd(hsm, dw, ln_g, ln_b):
    _, s, wd = hsm.shape
    rows, halo = CONV_ROWS, CONV_HALO

    def body(a_ref, g_ref, dw_ref, lg_ref, lb_ref, out_ref, hc_ref, xh_ref, rs_ref, pad_ref):
        hc = a_ref[...] * _sigmoid(g_ref[...])
        hc_ref[...] = hc
        pad_ref[0:halo, :] = jnp.zeros((halo, wd), f32)
        pad_ref[halo:halo + s, :] = hc
        taps = dw_ref[...]

        def chunk(c, _):
            base = pl.multiple_of(c * rows, rows)
            win = pad_ref[pl.ds(base, rows + halo), :]
            y = jnp.zeros((rows, wd), f32)
            for k in range(CONV_TAPS):
                lo = halo - (CONV_TAPS - 1) + k
                y = y + taps[k:k + 1, :] * win[lo:lo + rows]
            mu = jnp.mean(y, axis=-1, keepdims=True)
            cen = y - mu
            rstd = lax.rsqrt(jnp.mean(cen * cen, axis=-1, keepdims=True) + LN_EPS)
            xh = cen * rstd
            n = xh * lg_ref[...] + lb_ref[...]
            out_ref[pl.ds(base, rows), :] = n * _sigmoid(n)
            xh_ref[pl.ds(base, rows), :] = xh
            rs_ref[pl.ds(base, rows), :] = rstd
            return 0

        lax.fori_loop(0, s // rows, chunk, 0)

    full = pl.BlockSpec((s, wd), lambda i: (0, 0))
    return pl.pallas_call(
        body, grid=(1,),
        in_specs=[pl.BlockSpec((None, s, wd), lambda i: (0, 0, 0)), pl.BlockSpec((None, s, wd), lambda i: (1, 0, 0)),
                  pl.BlockSpec(dw.shape, lambda i: (0, 0)), _vec(wd), _vec(wd)],
        out_specs=[full, full, full, pl.BlockSpec((s, 1), lambda i: (0, 0))],
        out_shape=[_sds((s, wd))] * 3 + [_sds((s, 1))],
        scratch_shapes=[pltpu.VMEM((halo + s, wd), f32)], name="conv_fwd",
        compiler_params=_params())(hsm, hsm, dw, ln_g, ln_b)


def _conv_bwd(dcat, hsm, hc, xh, rstd, dw, ln_g, ln_b):
    s, wd = hc.shape
    rows, halo = CONV_ROWS, CONV_HALO

    def body(dc_ref, a_ref, g_ref, hc_ref, xh_ref, rs_ref, dw_ref, lg_ref, lb_ref,
             da_ref, dg_ref, ddw_ref, dlg_ref, dlb_ref, hpad_ref, ypad_ref):
        hpad_ref[0:halo, :] = jnp.zeros((halo, wd), f32)
        hpad_ref[halo:halo + s, :] = hc_ref[...]
        ypad_ref[s:s + halo, :] = jnp.zeros((halo, wd), f32)
        ddw_ref[...] = jnp.zeros_like(ddw_ref)
        dlg_ref[...] = jnp.zeros_like(dlg_ref)
        dlb_ref[...] = jnp.zeros_like(dlb_ref)
        taps = dw_ref[...]

        def norm_bwd(c, _):
            base = pl.multiple_of(c * rows, rows)
            xhv = xh_ref[pl.ds(base, rows), :]
            n = xhv * lg_ref[...] + lb_ref[...]
            sn = _sigmoid(n)
            dn = dc_ref[pl.ds(base, rows), :] * sn * (1.0 + n * (1.0 - sn))
            dlg_ref[...] += jnp.sum(dn * xhv, axis=0, keepdims=True)
            dlb_ref[...] += jnp.sum(dn, axis=0, keepdims=True)
            ypad_ref[pl.ds(base, rows), :] = _ln_bwd_rows(dn, xhv, rs_ref[pl.ds(base, rows), :], lg_ref[...])
            return 0

        lax.fori_loop(0, s // rows, norm_bwd, 0)

        def conv_bwd(c, _):
            base = pl.multiple_of(c * rows, rows)
            ywin = ypad_ref[pl.ds(base, rows + halo), :]
            hwin = hpad_ref[pl.ds(base, rows + halo), :]
            dy = ywin[0:rows]
            dhc = jnp.zeros((rows, wd), f32)
            for k in range(CONV_TAPS):
                fwd = CONV_TAPS - 1 - k
                dhc = dhc + taps[k:k + 1, :] * ywin[fwd:fwd + rows]
                lo = halo - (CONV_TAPS - 1) + k
                ddw_ref[k:k + 1, :] += jnp.sum(dy * hwin[lo:lo + rows], axis=0, keepdims=True)
            sg = _sigmoid(g_ref[pl.ds(base, rows), :])
            da_ref[pl.ds(base, rows), :] = dhc * sg
            dg_ref[pl.ds(base, rows), :] = dhc * a_ref[pl.ds(base, rows), :] * sg * (1.0 - sg)
            return 0

        lax.fori_loop(0, s // rows, conv_bwd, 0)

    full = pl.BlockSpec((s, wd), lambda i: (0, 0))
    tap_spec = pl.BlockSpec(dw.shape, lambda i: (0, 0))
    return pl.pallas_call(
        body, grid=(1,),
        in_specs=[full, pl.BlockSpec((None, s, wd), lambda i: (0, 0, 0)), pl.BlockSpec((None, s, wd), lambda i: (1, 0, 0)),
                  full, full, pl.BlockSpec((s, 1), lambda i: (0, 0)), tap_spec, _vec(wd), _vec(wd)],
        out_specs=[full, full, tap_spec, _vec(wd), _vec(wd)],
        out_shape=[_sds((s, wd)), _sds((s, wd)), _sds(dw.shape), _sds((1, wd)), _sds((1, wd))],
        scratch_shapes=[pltpu.VMEM((halo + s, wd), f32), pltpu.VMEM((s + halo, wd), f32)], name="conv_bwd",
        compiler_params=_params())(dcat, hsm, hsm, hc, xh, rstd, dw, ln_g, ln_b)


_GELU_C = 0.7978845608028654
_GELU_A = 0.044715


def _gelu(x):
    return 0.5 * x * (1.0 + jnp.tanh(_GELU_C * (x + _GELU_A * x * x * x)))


def _gelu_grad(x):
    th = jnp.tanh(_GELU_C * (x + _GELU_A * x * x * x))
    return 0.5 * (1.0 + th) + 0.5 * x * (1.0 - th * th) * _GELU_C * (1.0 + 3.0 * _GELU_A * x * x)


def _causal_sg_w(w_ref, g):
    row = lax.broadcasted_iota(jnp.int32, (SG_CHUNK, SG_CHUNK), 0)
    col = lax.broadcasted_iota(jnp.int32, (SG_CHUNK, SG_CHUNK), 1)
    return jnp.where(col <= row, w_ref[g], 0.0).astype(bf16), col <= row


def _gmlp_fwd(hsm, ln_g, ln_b, sg_w, sg_bt):
    _, s, wd = hsm.shape
    ck = SG_CHUNK

    def body(zu_ref, zv_ref, lg_ref, lb_ref, w_ref, bt_ref, out_ref, xh_ref, rs_ref):
        u = _gelu(zu_ref[...])
        vg = _gelu(zv_ref[...])
        mu = jnp.mean(vg, axis=-1, keepdims=True)
        cen = vg - mu
        rstd = lax.rsqrt(jnp.mean(cen * cen, axis=-1, keepdims=True) + LN_EPS)
        xh = cen * rstd
        xh_ref[...] = xh
        rs_ref[...] = rstd
        vn = (xh * lg_ref[...] + lb_ref[...]).astype(bf16)
        for g in range(4):
            cols = slice(g * GROUP_DIM, (g + 1) * GROUP_DIM)
            wm, _ = _causal_sg_w(w_ref, g)
            sv = _dot(wm, vn[:, cols]) + bt_ref[:, g:g + 1]
            out_ref[:, cols] = u[:, cols] * sv

    rows_spec = pl.BlockSpec((ck, wd), lambda i: (i, 0))
    return pl.pallas_call(
        body, grid=(s // ck,),
        in_specs=[pl.BlockSpec((None, ck, wd), lambda i: (2, i, 0)), pl.BlockSpec((None, ck, wd), lambda i: (3, i, 0)),
                  _vec(wd), _vec(wd), pl.BlockSpec(sg_w.shape, lambda i: (0, 0, 0)),
                  pl.BlockSpec(sg_bt.shape, lambda i: (0, 0))],
        out_specs=[rows_spec, rows_spec, pl.BlockSpec((ck, 1), lambda i: (i, 0))],
        out_shape=[_sds((s, wd)), _sds((s, wd)), _sds((s, 1))], name="gmlp_fwd",
        compiler_params=_params())(hsm, hsm, ln_g, ln_b, sg_w, sg_bt)


def _gmlp_bwd(dcat, hsm, xh, rstd, ln_g, ln_b, sg_w, sg_bt):
    s, wd = xh.shape
    ck = SG_CHUNK

    def body(dd_ref, zu_ref, zv_ref, xh_ref, rs_ref, lg_ref, lb_ref, w_ref, bt_ref,
             dzu_ref, dzv_ref, dw_ref, dbb_ref, dlg_ref, dlb_ref):
        @pl.when(pl.program_id(0) == 0)
        def _():
            dw_ref[...] = jnp.zeros_like(dw_ref)
            dbb_ref[...] = jnp.zeros_like(dbb_ref)
            dlg_ref[...] = jnp.zeros_like(dlg_ref)
            dlb_ref[...] = jnp.zeros_like(dlb_ref)

        zu, zv, dd, xhv = zu_ref[...], zv_ref[...], dd_ref[...], xh_ref[...]
        u = _gelu(zu)
        vn = (xhv * lg_ref[...] + lb_ref[...]).astype(bf16)
        du_parts, dvn_parts = [], []
        for g in range(4):
            cols = slice(g * GROUP_DIM, (g + 1) * GROUP_DIM)
            wm, keep = _causal_sg_w(w_ref, g)
            sv = _dot(wm, vn[:, cols]) + bt_ref[:, g:g + 1]
            du_parts.append(dd[:, cols] * sv)
            dsv = dd[:, cols] * u[:, cols]
            dsvb = dsv.astype(bf16)
            dbb_ref[g] += jnp.broadcast_to(jnp.sum(dsv, axis=1, keepdims=True), (ck, GROUP_DIM))
            dw_ref[g] += jnp.where(keep, _dot(dsvb, vn[:, cols], _NT), 0.0)
            dvn_parts.append(_dot(wm, dsvb, _TN))
        du = jnp.concatenate(du_parts, axis=1)
        dvn = jnp.concatenate(dvn_parts, axis=1)
        dlg_ref[...] += jnp.sum(dvn * xhv, axis=0, keepdims=True)
        dlb_ref[...] += jnp.sum(dvn, axis=0, keepdims=True)
        dzv_ref[...] = _ln_bwd_rows(dvn, xhv, rs_ref[...], lg_ref[...]) * _gelu_grad(zv)
        dzu_ref[...] = du * _gelu_grad(zu)

    rows_spec = pl.BlockSpec((ck, wd), lambda i: (i, 0))
    wspec = pl.BlockSpec(sg_w.shape, lambda i: (0, 0, 0))
    return pl.pallas_call(
        body, grid=(s // ck,),
        in_specs=[pl.BlockSpec((ck, wd), lambda i: (i, 1)), pl.BlockSpec((None, ck, wd), lambda i: (2, i, 0)),
                  pl.BlockSpec((None, ck, wd), lambda i: (3, i, 0)), rows_spec, pl.BlockSpec((ck, 1), lambda i: (i, 0)),
                  _vec(wd), _vec(wd), wspec, pl.BlockSpec(sg_bt.shape, lambda i: (0, 0))],
        out_specs=[rows_spec, rows_spec, wspec, wspec, _vec(wd), _vec(wd)],
        out_shape=[_sds((s, wd)), _sds((s, wd)), _sds(sg_w.shape), _sds(sg_w.shape), _sds((1, wd)), _sds((1, wd))],
        name="gmlp_bwd", compiler_params=_params())(dcat, hsm, hsm, xh, rstd, ln_g, ln_b, sg_w, sg_bt)


def _to_heads(x2d):
    s = x2d.shape[0]
    return jnp.transpose(x2d.reshape(s, N_HEADS, HEAD_DIM), (1, 0, 2)).astype(bf16)


def _from_heads(x3d):
    s = x3d.shape[1]
    return jnp.transpose(x3d, (1, 0, 2)).reshape(s, N_HEADS * HEAD_DIM)


def _local_step(x, p, target, wts, late_shards, small):
    wts = dict(wts)
    saved = []
    for i in range(DEPTH):
        tag = f"l{i}"
        if i % 2 == 0:
            hsm = _mm_nn_col(tag + "_in", x, wts["even_w_in"], 0)
            q, k, v = _to_heads(hsm[0]), _to_heads(hsm[1]), _to_heads(hsm[2])
            att_heads, sba_tot, late = _sba_fwd(q, k, v, late_shards)
            wts.update(zip(_LATE, late))
            att = _from_heads(att_heads)
            pool_out, pooled = _pool_fwd(hsm, small["pool_w"], small["pool_scale"])
            cat = jnp.concatenate([att, pool_out], axis=1)
            mix = _mm_nn_row(tag + "_out", cat, wts["even_w_out"], 0)
            mixer_saved = (hsm, q, k, v, sba_tot, pooled, cat)
        else:
            hsm = _mm_nn_col(tag + "_in", x, wts["odd_w_in"], 0)
            conv_out, hc, cxh, crs = _conv_fwd(hsm, small["conv_dw"], small["conv_ln_g"], small["conv_ln_b"])
            sg_out, sxh, srs = _gmlp_fwd(hsm, small["sg_ln_g"], small["sg_ln_b"], small["sg_w"], small["sg_bt"])
            cat = jnp.concatenate([conv_out, sg_out], axis=1)
            mix = _mm_nn_row(tag + "_out", cat, wts["odd_w_out"], 0)
            mixer_saved = (hsm, hc, cxh, crs, sxh, srs, cat)
        x1, xh1, rs1 = _ln_fwd(tag + "_ln_mix", x, mix, small["ln_mix_g"][i:i + 1], small["ln_mix_b"][i:i + 1])
        hg = _mm_nn_col(tag + "_gate", x1, wts["ffn_w_gate"], i)
        hu = _mm_nn_col(tag + "_up", x1, wts["ffn_w_up"], i)
        act = _swiglu_fwd(tag + "_swiglu", hg, hu)
        ffn = _mm_nn_row(tag + "_down", act, wts["ffn_w_down"], i)
        x2, xh2, rs2 = _ln_fwd(tag + "_ln_ffn", x1, ffn, small["ln_ffn_g"][i:i + 1], small["ln_ffn_b"][i:i + 1])
        gp = _mm_nn_row(tag + "_ple_gate", x2, wts["ple_w_gate"], i)
        pp = _mm_nn_col(tag + "_ple_proj", p[i], wts["ple_w_proj"], i, natural=True)
        x3 = _ple_fwd(tag + "_ple", x2, gp, small["ple_b_gate"][i:i + 1], pp)
        saved.append((x, mixer_saved, x1, xh1, rs1, hg, hu, act, x2, xh2, rs2, gp, pp))
        x = x3

    loss_part, dx = _loss_head("loss_head", x, target)

    big = {n: [None] * wts[n].shape[1] for n in wts}
    received = {}
    sm = {}
    per_layer = {n: [None] * DEPTH for n in ("ln_mix_g", "ln_mix_b", "ln_ffn_g", "ln_ffn_b", "ple_b_gate")}
    for i in reversed(range(DEPTH)):
        tag = f"l{i}b"
        x0, mixer_saved, x1, xh1, rs1, hg, hu, act, x2, xh2, rs2, gp, pp = saved[i]
        dgp, dpp, per_layer["ple_b_gate"][i] = _ple_bwd(tag + "_ple", dx, gp, small["ple_b_gate"][i:i + 1], pp)
        big["ple_w_proj"][i] = _mm_tn_col(tag + "_dproj", p[i], dpp)
        big["ple_w_gate"][i] = _mm_tn_row(tag + "_dgate", x2, dgp)
        dx2 = _mm_nt_row(tag + "_dx2", dgp, wts["ple_w_gate"], i, natural=True, add=dx)
        dr2, per_layer["ln_ffn_g"][i], per_layer["ln_ffn_b"][i] = _ln_bwd(tag + "_ln_ffn", dx2, xh2, rs2,
                                                                           small["ln_ffn_g"][i:i + 1])
        dact = _mm_nt_row(tag + "_dact", dr2, wts["ffn_w_down"], i)
        big["ffn_w_down"][i] = _mm_tn_row(tag + "_ddown", act, dr2)
        dhg, dhu = _swiglu_bwd(tag + "_swiglu", dact, hg, hu)
        big["ffn_w_gate"][i] = _mm_tn_col(tag + "_dgatew", x1, dhg)
        big["ffn_w_up"][i] = _mm_tn_col(tag + "_dupw", x1, dhu)
        part = _mm_nt_col(tag + "_dx1a", dhg, wts["ffn_w_gate"], i, dr2, ALPHA)
        dx1 = _mm_nt_col(tag + "_dx1b", dhu, wts["ffn_w_up"], i, part, 1.0)
        dr1, per_layer["ln_mix_g"][i], per_layer["ln_mix_b"][i] = _ln_bwd(tag + "_ln_mix", dx1, xh1, rs1,
                                                                           small["ln_mix_g"][i:i + 1])
        if i % 2 == 0:
            hsm, q, k, v, sba_tot, pooled, cat = mixer_saved
            big["even_w_out"][0] = _mm_tn_row(tag + "_dout", cat, dr1)
            dcat = _mm_nt_row(tag + "_dcat", dr1, wts["even_w_out"], 0, natural=True)
            entries, layout = _scatter_plan(big, _LATE)
            dq, dk, dv, got = _sba_bwd(q, k, v, sba_tot, _to_heads(dcat[:, :N_HEADS * HEAD_DIM]), entries, layout)
            received.update(zip(_LATE, got))
            du, sm["pool_w"], sm["pool_scale"] = _pool_bwd(dcat, pooled, small["pool_w"], small["pool_scale"])
            dhsm = jnp.stack([_from_heads(dq), _from_heads(dk), _from_heads(dv), du])
            w_in = "even_w_in"
        else:
            hsm, hc, cxh, crs, sxh, srs, cat = mixer_saved
            big["odd_w_out"][0] = _mm_tn_row(tag + "_dout", cat, dr1)
            dcat = _mm_nt_row(tag + "_dcat", dr1, wts["odd_w_out"], 0, natural=True)
            da, dg, sm["conv_dw"], sm["conv_ln_g"], sm["conv_ln_b"] = _conv_bwd(
                dcat, hsm, hc, cxh, crs, small["conv_dw"], small["conv_ln_g"], small["conv_ln_b"])
            dzu, dzv, sm["sg_w"], dsgb, sm["sg_ln_g"], sm["sg_ln_b"] = _gmlp_bwd(
                dcat, hsm, sxh, srs, small["sg_ln_g"], small["sg_ln_b"], small["sg_w"], small["sg_bt"])
            sm["sg_b"] = dsgb[:, :, 0]
            dhsm = jnp.stack([da, dg, dzu, dzv])
            w_in = "odd_w_in"
        big[w_in][0] = _mm_tn_col(tag + "_din", x0, dhsm)
        dx = _mm_nt_col(tag + "_dx", dhsm, wts[w_in], 0, dr1, ALPHA)
    for n, parts in per_layer.items():
        sm[n] = jnp.concatenate(parts, axis=0)
    received.update(zip(_EARLY, _scatter_chips(*_scatter_plan(big, _EARLY))))
    return loss_part, dx, received, sm


def _scatter_plan(big, names):
    entries, layout = [], []
    for pi, n in enumerate(names):
        for li, g in enumerate(big[n]):
            entries.append(g)
            layout.append((pi, li))
    return entries, layout


def _place():
    x, y, c = lax.axis_index("x"), lax.axis_index("y"), lax.axis_index("c")
    return x, y, c, [(1 - x, y), (x, 1 - y), (1 - x, 1 - y)]


def _gather_chips(shards):
    n = len(shards)

    def body(*refs):
        copies = _gather_copies(refs[:n], refs[n:2 * n], *refs[2 * n:])
        for cp in copies:
            cp.start()
        for cp in copies:
            cp.wait()

    return pl.pallas_call(body, in_specs=[_ANY] * n, out_specs=[_ANY] * n, out_shape=_gather_shapes(shards),
                          scratch_shapes=_comm_sems(n), name="gather_chips")(*shards)


def _comm_sems(n):
    return [pltpu.SemaphoreType.DMA((n, 3)), pltpu.SemaphoreType.DMA((n, 3)), pltpu.SemaphoreType.DMA((n,))]


def _gather_shapes(shards):
    return [_sds((N_CHIPS,) + a.shape, a.dtype) for a in shards]


def _gather_copies(ins, outs, send, recv, loc):
    x, y, c, chips = _place()
    mine = 2 * x + y
    copies = []
    for t in range(len(ins)):
        copies.append(pltpu.make_async_copy(ins[t], outs[t].at[mine], loc.at[t]))
        for j, (cx, cy) in enumerate(chips):
            copies.append(pltpu.make_async_remote_copy(
                src_ref=ins[t], dst_ref=outs[t].at[mine], send_sem=send.at[t, j], recv_sem=recv.at[t, j],
                device_id=(cx, cy, c), device_id_type=_MESH))
    return copies


def _gather_halves_copies(ins, outs, send, recv, loc, send_on, recv_on):
    x, y, c, chips = _place()
    mine = 2 * x + y
    local, over_ici, onward = [], [], []
    for t in range(len(ins)):
        nl, r, _ = ins[t].shape
        half = r // 2
        rows = pl.ds(pl.multiple_of(c * half, 32), half)
        local.append(pltpu.make_async_copy(ins[t], outs[t].at[mine], loc.at[t]))
        for j, (cx, cy) in enumerate(chips):
            over_ici.append(pltpu.make_async_remote_copy(
                src_ref=ins[t].at[pl.ds(0, nl), rows], dst_ref=outs[t].at[mine, pl.ds(0, nl), rows],
                send_sem=send.at[t, j], recv_sem=recv.at[t, j], device_id=(cx, cy, c), device_id_type=_MESH))
            landed = outs[t].at[2 * cx + cy, pl.ds(0, nl), rows]
            onward.append(pltpu.make_async_remote_copy(
                src_ref=landed, dst_ref=landed, send_sem=send_on.at[t, j], recv_sem=recv_on.at[t, j],
                device_id=(x, y, 1 - c), device_id_type=_MESH))
    return local, over_ici, onward


def _scatter_shapes(grads, layout):
    shapes = {}
    for e, (pi, li) in enumerate(layout):
        r, cdim = grads[e].shape[1:]
        shapes[pi] = (N_CHIPS, max(li + 1, shapes.get(pi, (0, 0))[1]), r, cdim)
    return [_sds(shapes[pi], grads[0].dtype) for pi in range(len(shapes))]


def _scatter_copies(ins, outs, layout, send, recv, loc):
    x, y, c, chips = _place()
    mine = 2 * x + y
    copies = []
    for e, (pi, li) in enumerate(layout):
        copies.append(pltpu.make_async_copy(ins[e].at[mine], outs[pi].at[mine, li], loc.at[e]))
        for j, (cx, cy) in enumerate(chips):
            copies.append(pltpu.make_async_remote_copy(
                src_ref=ins[e].at[2 * cx + cy], dst_ref=outs[pi].at[mine, li], send_sem=send.at[e, j],
                recv_sem=recv.at[e, j], device_id=(cx, cy, c), device_id_type=_MESH))
    return copies


def _scatter_chips(grads, layout):
    n_in = len(grads)
    out_shape = _scatter_shapes(grads, layout)
    n_out = len(out_shape)

    def body(*refs):
        copies = _scatter_copies(refs[:n_in], refs[n_in:n_in + n_out], layout, *refs[n_in + n_out:])
        for cp in copies:
            cp.start()
        for cp in copies:
            cp.wait()

    return pl.pallas_call(body, in_specs=[_ANY] * n_in, out_specs=[_ANY] * n_out, out_shape=out_shape,
                          scratch_shapes=_comm_sems(n_in), name="scatter_chips")(*grads)


def _swap_cores(arrays):
    n = len(arrays)

    def body(*refs):
        ins, outs = refs[:n], refs[n:2 * n]
        send, recv = refs[2 * n:]
        x, y, c, _ = _place()
        started = []
        for t in range(n):
            rc = pltpu.make_async_remote_copy(src_ref=ins[t], dst_ref=outs[t], send_sem=send.at[t], recv_sem=recv.at[t],
                                              device_id=(x, y, 1 - c), device_id_type=_MESH)
            rc.start()
            started.append(rc)
        for rc in started:
            rc.wait()

    return pl.pallas_call(
        body, in_specs=[_ANY] * n, out_specs=[_ANY] * n, out_shape=[_sds(a.shape, a.dtype) for a in arrays],
        scratch_shapes=[pltpu.SemaphoreType.DMA((n,)), pltpu.SemaphoreType.DMA((n,))], name="swap_cores")(*arrays)


def _gather_all(block):
    def body(in_ref, out_ref, send, recv, loc):
        x, y, c, _ = _place()
        mine = 4 * x + 2 * y + c
        cp = pltpu.make_async_copy(in_ref, out_ref.at[mine], loc)
        cp.start()
        started = [cp]
        for m in range(1, N_DEV):
            fx, fy, fc = (m >> 2) & 1, (m >> 1) & 1, m & 1
            peer = (x + fx - 2 * x * fx, y + fy - 2 * y * fy, c + fc - 2 * c * fc)
            rc = pltpu.make_async_remote_copy(src_ref=in_ref, dst_ref=out_ref.at[mine], send_sem=send.at[m - 1],
                                              recv_sem=recv.at[m - 1], device_id=peer, device_id_type=_MESH)
            rc.start()
            started.append(rc)
        for rc in started:
            rc.wait()

    return pl.pallas_call(
        body, in_specs=[_ANY], out_specs=_ANY, out_shape=_sds((N_DEV,) + block.shape),
        scratch_shapes=[pltpu.SemaphoreType.DMA((N_DEV - 1,)), pltpu.SemaphoreType.DMA((N_DEV - 1,)),
                        pltpu.SemaphoreType.DMA(())], name="gather_all")(block)


def _row_tile(r):
    for t in (256, 128, 64, 32, 16, 8):
        if r % t == 0:
            return t
    return r


def _sum_stack(name, stack):
    n, r, c = stack.shape
    tr = _row_tile(r)

    def body(s_ref, o_ref):
        acc = s_ref[0].astype(f32)
        for t in range(1, n):
            acc = acc + s_ref[t].astype(f32)
        o_ref[...] = acc

    return pl.pallas_call(body, grid=(r // tr,), in_specs=[pl.BlockSpec((n, tr, c), lambda i: (0, i, 0))],
                          out_specs=pl.BlockSpec((tr, c), lambda i: (i, 0)), out_shape=_sds((r, c)), name=name,
                          compiler_params=_params())(stack)


def _adamw(name, w, g_a, g_b, m, v):
    r, c = w.shape
    tr = _row_tile(r)
    two = g_b is not None
    bc1 = 1.0 - ADAM_B1 ** ADAM_STEP
    bc2 = 1.0 - ADAM_B2 ** ADAM_STEP

    def body(*refs):
        if two:
            w_ref, ga_ref, gb_ref, m_ref, v_ref, g_out, d_out, m_out, v_out = refs
            g = ga_ref[...] + gb_ref[...]
        else:
            w_ref, ga_ref, m_ref, v_ref, g_out, d_out, m_out, v_out = refs
            g = ga_ref[...]
        m_new = ADAM_B1 * m_ref[...] + (1.0 - ADAM_B1) * g
        v_new = ADAM_B2 * v_ref[...] + (1.0 - ADAM_B2) * (g * g)
        g_out[...] = g
        m_out[...] = m_new
        v_out[...] = v_new
        d_out[...] = -ADAM_LR * ((m_new / bc1) / (jnp.sqrt(v_new / bc2) + ADAM_EPS) + ADAM_WD * w_ref[...])

    spec = pl.BlockSpec((tr, c), lambda i: (i, 0))
    ins = [w, g_a] + ([g_b] if two else []) + [m, v]
    return pl.pallas_call(body, grid=(r // tr,), in_specs=[spec] * len(ins), out_specs=[spec] * 4,
                          out_shape=[_sds((r, c))] * 4, name=name, compiler_params=_params())(*ins)


_EARLY = ("even_w_in", "even_w_out")
_LATE = ("odd_w_in", "odd_w_out", "ffn_w_gate", "ffn_w_up", "ffn_w_down", "ple_w_proj", "ple_w_gate")
_BIG = _EARLY + _LATE
_SHARDED_SMALL = ("conv_dw", "conv_ln_g", "conv_ln_b", "sg_ln_g", "sg_ln_b")
_SMALL = ("pool_w", "pool_scale", "conv_dw", "conv_ln_g", "conv_ln_b", "sg_ln_g", "sg_ln_b", "sg_w", "sg_b",
          "ln_mix_g", "ln_mix_b", "ln_ffn_g", "ln_ffn_b", "ple_b_gate")
_WEIGHTS = ("even_w_in", "even_w_out", "pool_w", "pool_scale", "odd_w_in", "odd_w_out", "conv_dw", "conv_ln_g",
            "conv_ln_b", "sg_ln_g", "sg_ln_b", "sg_w", "sg_b", "ln_mix_g", "ln_mix_b", "ffn_w_gate", "ffn_w_up",
            "ffn_w_down", "ln_ffn_g", "ln_ffn_b", "ple_w_proj", "ple_w_gate", "ple_b_gate")


def _pack(arrays):
    flat = jnp.concatenate([a.reshape(-1) for a in arrays])
    pad = (-flat.shape[0]) % (256 * LANES)
    return jnp.pad(flat, (0, pad)).reshape(-1, LANES)


def _unpack(packed, shapes):
    flat = packed.reshape(-1)
    out, off = [], 0
    for shp in shapes:
        size = 1
        for dim in shp:
            size *= dim
        out.append(flat[off:off + size].reshape(shp))
        off += size
    return out


def _unshard_last(g4):
    return jnp.concatenate([g4[k] for k in range(N_CHIPS)], axis=-1)


def kernel(x, p, even_w_in, even_w_out, pool_w, pool_scale, odd_w_in, odd_w_out, conv_dw, conv_ln_g, conv_ln_b, sg_ln_g, sg_ln_b, sg_w, sg_b, ln_mix_g, ln_mix_b, ffn_w_gate, ffn_w_up, ffn_w_down, ln_ffn_g, ln_ffn_b, ple_w_proj, ple_w_gate, ple_b_gate, loss_target, m_even_w_in, m_even_w_out, m_pool_w, m_pool_scale, m_odd_w_in, m_odd_w_out, m_conv_dw, m_conv_ln_g, m_conv_ln_b, m_sg_ln_g, m_sg_ln_b, m_sg_w, m_sg_b, m_ln_mix_g, m_ln_mix_b, m_ffn_w_gate, m_ffn_w_up, m_ffn_w_down, m_ln_ffn_g, m_ln_ffn_b, m_ple_w_proj, m_ple_w_gate, m_ple_b_gate, v_even_w_in, v_even_w_out, v_pool_w, v_pool_scale, v_odd_w_in, v_odd_w_out, v_conv_dw, v_conv_ln_g, v_conv_ln_b, v_sg_ln_g, v_sg_ln_b, v_sg_w, v_sg_b, v_ln_mix_g, v_ln_mix_b, v_ffn_w_gate, v_ffn_w_up, v_ffn_w_down, v_ln_ffn_g, v_ln_ffn_b, v_ple_w_proj, v_ple_w_gate, v_ple_b_gate):
    w = dict(even_w_in=even_w_in, even_w_out=even_w_out, pool_w=pool_w, pool_scale=pool_scale, odd_w_in=odd_w_in,
             odd_w_out=odd_w_out, conv_dw=conv_dw, conv_ln_g=conv_ln_g, conv_ln_b=conv_ln_b, sg_ln_g=sg_ln_g,
             sg_ln_b=sg_ln_b, sg_w=sg_w, sg_b=sg_b, ln_mix_g=ln_mix_g, ln_mix_b=ln_mix_b, ffn_w_gate=ffn_w_gate,
             ffn_w_up=ffn_w_up, ffn_w_down=ffn_w_down, ln_ffn_g=ln_ffn_g, ln_ffn_b=ln_ffn_b, ple_w_proj=ple_w_proj,
             ple_w_gate=ple_w_gate, ple_b_gate=ple_b_gate)
    mom = dict(even_w_in=m_even_w_in, even_w_out=m_even_w_out, pool_w=m_pool_w, pool_scale=m_pool_scale,
               odd_w_in=m_odd_w_in, odd_w_out=m_odd_w_out, conv_dw=m_conv_dw, conv_ln_g=m_conv_ln_g,
               conv_ln_b=m_conv_ln_b, sg_ln_g=m_sg_ln_g, sg_ln_b=m_sg_ln_b, sg_w=m_sg_w, sg_b=m_sg_b,
               ln_mix_g=m_ln_mix_g, ln_mix_b=m_ln_mix_b, ffn_w_gate=m_ffn_w_gate, ffn_w_up=m_ffn_w_up,
               ffn_w_down=m_ffn_w_down, ln_ffn_g=m_ln_ffn_g, ln_ffn_b=m_ln_ffn_b, ple_w_proj=m_ple_w_proj,
               ple_w_gate=m_ple_w_gate, ple_b_gate=m_ple_b_gate)
    var = dict(even_w_in=v_even_w_in, even_w_out=v_even_w_out, pool_w=v_pool_w, pool_scale=v_pool_scale,
               odd_w_in=v_odd_w_in, odd_w_out=v_odd_w_out, conv_dw=v_conv_dw, conv_ln_g=v_conv_ln_g,
               conv_ln_b=v_conv_ln_b, sg_ln_g=v_sg_ln_g, sg_ln_b=v_sg_ln_b, sg_w=v_sg_w, sg_b=v_sg_b,
               ln_mix_g=v_ln_mix_g, ln_mix_b=v_ln_mix_b, ffn_w_gate=v_ffn_w_gate, ffn_w_up=v_ffn_w_up,
               ffn_w_down=v_ffn_w_down, ln_ffn_g=v_ln_ffn_g, ln_ffn_b=v_ln_ffn_b, ple_w_proj=v_ple_w_proj,
               ple_w_gate=v_ple_w_gate, ple_b_gate=v_ple_b_gate)

    gathered = _gather_chips([w[n].astype(bf16) for n in _EARLY] + [w[n] for n in _SHARDED_SMALL])
    wts = dict(zip(_EARLY, gathered[:len(_EARLY)]))
    small = {n: w[n][0] for n in ("pool_w", "sg_w")}
    small.update({n: w[n] for n in ("pool_scale", "ln_mix_g", "ln_mix_b", "ln_ffn_g", "ln_ffn_b", "ple_b_gate")})
    small["sg_bt"] = jnp.transpose(w["sg_b"][0])
    for n, g4 in zip(_SHARDED_SMALL, gathered[len(_EARLY):]):
        small[n] = _unshard_last(g4)[0]
        if n != "conv_dw":
            small[n] = small[n][None]

    loss_part, grad_x, received, sm = _local_step(x[0], p[:, 0], loss_target[0], wts,
                                                  [w[n].astype(bf16) for n in _LATE], small)

    chip_sums = []
    for n in _BIG:
        _, nl, r, c = received[n].shape
        chip_sums.append(_sum_stack("sum_" + n, received[n].reshape(N_CHIPS, nl * r, c)))
    other = _swap_cores(chip_sums)
    results = {}
    for n, mine, theirs in zip(_BIG, chip_sums, other):
        shp = w[n].shape
        flat = (shp[0] * shp[1], shp[2])
        outs = _adamw("adamw_" + n, w[n].reshape(flat), mine, theirs, mom[n].reshape(flat), var[n].reshape(flat))
        results[n] = [o.reshape(shp) for o in outs]

    sm_shapes = [(1,) + sm[n].shape if n in ("pool_w", "sg_w", "conv_dw", "sg_b") else sm[n].shape for n in _SMALL]
    packed = _pack([sm[n] for n in _SMALL] + [loss_part[0, 0:1]])
    total = _sum_stack("sum_small", _gather_all(packed))
    parts = _unpack(total, sm_shapes + [(1,)])
    loss = parts[-1][0]
    chip = 2 * lax.axis_index("x") + lax.axis_index("y")
    g_small = {}
    for n, g in zip(_SMALL, parts[:-1]):
        if n in _SHARDED_SMALL:
            width = w[n].shape[-1]
            g = lax.dynamic_slice_in_dim(g, chip * width, width, axis=g.ndim - 1)
        g_small[n] = g
    shapes = [w[n].shape for n in _SMALL]
    outs = _adamw("adamw_small", _pack([w[n] for n in _SMALL]), _pack([g_small[n] for n in _SMALL]), None,
                  _pack([mom[n] for n in _SMALL]), _pack([var[n] for n in _SMALL]))
    unpacked = [_unpack(o, shapes) for o in outs]
    for idx, n in enumerate(_SMALL):
        results[n] = [u[idx] for u in unpacked]

    return (loss, grad_x[None], *[results[n][0] for n in _WEIGHTS], *[results[n][1] for n in _WEIGHTS],
            *[results[n][2] for n in _WEIGHTS], *[results[n][3] for n in _WEIGHTS])
```

```python
import jax
import jax.numpy as jnp
from jax import lax
from jax.experimental import pallas as pl
from jax.experimental.pallas import tpu as pltpu

f32 = jnp.float32
bf16 = jnp.bfloat16

D_MODEL = 1024
N_HEADS = 8
HEAD_DIM = 64
Q_BLOCK = 128
POOL_WINDOWS = (2, 4, 8, 16)
GROUP_DIM = 128
CONV_TAPS = 31
SG_CHUNK = 128
DEPTH = 2
ALPHA = (2 * DEPTH) ** 0.25
LN_EPS = 1e-5
SB_SCALE = HEAD_DIM ** -0.5
ADAM_LR, ADAM_B1, ADAM_B2, ADAM_EPS, ADAM_WD, ADAM_STEP = 0.001, 0.9, 0.999, 1e-08, 0.01, 10
N_CHIPS = 4
N_DEV = 8
LANES = 128
VMEM_LIMIT = 56 * 1024 * 1024
TM = 512
TR = 256
SBA_FWD_HEADS = 8
SBA_BWD_HEADS = 8
CONV_ROWS = 64
CONV_HALO = 32
POOL_HALO = 16

_NN = (((1,), (0,)), ((), ()))
_NT = (((1,), (1,)), ((), ()))
_TN = (((0,), (0,)), ((), ()))
_ANY = pl.BlockSpec(memory_space=pl.ANY)
_MESH = pl.DeviceIdType.MESH


def _params():
    return pltpu.CompilerParams(vmem_limit_bytes=VMEM_LIMIT)


def _sds(shape, dtype=f32):
    return jax.ShapeDtypeStruct(tuple(shape), dtype)


def _dot(a, b, dims=_NN):
    return lax.dot_general(a, b, dims, preferred_element_type=f32)


def _sigmoid(x):
    return 1.0 / (1.0 + jnp.exp(-x))


def _mm(name, a, b, grid, a_spec, b_spec, out_shape, out_spec, dims, reduce=False, add=None, add_spec=None,
        add_scale=1.0, out_dtype=f32):
    has_add = add is not None
    k_axis = len(grid) - 1

    def body(*refs):
        if has_add:
            a_ref, b_ref, add_ref, o_ref = refs
        else:
            a_ref, b_ref, o_ref = refs
        r = _dot(a_ref[...].astype(bf16), b_ref[...].astype(bf16), dims)
        if reduce:
            k = pl.program_id(k_axis)

            @pl.when(k == 0)
            def _():
                o_ref[...] = r + add_scale * add_ref[...] if has_add else r

            @pl.when(k > 0)
            def _():
                o_ref[...] += r
        else:
            o_ref[...] = (r + add_scale * add_ref[...] if has_add else r).astype(out_dtype)

    ins = [a, b] + ([add] if has_add else [])
    specs = [a_spec, b_spec] + ([add_spec] if has_add else [])
    return pl.pallas_call(body, grid=grid, in_specs=specs, out_specs=out_spec, out_shape=_sds(out_shape, out_dtype),
                          name=name, compiler_params=_params())(*ins)


def _tm(s):
    return min(TM, s)


def _act_spec(a, tm, width):
    if a.ndim == 3:
        return pl.BlockSpec((None, tm, width), lambda i, k: (k, i, 0))
    return pl.BlockSpec((tm, width), lambda i, k: (i, k))


def _mm_nn_col(name, x, w4, layer, natural=False):
    s, kk = x.shape
    nq = w4.shape[3]
    tm = _tm(s)
    if natural:
        out_shape, out_spec = (s, 4 * nq), pl.BlockSpec((tm, nq), lambda i, k: (i, k))
    else:
        out_shape, out_spec = (4, s, nq), pl.BlockSpec((None, tm, nq), lambda i, k: (k, i, 0))
    return _mm(name, x, w4, (s // tm, 4), pl.BlockSpec((tm, kk), lambda i, k: (i, 0)),
               pl.BlockSpec((None, None, kk, nq), lambda i, k: (k, layer, 0, 0)), out_shape, out_spec, _NN)


def _mm_nn_row(name, a, w4, layer):
    s = a.shape[-2]
    kq, n = w4.shape[2], w4.shape[3]
    tm = _tm(s)
    return _mm(name, a, w4, (s // tm, 4), _act_spec(a, tm, kq),
               pl.BlockSpec((None, None, kq, n), lambda i, k: (k, layer, 0, 0)), (s, n),
               pl.BlockSpec((tm, n), lambda i, k: (i, 0)), _NN, reduce=True)


def _mm_nt_col(name, dh, w4, layer, add, add_scale):
    s = dh.shape[-2]
    kk, nq = w4.shape[2], w4.shape[3]
    tm = _tm(s)
    row = pl.BlockSpec((tm, kk), lambda i, k: (i, 0))
    return _mm(name, dh, w4, (s // tm, 4), _act_spec(dh, tm, nq),
               pl.BlockSpec((None, None, kk, nq), lambda i, k: (k, layer, 0, 0)), (s, kk), row, _NT,
               reduce=True, add=add, add_spec=row, add_scale=add_scale)


def _mm_nt_row(name, dy, w4, layer, natural=False, add=None):
    s, n = dy.shape
    kq = w4.shape[2]
    tm = _tm(s)
    if natural:
        out_shape, out_spec = (s, 4 * kq), pl.BlockSpec((tm, kq), lambda i, k: (i, k))
    else:
        out_shape, out_spec = (4, s, kq), pl.BlockSpec((None, tm, kq), lambda i, k: (k, i, 0))
    return _mm(name, dy, w4, (s // tm, 4), pl.BlockSpec((tm, n), lambda i, k: (i, 0)),
               pl.BlockSpec((None, None, kq, n), lambda i, k: (k, layer, 0, 0)), out_shape, out_spec, _NT,
               add=add, add_spec=out_spec if add is not None else None)


def _mm_tn_col(name, x, dh):
    s, kk = x.shape
    if dh.ndim == 3:
        nq = dh.shape[2]
        b_spec = pl.BlockSpec((None, s, nq), lambda k, j: (k, 0, 0))
    else:
        nq = dh.shape[1] // 4
        b_spec = pl.BlockSpec((s, nq), lambda k, j: (0, k))
    tk = min(256, kk)
    return _mm(name, x, dh, (4, kk // tk), pl.BlockSpec((s, tk), lambda k, j: (0, j)), b_spec, (4, kk, nq),
               pl.BlockSpec((None, tk, nq), lambda k, j: (k, j, 0)), _TN, out_dtype=bf16)


def _mm_tn_row(name, a, dy):
    s, n = dy.shape
    if a.ndim == 3:
        kq = a.shape[2]
        a_spec = pl.BlockSpec((None, s, kq), lambda k, j: (k, 0, 0))
    else:
        kq = a.shape[1] // 4
        a_spec = pl.BlockSpec((s, kq), lambda k, j: (0, k))
    tn = min(512, n)
    return _mm(name, a, dy, (4, n // tn), a_spec, pl.BlockSpec((s, tn), lambda k, j: (0, j)), (4, kq, n),
               pl.BlockSpec((None, kq, tn), lambda k, j: (k, 0, j)), _TN, out_dtype=bf16)


def _tr(s):
    return min(TR, s)


def _rows(tm, d):
    return pl.BlockSpec((tm, d), lambda i: (i, 0))


def _vec(d):
    return pl.BlockSpec((1, d), lambda i: (0, 0))


def _ln_fwd(name, x, mix, g, b):
    s, d = x.shape
    tm = _tr(s)

    def body(x_ref, m_ref, g_ref, b_ref, y_ref, xh_ref, rs_ref):
        r = ALPHA * x_ref[...] + m_ref[...]
        mu = jnp.mean(r, axis=-1, keepdims=True)
        c = r - mu
        rstd = lax.rsqrt(jnp.mean(c * c, axis=-1, keepdims=True) + LN_EPS)
        xh = c * rstd
        y_ref[...] = xh * g_ref[...] + b_ref[...]
        xh_ref[...] = xh
        rs_ref[...] = rstd

    return pl.pallas_call(
        body, grid=(s // tm,), in_specs=[_rows(tm, d), _rows(tm, d), _vec(d), _vec(d)],
        out_specs=[_rows(tm, d), _rows(tm, d), _rows(tm, 1)],
        out_shape=[_sds((s, d)), _sds((s, d)), _sds((s, 1))], name=name, compiler_params=_params())(x, mix, g, b)


def _ln_bwd_rows(dy, xh, rstd, g):
    dxh = dy * g
    m1 = jnp.mean(dxh, axis=-1, keepdims=True)
    m2 = jnp.mean(dxh * xh, axis=-1, keepdims=True)
    return rstd * (dxh - m1 - xh * m2)


def _ln_bwd(name, dy, xh, rstd, g):
    s, d = dy.shape
    tm = _tr(s)

    def body(dy_ref, xh_ref, rs_ref, g_ref, dr_ref, dg_ref, db_ref):
        @pl.when(pl.program_id(0) == 0)
        def _():
            dg_ref[...] = jnp.zeros_like(dg_ref)
            db_ref[...] = jnp.zeros_like(db_ref)

        dyv, xhv = dy_ref[...], xh_ref[...]
        dr_ref[...] = _ln_bwd_rows(dyv, xhv, rs_ref[...], g_ref[...])
        dg_ref[...] += jnp.sum(dyv * xhv, axis=0, keepdims=True)
        db_ref[...] += jnp.sum(dyv, axis=0, keepdims=True)

    return pl.pallas_call(
        body, grid=(s // tm,), in_specs=[_rows(tm, d), _rows(tm, d), _rows(tm, 1), _vec(d)],
        out_specs=[_rows(tm, d), _vec(d), _vec(d)],
        out_shape=[_sds((s, d)), _sds((1, d)), _sds((1, d))], name=name, compiler_params=_params())(dy, xh, rstd, g)


def _sm_spec(tm, w):
    return pl.BlockSpec((None, tm, w), lambda k, i: (k, i, 0))


def _swiglu_fwd(name, hg, hu):
    _, s, w = hg.shape
    tm = _tr(s)

    def body(g_ref, u_ref, a_ref):
        g = g_ref[...]
        a_ref[...] = (g * _sigmoid(g) * u_ref[...]).astype(bf16)

    return pl.pallas_call(body, grid=(4, s // tm), in_specs=[_sm_spec(tm, w)] * 2, out_specs=_sm_spec(tm, w),
                          out_shape=_sds(hg.shape, bf16), name=name, compiler_params=_params())(hg, hu)


def _swiglu_bwd(name, da, hg, hu):
    _, s, w = hg.shape
    tm = _tr(s)

    def body(da_ref, g_ref, u_ref, dg_ref, du_ref):
        g, da_v = g_ref[...], da_ref[...]
        sg = _sigmoid(g)
        du_ref[...] = (da_v * g * sg).astype(bf16)
        dg_ref[...] = (da_v * u_ref[...] * sg * (1.0 + g * (1.0 - sg))).astype(bf16)

    return pl.pallas_call(body, grid=(4, s // tm), in_specs=[_sm_spec(tm, w)] * 3, out_specs=[_sm_spec(tm, w)] * 2,
                          out_shape=[_sds(hg.shape, bf16)] * 2, name=name, compiler_params=_params())(da, hg, hu)


def _ple_fwd(name, x2, gp, bias, pp):
    s, d = x2.shape
    tm = _tr(s)

    def body(x_ref, gp_ref, b_ref, pp_ref, y_ref):
        y_ref[...] = x_ref[...] + _sigmoid(gp_ref[...] + b_ref[...]) * pp_ref[...]

    return pl.pallas_call(body, grid=(s // tm,), in_specs=[_rows(tm, d), _rows(tm, d), _vec(d), _rows(tm, d)],
                          out_specs=_rows(tm, d), out_shape=_sds((s, d)), name=name,
                          compiler_params=_params())(x2, gp, bias, pp)


def _ple_bwd(name, dy, gp, bias, pp):
    s, d = dy.shape
    tm = _tr(s)

    def body(dy_ref, gp_ref, b_ref, pp_ref, dgp_ref, dpp_ref, db_ref):
        @pl.when(pl.program_id(0) == 0)
        def _():
            db_ref[...] = jnp.zeros_like(db_ref)

        dyv = dy_ref[...]
        gate = _sigmoid(gp_ref[...] + b_ref[...])
        dgp = dyv * pp_ref[...] * gate * (1.0 - gate)
        dgp_ref[...] = dgp.astype(bf16)
        dpp_ref[...] = (dyv * gate).astype(bf16)
        db_ref[...] += jnp.sum(dgp, axis=0, keepdims=True)

    return pl.pallas_call(body, grid=(s // tm,), in_specs=[_rows(tm, d), _rows(tm, d), _vec(d), _rows(tm, d)],
                          out_specs=[_rows(tm, d), _rows(tm, d), _vec(d)],
                          out_shape=[_sds((s, d), bf16), _sds((s, d), bf16), _sds((1, d))], name=name,
                          compiler_params=_params())(dy, gp, bias, pp)


def _loss_head(name, y, target):
    s, d = y.shape
    tm = _tr(s)

    def body(y_ref, t_ref, l_ref, dy_ref):
        @pl.when(pl.program_id(0) == 0)
        def _():
            l_ref[...] = jnp.zeros_like(l_ref)

        e = y_ref[...] - t_ref[...]
        dy_ref[...] = e * (1.0 / d)
        tot = jnp.sum(jnp.sum(e * e, axis=1, keepdims=True), axis=0, keepdims=True) * (0.5 / d)
        l_ref[...] += jnp.broadcast_to(tot, l_ref.shape)

    return pl.pallas_call(body, grid=(s // tm,), in_specs=[_rows(tm, d), _rows(tm, d)],
                          out_specs=[pl.BlockSpec((1, LANES), lambda i: (0, 0)), _rows(tm, d)],
                          out_shape=[_sds((1, LANES)), _sds((s, d))], name=name, compiler_params=_params())(y, target)


def _split_dot(x, m01):
    hi = x.astype(bf16)
    lo = (x - hi.astype(f32)).astype(bf16)
    return _dot(hi, m01) + _dot(lo, m01)


def _softplus(z):
    return jnp.maximum(z, 0.0) + jnp.log(1.0 + jnp.exp(-jnp.abs(z)))


def _sba_specs(s, dh, hp):
    qb_n = Q_BLOCK
    blk = pl.BlockSpec((hp, qb_n, dh), lambda hh, i: (hh, i, 0))
    full = pl.BlockSpec((hp, s, dh), lambda hh, i: (hh, 0, 0))
    col1 = pl.BlockSpec((hp, qb_n, 1), lambda hh, i: (hh, i, 0))
    return blk, full, col1


def _sba_fwd(q, k, v, shards):
    h, s, dh = q.shape
    hp, qb_n = SBA_FWD_HEADS, Q_BLOCK
    n = len(shards)
    steps = (h // hp, s // qb_n)

    def body(q_ref, k_ref, v_ref, *rest):
        o_ref, t_ref = rest[n:n + 2]
        local, over_ici, to_sibling = _gather_halves_copies(rest[:n], rest[n + 2:2 * n + 2], *rest[2 * n + 2:])
        i = pl.program_id(1)

        @pl.when(jnp.logical_and(pl.program_id(0) == 0, i == 0))
        def _():
            for cp in local + over_ici:
                cp.start()

        qbs = [q_ref[a].astype(bf16) for a in range(hp)]
        row = lax.broadcasted_iota(jnp.int32, (qb_n, qb_n), 0)
        col = lax.broadcasted_iota(jnp.int32, (qb_n, qb_n), 1)
        later = (row >= col).astype(bf16)

        def step(n, carry):
            tails, accs = carry
            off = pl.multiple_of((i - n) * qb_n, qb_n)
            mask = col < row + jnp.minimum(n, 1) * qb_n
            new_tails, new_accs = [], []
            for a in range(hp):
                kb = k_ref[a, pl.ds(off, qb_n), :].astype(bf16)
                vb = v_ref[a, pl.ds(off, qb_n), :].astype(bf16)
                z = _dot(qbs[a], kb, _NT) * SB_SCALE
                sp = _softplus(z)
                lk = jnp.where(mask, -sp, 0.0)
                cum = _split_dot(lk, later)
                w = jnp.where(mask, jnp.exp(z - sp + cum - lk + tails[a]), 0.0)
                new_tails.append(tails[a] + cum[:, 0:1])
                new_accs.append(accs[a] + _dot(w.astype(bf16), vb))
            return tuple(new_tails), tuple(new_accs)

        init = (tuple(jnp.zeros((qb_n, 1), f32) for _ in range(hp)), tuple(jnp.zeros((qb_n, dh), f32) for _ in range(hp)))
        tails, accs = lax.fori_loop(0, i + 1, step, init)
        for a in range(hp):
            o_ref[a] = accs[a]
            t_ref[a] = tails[a]

        @pl.when(jnp.logical_and(pl.program_id(0) == steps[0] - 1, i == steps[1] - 1))
        def _():
            for arrived, onward in zip(over_ici, to_sibling):
                arrived.wait_recv()
                onward.start()
            for cp in over_ici:
                cp.wait_send()
            for cp in to_sibling + local:
                cp.wait()

    blk, full, col1 = _sba_specs(s, dh, hp)
    sems = _comm_sems(n)
    outs = pl.pallas_call(body, grid=steps, in_specs=[blk, full, full] + [_ANY] * n,
                          out_specs=[blk, col1] + [_ANY] * n,
                          out_shape=[_sds((h, s, dh)), _sds((h, s, 1))] + _gather_shapes(shards),
                          scratch_shapes=sems + sems[:2], name="sba_fwd", compiler_params=_params())(q, k, v, *shards)
    return outs[0], outs[1], outs[2:]


def _sba_bwd(q, k, v, tot, do, grads, layout):
    h, s, dh = q.shape
    hp, qb_n = SBA_BWD_HEADS, Q_BLOCK
    n_in = len(grads)
    scatter_shape = _scatter_shapes(grads, layout)
    n_out = len(scatter_shape)
    steps = (h // hp, s // qb_n)

    def body(q_ref, k_ref, v_ref, t_ref, do_ref, *rest):
        dq_ref, dk_ref, dv_ref = rest[n_in:n_in + 3]
        copies = _scatter_copies(rest[:n_in], rest[n_in + 3:n_in + 3 + n_out], layout, *rest[n_in + 3 + n_out:])
        i = pl.program_id(1)

        @pl.when(jnp.logical_and(pl.program_id(0) == 0, i == 0))
        def _():
            for cp in copies:
                cp.start()

        @pl.when(i == 0)
        def _():
            dk_ref[...] = jnp.zeros_like(dk_ref)
            dv_ref[...] = jnp.zeros_like(dv_ref)

        qbs = [q_ref[a].astype(bf16) for a in range(hp)]
        dobs = [do_ref[a].astype(bf16) for a in range(hp)]
        tots = [t_ref[a] for a in range(hp)]
        row = lax.broadcasted_iota(jnp.int32, (qb_n, qb_n), 0)
        col = lax.broadcasted_iota(jnp.int32, (qb_n, qb_n), 1)
        upto = (row <= col).astype(bf16)
        before = (row < col).astype(bf16)

        def step(j, carry):
            heads, eheads, dqs = carry
            off = pl.multiple_of(j * qb_n, qb_n)
            mask = col < row + jnp.minimum(i - j, 1) * qb_n
            new_heads, new_eheads, new_dqs = [], [], []
            for a in range(hp):
                kb = k_ref[a, pl.ds(off, qb_n), :].astype(bf16)
                vb = v_ref[a, pl.ds(off, qb_n), :].astype(bf16)
                z = _dot(qbs[a], kb, _NT) * SB_SCALE
                sp = _softplus(z)
                lk = jnp.where(mask, -sp, 0.0)
                pre = _split_dot(lk, upto)
                w = jnp.where(mask, jnp.exp(z - sp + (tots[a] - heads[a] - pre)), 0.0)
                e = _dot(dobs[a], vb, _NT) * w
                epre = eheads[a] + _split_dot(e, before)
                dz = jnp.where(mask, e * jnp.exp(-sp) - epre * jnp.exp(z - sp), 0.0) * SB_SCALE
                dzb = dz.astype(bf16)
                dk_ref[a, pl.ds(off, qb_n), :] += _dot(dzb, qbs[a], _TN)
                dv_ref[a, pl.ds(off, qb_n), :] += _dot(w.astype(bf16), dobs[a], _TN)
                new_heads.append(heads[a] + pre[:, qb_n - 1:qb_n])
                new_eheads.append(eheads[a] + jnp.sum(e, axis=1, keepdims=True))
                new_dqs.append(dqs[a] + _dot(dzb, kb))
            return tuple(new_heads), tuple(new_eheads), tuple(new_dqs)

        zeros = tuple(jnp.zeros((qb_n, 1), f32) for _ in range(hp))
        _, _, dqs = lax.fori_loop(0, i + 1, step, (zeros, zeros, tuple(jnp.zeros((qb_n, dh), f32) for _ in range(hp))))
        for a in range(hp):
            dq_ref[a] = dqs[a]

        @pl.when(jnp.logical_and(pl.program_id(0) == steps[0] - 1, i == steps[1] - 1))
        def _():
            for cp in copies:
                cp.wait()

    blk, full, col1 = _sba_specs(s, dh, hp)
    outs = pl.pallas_call(body, grid=steps, in_specs=[blk, full, full, col1, blk] + [_ANY] * n_in,
                          out_specs=[blk, full, full] + [_ANY] * n_out,
                          out_shape=[_sds((h, s, dh))] * 3 + scatter_shape, scratch_shapes=_comm_sems(n_in),
                          name="sba_bwd", compiler_params=_params())(q, k, v, tot, do, *grads)
    return outs[0], outs[1], outs[2], outs[3:]


def _pool_fwd(hsm, pool_w, pool_scale):
    _, s, wd = hsm.shape
    ch = min(256, s)

    def body(u_ref, w_ref, sc_ref, b_ref, pooled_ref, pad_ref):
        pad_ref[0:POOL_HALO, :] = jnp.zeros((POOL_HALO, wd), f32)
        pad_ref[POOL_HALO:POOL_HALO + s, :] = u_ref[...]
        for g, win in enumerate(POOL_WINDOWS):
            cols = slice(g * GROUP_DIM, (g + 1) * GROUP_DIM)
            wg = w_ref[g].astype(bf16)
            for r0 in range(0, s, ch):
                acc = pad_ref[POOL_HALO + r0:POOL_HALO + r0 + ch, cols]
                own = acc
                for dlt in range(1, win):
                    acc = acc + pad_ref[POOL_HALO + r0 - dlt:POOL_HALO + r0 - dlt + ch, cols]
                t = r0 + lax.broadcasted_iota(jnp.int32, (ch, 1), 0)
                cnt = jnp.minimum(t + 1, win).astype(f32)
                pooled = acc / cnt - own
                pooled_ref[r0:r0 + ch, cols] = pooled
                b_ref[r0:r0 + ch, cols] = _dot(pooled.astype(bf16), wg) * sc_ref[:, cols]

    return pl.pallas_call(
        body, grid=(1,),
        in_specs=[pl.BlockSpec((None, s, wd), lambda i: (3, 0, 0)), pl.BlockSpec(pool_w.shape, lambda i: (0, 0, 0)),
                  _vec(wd)],
        out_specs=[pl.BlockSpec((s, wd), lambda i: (0, 0))] * 2, out_shape=[_sds((s, wd))] * 2,
        scratch_shapes=[pltpu.VMEM((POOL_HALO + s, wd), f32)], name="pool_fwd",
        compiler_params=_params())(hsm, pool_w, pool_scale)


def _pool_bwd(dcat, pooled, pool_w, pool_scale):
    s, wd = pooled.shape
    ch = min(256, s)

    def body(db_ref, p_ref, w_ref, sc_ref, du_ref, dw_ref, dsc_ref, pad_ref):
        pad_ref[s:s + POOL_HALO, :] = jnp.zeros((POOL_HALO, wd), f32)
        for g, win in enumerate(POOL_WINDOWS):
            cols = slice(g * GROUP_DIM, (g + 1) * GROUP_DIM)
            wg = w_ref[g].astype(bf16)
            pooled_g = p_ref[:, cols].astype(bf16)
            db = db_ref[:, cols]
            dmixed = (db * sc_ref[:, cols]).astype(bf16)
            dsc_ref[:, cols] = jnp.sum(db * _dot(pooled_g, wg), axis=0, keepdims=True)
            dw_ref[g] = _dot(pooled_g, dmixed, _TN)
            dpooled = _dot(dmixed, wg, _NT)
            t = lax.broadcasted_iota(jnp.int32, (s, 1), 0)
            pad_ref[0:s, cols] = dpooled / jnp.minimum(t + 1, win).astype(f32)
            for r0 in range(0, s, ch):
                acc = pad_ref[r0:r0 + ch, cols]
                for dlt in range(1, win):
                    acc = acc + pad_ref[r0 + dlt:r0 + dlt + ch, cols]
                du_ref[r0:r0 + ch, cols] = acc - dpooled[r0:r0 + ch]

    return pl.pallas_call(
        body, grid=(1,),
        in_specs=[pl.BlockSpec((s, wd), lambda i: (0, 1)), pl.BlockSpec((s, wd), lambda i: (0, 0)),
                  pl.BlockSpec(pool_w.shape, lambda i: (0, 0, 0)), _vec(wd)],
        out_specs=[pl.BlockSpec((s, wd), lambda i: (0, 0)), pl.BlockSpec(pool_w.shape, lambda i: (0, 0, 0)), _vec(wd)],
        out_shape=[_sds((s, wd)), _sds(pool_w.shape), _sds((1, wd))],
        scratch_shapes=[pltpu.VMEM((s + POOL_HALO, wd), f32)], name="pool_bwd",
        compiler_params=_params())(dcat, pooled, pool_w, pool_scale)


def _conv_fwd(hsm, dw, ln_g, ln_b):
    _, s, wd = hsm.shape
    rows, halo = CONV_ROWS, CONV_HALO

    def body(a_ref, g_ref, dw_ref, lg_ref, lb_ref, out_ref, hc_ref, xh_ref, rs_ref, pad_ref):
        hc = a_ref[...] * _sigmoid(g_ref[...])
        hc_ref[...] = hc
        pad_ref[0:halo, :] = jnp.zeros((halo, wd), f32)
        pad_ref[halo:halo + s, :] = hc
        taps = dw_ref[...]

        def chunk(c, _):
            base = pl.multiple_of(c * rows, rows)
            win = pad_ref[pl.ds(base, rows + halo), :]
            y = jnp.zeros((rows, wd), f32)
            for k in range(CONV_TAPS):
                lo = halo - (CONV_TAPS - 1) + k
                y = y + taps[k:k + 1, :] * win[lo:lo + rows]
            mu = jnp.mean(y, axis=-1, keepdims=True)
            cen = y - mu
            rstd = lax.rsqrt(jnp.mean(cen * cen, axis=-1, keepdims=True) + LN_EPS)
            xh = cen * rstd
            n = xh * lg_ref[...] + lb_ref[...]
            out_ref[pl.ds(base, rows), :] = n * _sigmoid(n)
            xh_ref[pl.ds(base, rows), :] = xh
            rs_ref[pl.ds(base, rows), :] = rstd
            return 0

        lax.fori_loop(0, s // rows, chunk, 0)

    full = pl.BlockSpec((s, wd), lambda i: (0, 0))
    return pl.pallas_call(
        body, grid=(1,),
        in_specs=[pl.BlockSpec((None, s, wd), lambda i: (0, 0, 0)), pl.BlockSpec((None, s, wd), lambda i: (1, 0, 0)),
                  pl.BlockSpec(dw.shape, lambda i: (0, 0)), _vec(wd), _vec(wd)],
        out_specs=[full, full, full, pl.BlockSpec((s, 1), lambda i: (0, 0))],
        out_shape=[_sds((s, wd))] * 3 + [_sds((s, 1))],
        scratch_shapes=[pltpu.VMEM((halo + s, wd), f32)], name="conv_fwd",
        compiler_params=_params())(hsm, hsm, dw, ln_g, ln_b)


def _conv_bwd(dcat, hsm, hc, xh, rstd, dw, ln_g, ln_b):
    s, wd = hc.shape
    rows, halo = CONV_ROWS, CONV_HALO

    def body(dc_ref, a_ref, g_ref, hc_ref, xh_ref, rs_ref, dw_ref, lg_ref, lb_ref,
             da_ref, dg_ref, ddw_ref, dlg_ref, dlb_ref, hpad_ref, ypad_ref):
        hpad_ref[0:halo, :] = jnp.zeros((halo, wd), f32)
        hpad_ref[halo:halo + s, :] = hc_ref[...]
        ypad_ref[s:s + halo, :] = jnp.zeros((halo, wd), f32)
        ddw_ref[...] = jnp.zeros_like(ddw_ref)
        dlg_ref[...] = jnp.zeros_like(dlg_ref)
        dlb_ref[...] = jnp.zeros_like(dlb_ref)
        taps = dw_ref[...]

        def norm_bwd(c, _):
            base = pl.multiple_of(c * rows, rows)
            xhv = xh_ref[pl.ds(base, rows), :]
            n = xhv * lg_ref[...] + lb_ref[...]
            sn = _sigmoid(n)
            dn = dc_ref[pl.ds(base, rows), :] * sn * (1.0 + n * (1.0 - sn))
            dlg_ref[...] += jnp.sum(dn * xhv, axis=0, keepdims=True)
            dlb_ref[...] += jnp.sum(dn, axis=0, keepdims=True)
            ypad_ref[pl.ds(base, rows), :] = _ln_bwd_rows(dn, xhv, rs_ref[pl.ds(base, rows), :], lg_ref[...])
            return 0

        lax.fori_loop(0, s // rows, norm_bwd, 0)

        def conv_bwd(c, _):
            base = pl.multiple_of(c * rows, rows)
            ywin = ypad_ref[pl.ds(base, rows + halo), :]
            hwin = hpad_ref[pl.ds(base, rows + halo), :]
            dy = ywin[0:rows]
            dhc = jnp.zeros((rows, wd), f32)
            for k in range(CONV_TAPS):
                fwd = CONV_TAPS - 1 - k
                dhc = dhc + taps[k:k + 1, :] * ywin[fwd:fwd + rows]
                lo = halo - (CONV_TAPS - 1) + k
                ddw_ref[k:k + 1, :] += jnp.sum(dy * hwin[lo:lo + rows], axis=0, keepdims=True)
            sg = _sigmoid(g_ref[pl.ds(base, rows), :])
            da_ref[pl.ds(base, rows), :] = dhc * sg
            dg_ref[pl.ds(base, rows), :] = dhc * a_ref[pl.ds(base, rows), :] * sg * (1.0 - sg)
            return 0

        lax.fori_loop(0, s // rows, conv_bwd, 0)

    full = pl.BlockSpec((s, wd), lambda i: (0, 0))
    tap_spec = pl.BlockSpec(dw.shape, lambda i: (0, 0))
    return pl.pallas_call(
        body, grid=(1,),
        in_specs=[full, pl.BlockSpec((None, s, wd), lambda i: (0, 0, 0)), pl.BlockSpec((None, s, wd), lambda i: (1, 0, 0)),
                  full, full, pl.BlockSpec((s, 1), lambda i: (0, 0)), tap_spec, _vec(wd), _vec(wd)],
        out_specs=[full, full, tap_spec, _vec(wd), _vec(wd)],
        out_shape=[_sds((s, wd)), _sds((s, wd)), _sds(dw.shape), _sds((1, wd)), _sds((1, wd))],
        scratch_shapes=[pltpu.VMEM((halo + s, wd), f32), pltpu.VMEM((s + halo, wd), f32)], name="conv_bwd",
        compiler_params=_params())(dcat, hsm, hsm, hc, xh, rstd, dw, ln_g, ln_b)


_GELU_C = 0.7978845608028654
_GELU_A = 0.044715


def _gelu(x):
    return 0.5 * x * (1.0 + jnp.tanh(_GELU_C * (x + _GELU_A * x * x * x)))


def _gelu_grad(x):
    th = jnp.tanh(_GELU_C * (x + _GELU_A * x * x * x))
    return 0.5 * (1.0 + th) + 0.5 * x * (1.0 - th * th) * _GELU_C * (1.0 + 3.0 * _GELU_A * x * x)


def _causal_sg_w(w_ref, g):
    row = lax.broadcasted_iota(jnp.int32, (SG_CHUNK, SG_CHUNK), 0)
    col = lax.broadcasted_iota(jnp.int32, (SG_CHUNK, SG_CHUNK), 1)
    return jnp.where(col <= row, w_ref[g], 0.0).astype(bf16), col <= row


def _gmlp_fwd(hsm, ln_g, ln_b, sg_w, sg_bt):
    _, s, wd = hsm.shape
    ck = SG_CHUNK

    def body(zu_ref, zv_ref, lg_ref, lb_ref, w_ref, bt_ref, out_ref, xh_ref, rs_ref):
        u = _gelu(zu_ref[...])
        vg = _gelu(zv_ref[...])
        mu = jnp.mean(vg, axis=-1, keepdims=True)
        cen = vg - mu
        rstd = lax.rsqrt(jnp.mean(cen * cen, axis=-1, keepdims=True) + LN_EPS)
        xh = cen * rstd
        xh_ref[...] = xh
        rs_ref[...] = rstd
        vn = (xh * lg_ref[...] + lb_ref[...]).astype(bf16)
        for g in range(4):
            cols = slice(g * GROUP_DIM, (g + 1) * GROUP_DIM)
            wm, _ = _causal_sg_w(w_ref, g)
            sv = _dot(wm, vn[:, cols]) + bt_ref[:, g:g + 1]
            out_ref[:, cols] = u[:, cols] * sv

    rows_spec = pl.BlockSpec((ck, wd), lambda i: (i, 0))
    return pl.pallas_call(
        body, grid=(s // ck,),
        in_specs=[pl.BlockSpec((None, ck, wd), lambda i: (2, i, 0)), pl.BlockSpec((None, ck, wd), lambda i: (3, i, 0)),
                  _vec(wd), _vec(wd), pl.BlockSpec(sg_w.shape, lambda i: (0, 0, 0)),
                  pl.BlockSpec(sg_bt.shape, lambda i: (0, 0))],
        out_specs=[rows_spec, rows_spec, pl.BlockSpec((ck, 1), lambda i: (i, 0))],
        out_shape=[_sds((s, wd)), _sds((s, wd)), _sds((s, 1))], name="gmlp_fwd",
        compiler_params=_params())(hsm, hsm, ln_g, ln_b, sg_w, sg_bt)


def _gmlp_bwd(dcat, hsm, xh, rstd, ln_g, ln_b, sg_w, sg_bt):
    s, wd = xh.shape
    ck = SG_CHUNK

    def body(dd_ref, zu_ref, zv_ref, xh_ref, rs_ref, lg_ref, lb_ref, w_ref, bt_ref,
             dzu_ref, dzv_ref, dw_ref, dbb_ref, dlg_ref, dlb_ref):
        @pl.when(pl.program_id(0) == 0)
        def _():
            dw_ref[...] = jnp.zeros_like(dw_ref)
            dbb_ref[...] = jnp.zeros_like(dbb_ref)
            dlg_ref[...] = jnp.zeros_like(dlg_ref)
            dlb_ref[...] = jnp.zeros_like(dlb_ref)

        zu, zv, dd, xhv = zu_ref[...], zv_ref[...], dd_ref[...], xh_ref[...]
        u = _gelu(zu)
        vn = (xhv * lg_ref[...] + lb_ref[...]).astype(bf16)
        du_parts, dvn_parts = [], []
        for g in range(4):
            cols = slice(g * GROUP_DIM, (g + 1) * GROUP_DIM)
            wm, keep = _causal_sg_w(w_ref, g)
            sv = _dot(wm, vn[:, cols]) + bt_ref[:, g:g + 1]
            du_parts.append(dd[:, cols] * sv)
            dsv = dd[:, cols] * u[:, cols]
            dsvb = dsv.astype(bf16)
            dbb_ref[g] += jnp.broadcast_to(jnp.sum(dsv, axis=1, keepdims=True), (ck, GROUP_DIM))
            dw_ref[g] += jnp.where(keep, _dot(dsvb, vn[:, cols], _NT), 0.0)
            dvn_parts.append(_dot(wm, dsvb, _TN))
        du = jnp.concatenate(du_parts, axis=1)
        dvn = jnp.concatenate(dvn_parts, axis=1)
        dlg_ref[...] += jnp.sum(dvn * xhv, axis=0, keepdims=True)
        dlb_ref[...] += jnp.sum(dvn, axis=0, keepdims=True)
        dzv_ref[...] = _ln_bwd_rows(dvn, xhv, rs_ref[...], lg_ref[...]) * _gelu_grad(zv)
        dzu_ref[...] = du * _gelu_grad(zu)

    rows_spec = pl.BlockSpec((ck, wd), lambda i: (i, 0))
    wspec = pl.BlockSpec(sg_w.shape, lambda i: (0, 0, 0))
    return pl.pallas_call(
        body, grid=(s // ck,),
        in_specs=[pl.BlockSpec((ck, wd), lambda i: (i, 1)), pl.BlockSpec((None, ck, wd), lambda i: (2, i, 0)),
                  pl.BlockSpec((None, ck, wd), lambda i: (3, i, 0)), rows_spec, pl.BlockSpec((ck, 1), lambda i: (i, 0)),
                  _vec(wd), _vec(wd), wspec, pl.BlockSpec(sg_bt.shape, lambda i: (0, 0))],
        out_specs=[rows_spec, rows_spec, wspec, wspec, _vec(wd), _vec(wd)],
        out_shape=[_sds((s, wd)), _sds((s, wd)), _sds(sg_w.shape), _sds(sg_w.shape), _sds((1, wd)), _sds((1, wd))],
        name="gmlp_bwd", compiler_params=_params())(dcat, hsm, hsm, xh, rstd, ln_g, ln_b, sg_w, sg_bt)


def _to_heads(x2d):
    s = x2d.shape[0]
    return jnp.transpose(x2d.reshape(s, N_HEADS, HEAD_DIM), (1, 0, 2)).astype(bf16)


def _from_heads(x3d):
    s = x3d.shape[1]
    return jnp.transpose(x3d, (1, 0, 2)).reshape(s, N_HEADS * HEAD_DIM)


def _local_step(x, p, target, wts, late_shards, small):
    wts = dict(wts)
    saved = []
    for i in range(DEPTH):
        tag = f"l{i}"
        if i % 2 == 0:
            hsm = _mm_nn_col(tag + "_in", x, wts["even_w_in"], 0)
            q, k, v = _to_heads(hsm[0]), _to_heads(hsm[1]), _to_heads(hsm[2])
            att_heads, sba_tot, late = _sba_fwd(q, k, v, late_shards)
            wts.update(zip(_LATE, late))
            att = _from_heads(att_heads)
            pool_out, pooled = _pool_fwd(hsm, small["pool_w"], small["pool_scale"])
            cat = jnp.concatenate([att, pool_out], axis=1).astype(bf16)
            mix = _mm_nn_row(tag + "_out", cat, wts["even_w_out"], 0)
            mixer_saved = (hsm, q, k, v, sba_tot, pooled, cat)
        else:
            hsm = _mm_nn_col(tag + "_in", x, wts["odd_w_in"], 0)
            conv_out, hc, cxh, crs = _conv_fwd(hsm, small["conv_dw"], small["conv_ln_g"], small["conv_ln_b"])
            sg_out, sxh, srs = _gmlp_fwd(hsm, small["sg_ln_g"], small["sg_ln_b"], small["sg_w"], small["sg_bt"])
            cat = jnp.concatenate([conv_out, sg_out], axis=1).astype(bf16)
            mix = _mm_nn_row(tag + "_out", cat, wts["odd_w_out"], 0)
            mixer_saved = (hsm, hc, cxh, crs, sxh, srs, cat)
        x1, xh1, rs1 = _ln_fwd(tag + "_ln_mix", x, mix, small["ln_mix_g"][i:i + 1], small["ln_mix_b"][i:i + 1])
        hg = _mm_nn_col(tag + "_gate", x1, wts["ffn_w_gate"], i)
        hu = _mm_nn_col(tag + "_up", x1, wts["ffn_w_up"], i)
        act = _swiglu_fwd(tag + "_swiglu", hg, hu)
        ffn = _mm_nn_row(tag + "_down", act, wts["ffn_w_down"], i)
        x2, xh2, rs2 = _ln_fwd(tag + "_ln_ffn", x1, ffn, small["ln_ffn_g"][i:i + 1], small["ln_ffn_b"][i:i + 1])
        gp = _mm_nn_row(tag + "_ple_gate", x2, wts["ple_w_gate"], i)
        pp = _mm_nn_col(tag + "_ple_proj", p[i], wts["ple_w_proj"], i, natural=True)
        x3 = _ple_fwd(tag + "_ple", x2, gp, small["ple_b_gate"][i:i + 1], pp)
        saved.append((x, mixer_saved, x1, xh1, rs1, hg, hu, act, x2, xh2, rs2, gp, pp))
        x = x3

    loss_part, dx = _loss_head("loss_head", x, target)

    big = {n: [None] * wts[n].shape[1] for n in wts}
    received = {}
    sm = {}
    per_layer = {n: [None] * DEPTH for n in ("ln_mix_g", "ln_mix_b", "ln_ffn_g", "ln_ffn_b", "ple_b_gate")}
    for i in reversed(range(DEPTH)):
        tag = f"l{i}b"
        x0, mixer_saved, x1, xh1, rs1, hg, hu, act, x2, xh2, rs2, gp, pp = saved[i]
        dgp, dpp, per_layer["ple_b_gate"][i] = _ple_bwd(tag + "_ple", dx, gp, small["ple_b_gate"][i:i + 1], pp)
        big["ple_w_proj"][i] = _mm_tn_col(tag + "_dproj", p[i], dpp)
        big["ple_w_gate"][i] = _mm_tn_row(tag + "_dgate", x2, dgp)
        dx2 = _mm_nt_row(tag + "_dx2", dgp, wts["ple_w_gate"], i, natural=True, add=dx)
        dr2, per_layer["ln_ffn_g"][i], per_layer["ln_ffn_b"][i] = _ln_bwd(tag + "_ln_ffn", dx2, xh2, rs2,
                                                                           small["ln_ffn_g"][i:i + 1])
        dact = _mm_nt_row(tag + "_dact", dr2, wts["ffn_w_down"], i)
        big["ffn_w_down"][i] = _mm_tn_row(tag + "_ddown", act, dr2)
        dhg, dhu = _swiglu_bwd(tag + "_swiglu", dact, hg, hu)
        big["ffn_w_gate"][i] = _mm_tn_col(tag + "_dgatew", x1, dhg)
        big["ffn_w_up"][i] = _mm_tn_col(tag + "_dupw", x1, dhu)
        part = _mm_nt_col(tag + "_dx1a", dhg, wts["ffn_w_gate"], i, dr2, ALPHA)
        dx1 = _mm_nt_col(tag + "_dx1b", dhu, wts["ffn_w_up"], i, part, 1.0)
        dr1, per_layer["ln_mix_g"][i], per_layer["ln_mix_b"][i] = _ln_bwd(tag + "_ln_mix", dx1, xh1, rs1,
                                                                           small["ln_mix_g"][i:i + 1])
        if i % 2 == 0:
            hsm, q, k, v, sba_tot, pooled, cat = mixer_saved
            big["even_w_out"][0] = _mm_tn_row(tag + "_dout", cat, dr1)
            dcat = _mm_nt_row(tag + "_dcat", dr1, wts["even_w_out"], 0, natural=True)
            entries, layout = _scatter_plan(big, _LATE)
            dq, dk, dv, got = _sba_bwd(q, k, v, sba_tot, _to_heads(dcat[:, :N_HEADS * HEAD_DIM]), entries, layout)
            received.update(zip(_LATE, got))
            du, sm["pool_w"], sm["pool_scale"] = _pool_bwd(dcat, pooled, small["pool_w"], small["pool_scale"])
            dhsm = jnp.stack([_from_heads(dq), _from_heads(dk), _from_heads(dv), du]).astype(bf16)
            w_in = "even_w_in"
        else:
            hsm, hc, cxh, crs, sxh, srs, cat = mixer_saved
            big["odd_w_out"][0] = _mm_tn_row(tag + "_dout", cat, dr1)
            dcat = _mm_nt_row(tag + "_dcat", dr1, wts["odd_w_out"], 0, natural=True)
            da, dg, sm["conv_dw"], sm["conv_ln_g"], sm["conv_ln_b"] = _conv_bwd(
                dcat, hsm, hc, cxh, crs, small["conv_dw"], small["conv_ln_g"], small["conv_ln_b"])
            dzu, dzv, sm["sg_w"], dsgb, sm["sg_ln_g"], sm["sg_ln_b"] = _gmlp_bwd(
                dcat, hsm, sxh, srs, small["sg_ln_g"], small["sg_ln_b"], small["sg_w"], small["sg_bt"])
            sm["sg_b"] = dsgb[:, :, 0]
            dhsm = jnp.stack([da, dg, dzu, dzv]).astype(bf16)
            w_in = "odd_w_in"
        big[w_in][0] = _mm_tn_col(tag + "_din", x0, dhsm)
        dx = _mm_nt_col(tag + "_dx", dhsm, wts[w_in], 0, dr1, ALPHA)
    for n, parts in per_layer.items():
        sm[n] = jnp.concatenate(parts, axis=0)
    return loss_part, dx, received, _scatter_plan(big, _EARLY), sm


def _scatter_plan(big, names):
    entries, layout = [], []
    for pi, n in enumerate(names):
        for li, g in enumerate(big[n]):
            entries.append(g)
            layout.append((pi, li))
    return entries, layout


def _place():
    x, y, c = lax.axis_index("x"), lax.axis_index("y"), lax.axis_index("c")
    return x, y, c, [(1 - x, y), (x, 1 - y), (1 - x, 1 - y)]


def _gather_chips(shards):
    n = len(shards)

    def body(*refs):
        copies = _gather_copies(refs[:n], refs[n:2 * n], *refs[2 * n:])
        for cp in copies:
            cp.start()
        for cp in copies:
            cp.wait()

    return pl.pallas_call(body, in_specs=[_ANY] * n, out_specs=[_ANY] * n, out_shape=_gather_shapes(shards),
                          scratch_shapes=_comm_sems(n), name="gather_chips")(*shards)


def _comm_sems(n):
    return [pltpu.SemaphoreType.DMA((n, 3)), pltpu.SemaphoreType.DMA((n, 3)), pltpu.SemaphoreType.DMA((n,))]


def _gather_shapes(shards):
    return [_sds((N_CHIPS,) + a.shape, a.dtype) for a in shards]


def _gather_copies(ins, outs, send, recv, loc):
    x, y, c, chips = _place()
    mine = 2 * x + y
    copies = []
    for t in range(len(ins)):
        copies.append(pltpu.make_async_copy(ins[t], outs[t].at[mine], loc.at[t]))
        for j, (cx, cy) in enumerate(chips):
            copies.append(pltpu.make_async_remote_copy(
                src_ref=ins[t], dst_ref=outs[t].at[mine], send_sem=send.at[t, j], recv_sem=recv.at[t, j],
                device_id=(cx, cy, c), device_id_type=_MESH))
    return copies


def _gather_halves_copies(ins, outs, send, recv, loc, send_on, recv_on):
    x, y, c, chips = _place()
    mine = 2 * x + y
    local, over_ici, onward = [], [], []
    for t in range(len(ins)):
        nl, r, _ = ins[t].shape
        half = r // 2
        rows = pl.ds(pl.multiple_of(c * half, 32), half)
        local.append(pltpu.make_async_copy(ins[t], outs[t].at[mine], loc.at[t]))
        for j, (cx, cy) in enumerate(chips):
            over_ici.append(pltpu.make_async_remote_copy(
                src_ref=ins[t].at[pl.ds(0, nl), rows], dst_ref=outs[t].at[mine, pl.ds(0, nl), rows],
                send_sem=send.at[t, j], recv_sem=recv.at[t, j], device_id=(cx, cy, c), device_id_type=_MESH))
            landed = outs[t].at[2 * cx + cy, pl.ds(0, nl), rows]
            onward.append(pltpu.make_async_remote_copy(
                src_ref=landed, dst_ref=landed, send_sem=send_on.at[t, j], recv_sem=recv_on.at[t, j],
                device_id=(x, y, 1 - c), device_id_type=_MESH))
    return local, over_ici, onward


def _scatter_shapes(grads, layout):
    shapes = {}
    for e, (pi, li) in enumerate(layout):
        r, cdim = grads[e].shape[1:]
        shapes[pi] = (N_CHIPS, max(li + 1, shapes.get(pi, (0, 0))[1]), r, cdim)
    return [_sds(shapes[pi], grads[0].dtype) for pi in range(len(shapes))]


def _scatter_copies(ins, outs, layout, send, recv, loc):
    x, y, c, chips = _place()
    mine = 2 * x + y
    copies = []
    for e, (pi, li) in enumerate(layout):
        copies.append(pltpu.make_async_copy(ins[e].at[mine], outs[pi].at[mine, li], loc.at[e]))
        for j, (cx, cy) in enumerate(chips):
            copies.append(pltpu.make_async_remote_copy(
                src_ref=ins[e].at[2 * cx + cy], dst_ref=outs[pi].at[mine, li], send_sem=send.at[e, j],
                recv_sem=recv.at[e, j], device_id=(cx, cy, c), device_id_type=_MESH))
    return copies


def _final_exchange(grads, layout, block):
    n_in = len(grads)
    out_shape = _scatter_shapes(grads, layout)
    n_out = len(out_shape)

    def body(*refs):
        blk_in, blk_out = refs[n_in], refs[n_in + 1 + n_out]
        send, recv, loc, send_all, recv_all, loc_all = refs[n_in + n_out + 2:]
        copies = _scatter_copies(refs[:n_in], refs[n_in + 1:n_in + 1 + n_out], layout, send, recv, loc)
        x, y, c, _ = _place()
        mine = 4 * x + 2 * y + c
        copies.append(pltpu.make_async_copy(blk_in, blk_out.at[mine], loc_all))
        for m in range(1, N_DEV):
            fx, fy, fc = (m >> 2) & 1, (m >> 1) & 1, m & 1
            peer = (x + fx - 2 * x * fx, y + fy - 2 * y * fy, c + fc - 2 * c * fc)
            copies.append(pltpu.make_async_remote_copy(
                src_ref=blk_in, dst_ref=blk_out.at[mine], send_sem=send_all.at[m - 1], recv_sem=recv_all.at[m - 1],
                device_id=peer, device_id_type=_MESH))
        for cp in copies:
            cp.start()
        for cp in copies:
            cp.wait()

    outs = pl.pallas_call(
        body, in_specs=[_ANY] * (n_in + 1), out_specs=[_ANY] * (n_out + 1),
        out_shape=out_shape + [_sds((N_DEV,) + block.shape)],
        scratch_shapes=_comm_sems(n_in) + [pltpu.SemaphoreType.DMA((N_DEV - 1,)), pltpu.SemaphoreType.DMA((N_DEV - 1,)),
                                           pltpu.SemaphoreType.DMA(())],
        name="final_exchange")(*grads, block)
    return outs[:n_out], outs[n_out]


def _swap_cores(arrays):
    n = len(arrays)

    def body(*refs):
        ins, outs = refs[:n], refs[n:2 * n]
        send, recv = refs[2 * n:]
        x, y, c, _ = _place()
        started = []
        for t in range(n):
            rc = pltpu.make_async_remote_copy(src_ref=ins[t], dst_ref=outs[t], send_sem=send.at[t], recv_sem=recv.at[t],
                                              device_id=(x, y, 1 - c), device_id_type=_MESH)
            rc.start()
            started.append(rc)
        for rc in started:
            rc.wait()

    return pl.pallas_call(
        body, in_specs=[_ANY] * n, out_specs=[_ANY] * n, out_shape=[_sds(a.shape, a.dtype) for a in arrays],
        scratch_shapes=[pltpu.SemaphoreType.DMA((n,)), pltpu.SemaphoreType.DMA((n,))], name="swap_cores")(*arrays)


def _row_tile(r):
    for t in (256, 128, 64, 32, 16, 8):
        if r % t == 0:
            return t
    return r


def _sum_stack(name, stack):
    n, r, c = stack.shape
    tr = _row_tile(r)

    def body(s_ref, o_ref):
        acc = s_ref[0].astype(f32)
        for t in range(1, n):
            acc = acc + s_ref[t].astype(f32)
        o_ref[...] = acc

    return pl.pallas_call(body, grid=(r // tr,), in_specs=[pl.BlockSpec((n, tr, c), lambda i: (0, i, 0))],
                          out_specs=pl.BlockSpec((tr, c), lambda i: (i, 0)), out_shape=_sds((r, c)), name=name,
                          compiler_params=_params())(stack)


def _adamw(name, w, g_a, g_b, m, v):
    r, c = w.shape
    tr = _row_tile(r)
    two = g_b is not None
    bc1 = 1.0 - ADAM_B1 ** ADAM_STEP
    bc2 = 1.0 - ADAM_B2 ** ADAM_STEP

    def body(*refs):
        if two:
            w_ref, ga_ref, gb_ref, m_ref, v_ref, g_out, d_out, m_out, v_out = refs
            g = ga_ref[...] + gb_ref[...]
        else:
            w_ref, ga_ref, m_ref, v_ref, g_out, d_out, m_out, v_out = refs
            g = ga_ref[...]
        m_new = ADAM_B1 * m_ref[...] + (1.0 - ADAM_B1) * g
        v_new = ADAM_B2 * v_ref[...] + (1.0 - ADAM_B2) * (g * g)
        g_out[...] = g
        m_out[...] = m_new
        v_out[...] = v_new
        d_out[...] = -ADAM_LR * ((m_new / bc1) / (jnp.sqrt(v_new / bc2) + ADAM_EPS) + ADAM_WD * w_ref[...])

    spec = pl.BlockSpec((tr, c), lambda i: (i, 0))
    ins = [w, g_a] + ([g_b] if two else []) + [m, v]
    return pl.pallas_call(body, grid=(r // tr,), in_specs=[spec] * len(ins), out_specs=[spec] * 4,
                          out_shape=[_sds((r, c))] * 4, name=name, compiler_params=_params())(*ins)


_EARLY = ("even_w_in", "even_w_out")
_LATE = ("odd_w_in", "odd_w_out", "ffn_w_gate", "ffn_w_up", "ffn_w_down", "ple_w_proj", "ple_w_gate")
_BIG = _EARLY + _LATE
_SHARDED_SMALL = ("conv_dw", "conv_ln_g", "conv_ln_b", "sg_ln_g", "sg_ln_b")
_SMALL = ("pool_w", "pool_scale", "conv_dw", "conv_ln_g", "conv_ln_b", "sg_ln_g", "sg_ln_b", "sg_w", "sg_b",
          "ln_mix_g", "ln_mix_b", "ln_ffn_g", "ln_ffn_b", "ple_b_gate")
_WEIGHTS = ("even_w_in", "even_w_out", "pool_w", "pool_scale", "odd_w_in", "odd_w_out", "conv_dw", "conv_ln_g",
            "conv_ln_b", "sg_ln_g", "sg_ln_b", "sg_w", "sg_b", "ln_mix_g", "ln_mix_b", "ffn_w_gate", "ffn_w_up",
            "ffn_w_down", "ln_ffn_g", "ln_ffn_b", "ple_w_proj", "ple_w_gate", "ple_b_gate")


def _pack(arrays):
    flat = jnp.concatenate([a.reshape(-1) for a in arrays])
    pad = (-flat.shape[0]) % (256 * LANES)
    return jnp.pad(flat, (0, pad)).reshape(-1, LANES)


def _unpack(packed, shapes):
    flat = packed.reshape(-1)
    out, off = [], 0
    for shp in shapes:
        size = 1
        for dim in shp:
            size *= dim
        out.append(flat[off:off + size].reshape(shp))
        off += size
    return out


def _unshard_last(g4):
    return jnp.concatenate([g4[k] for k in range(N_CHIPS)], axis=-1)


def kernel(x, p, even_w_in, even_w_out, pool_w, pool_scale, odd_w_in, odd_w_out, conv_dw, conv_ln_g, conv_ln_b, sg_ln_g, sg_ln_b, sg_w, sg_b, ln_mix_g, ln_mix_b, ffn_w_gate, ffn_w_up, ffn_w_down, ln_ffn_g, ln_ffn_b, ple_w_proj, ple_w_gate, ple_b_gate, loss_target, m_even_w_in, m_even_w_out, m_pool_w, m_pool_scale, m_odd_w_in, m_odd_w_out, m_conv_dw, m_conv_ln_g, m_conv_ln_b, m_sg_ln_g, m_sg_ln_b, m_sg_w, m_sg_b, m_ln_mix_g, m_ln_mix_b, m_ffn_w_gate, m_ffn_w_up, m_ffn_w_down, m_ln_ffn_g, m_ln_ffn_b, m_ple_w_proj, m_ple_w_gate, m_ple_b_gate, v_even_w_in, v_even_w_out, v_pool_w, v_pool_scale, v_odd_w_in, v_odd_w_out, v_conv_dw, v_conv_ln_g, v_conv_ln_b, v_sg_ln_g, v_sg_ln_b, v_sg_w, v_sg_b, v_ln_mix_g, v_ln_mix_b, v_ffn_w_gate, v_ffn_w_up, v_ffn_w_down, v_ln_ffn_g, v_ln_ffn_b, v_ple_w_proj, v_ple_w_gate, v_ple_b_gate):
    w = dict(even_w_in=even_w_in, even_w_out=even_w_out, pool_w=pool_w, pool_scale=pool_scale, odd_w_in=odd_w_in,
             odd_w_out=odd_w_out, conv_dw=conv_dw, conv_ln_g=conv_ln_g, conv_ln_b=conv_ln_b, sg_ln_g=sg_ln_g,
             sg_ln_b=sg_ln_b, sg_w=sg_w, sg_b=sg_b, ln_mix_g=ln_mix_g, ln_mix_b=ln_mix_b, ffn_w_gate=ffn_w_gate,
             ffn_w_up=ffn_w_up, ffn_w_down=ffn_w_down, ln_ffn_g=ln_ffn_g, ln_ffn_b=ln_ffn_b, ple_w_proj=ple_w_proj,
             ple_w_gate=ple_w_gate, ple_b_gate=ple_b_gate)
    mom = dict(even_w_in=m_even_w_in, even_w_out=m_even_w_out, pool_w=m_pool_w, pool_scale=m_pool_scale,
               odd_w_in=m_odd_w_in, odd_w_out=m_odd_w_out, conv_dw=m_conv_dw, conv_ln_g=m_conv_ln_g,
               conv_ln_b=m_conv_ln_b, sg_ln_g=m_sg_ln_g, sg_ln_b=m_sg_ln_b, sg_w=m_sg_w, sg_b=m_sg_b,
               ln_mix_g=m_ln_mix_g, ln_mix_b=m_ln_mix_b, ffn_w_gate=m_ffn_w_gate, ffn_w_up=m_ffn_w_up,
               ffn_w_down=m_ffn_w_down, ln_ffn_g=m_ln_ffn_g, ln_ffn_b=m_ln_ffn_b, ple_w_proj=m_ple_w_proj,
               ple_w_gate=m_ple_w_gate, ple_b_gate=m_ple_b_gate)
    var = dict(even_w_in=v_even_w_in, even_w_out=v_even_w_out, pool_w=v_pool_w, pool_scale=v_pool_scale,
               odd_w_in=v_odd_w_in, odd_w_out=v_odd_w_out, conv_dw=v_conv_dw, conv_ln_g=v_conv_ln_g,
               conv_ln_b=v_conv_ln_b, sg_ln_g=v_sg_ln_g, sg_ln_b=v_sg_ln_b, sg_w=v_sg_w, sg_b=v_sg_b,
               ln_mix_g=v_ln_mix_g, ln_mix_b=v_ln_mix_b, ffn_w_gate=v_ffn_w_gate, ffn_w_up=v_ffn_w_up,
               ffn_w_down=v_ffn_w_down, ln_ffn_g=v_ln_ffn_g, ln_ffn_b=v_ln_ffn_b, ple_w_proj=v_ple_w_proj,
               ple_w_gate=v_ple_w_gate, ple_b_gate=v_ple_b_gate)

    gathered = _gather_chips([w[n].astype(bf16) for n in _EARLY] + [w[n] for n in _SHARDED_SMALL])
    wts = dict(zip(_EARLY, gathered[:len(_EARLY)]))
    small = {n: w[n][0] for n in ("pool_w", "sg_w")}
    small.update({n: w[n] for n in ("pool_scale", "ln_mix_g", "ln_mix_b", "ln_ffn_g", "ln_ffn_b", "ple_b_gate")})
    small["sg_bt"] = jnp.transpose(w["sg_b"][0])
    for n, g4 in zip(_SHARDED_SMALL, gathered[len(_EARLY):]):
        small[n] = _unshard_last(g4)[0]
        if n != "conv_dw":
            small[n] = small[n][None]

    loss_part, grad_x, received, last_plan, sm = _local_step(x[0], p[:, 0], loss_target[0], wts,
                                                             [w[n].astype(bf16) for n in _LATE], small)

    sm_shapes = [(1,) + sm[n].shape if n in ("pool_w", "sg_w", "conv_dw", "sg_b") else sm[n].shape for n in _SMALL]
    got, small_stack = _final_exchange(*last_plan, _pack([sm[n] for n in _SMALL] + [loss_part[0, 0:1]]))
    received.update(zip(_EARLY, got))

    chip_sums = []
    for n in _BIG:
        _, nl, r, c = received[n].shape
        chip_sums.append(_sum_stack("sum_" + n, received[n].reshape(N_CHIPS, nl * r, c)))
    other = _swap_cores(chip_sums)
    results = {}
    for n, mine, theirs in zip(_BIG, chip_sums, other):
        shp = w[n].shape
        flat = (shp[0] * shp[1], shp[2])
        outs = _adamw("adamw_" + n, w[n].reshape(flat), mine, theirs, mom[n].reshape(flat), var[n].reshape(flat))
        results[n] = [o.reshape(shp) for o in outs]

    total = _sum_stack("sum_small", small_stack)
    parts = _unpack(total, sm_shapes + [(1,)])
    loss = parts[-1][0]
    chip = 2 * lax.axis_index("x") + lax.axis_index("y")
    g_small = {}
    for n, g in zip(_SMALL, parts[:-1]):
        if n in _SHARDED_SMALL:
            width = w[n].shape[-1]
            g = lax.dynamic_slice_in_dim(g, chip * width, width, axis=g.ndim - 1)
        g_small[n] = g
    shapes = [w[n].shape for n in _SMALL]
    outs = _adamw("adamw_small", _pack([w[n] for n in _SMALL]), _pack([g_small[n] for n in _SMALL]), None,
                  _pack([mom[n] for n in _SMALL]), _pack([var[n] for n in _SMALL]))
    unpacked = [_unpack(o, shapes) for o in outs]
    for idx, n in enumerate(_SMALL):
        results[n] = [u[idx] for u in unpacked]

    return (loss, grad_x[None], *[results[n][0] for n in _WEIGHTS], *[results[n][1] for n in _WEIGHTS],
            *[results[n][2] for n in _WEIGHTS], *[results[n][3] for n in _WEIGHTS])
```

```python
import jax
import jax.numpy as jnp
from jax import lax
from jax.experimental import pallas as pl
from jax.experimental.pallas import tpu as pltpu

f32 = jnp.float32
bf16 = jnp.bfloat16

D_MODEL = 1024
N_HEADS = 8
HEAD_DIM = 64
Q_BLOCK = 128
POOL_WINDOWS = (2, 4, 8, 16)
GROUP_DIM = 128
CONV_TAPS = 31
SG_CHUNK = 128
DEPTH = 2
ALPHA = (2 * DEPTH) ** 0.25
LN_EPS = 1e-5
SB_SCALE = HEAD_DIM ** -0.5
ADAM_LR, ADAM_B1, ADAM_B2, ADAM_EPS, ADAM_WD, ADAM_STEP = 0.001, 0.9, 0.999, 1e-08, 0.01, 10
N_CHIPS = 4
N_DEV = 8
LANES = 128
VMEM_LIMIT = 56 * 1024 * 1024
TM = 1024
TR = 512
SBA_FWD_HEADS = 8
SBA_BWD_HEADS = 8
CONV_ROWS = 64
CONV_HALO = 32
POOL_HALO = 16

_NN = (((1,), (0,)), ((), ()))
_NT = (((1,), (1,)), ((), ()))
_TN = (((0,), (0,)), ((), ()))
_ANY = pl.BlockSpec(memory_space=pl.ANY)
_MESH = pl.DeviceIdType.MESH


def _params():
    return pltpu.CompilerParams(vmem_limit_bytes=VMEM_LIMIT)


def _sds(shape, dtype=f32):
    return jax.ShapeDtypeStruct(tuple(shape), dtype)


def _dot(a, b, dims=_NN):
    return lax.dot_general(a, b, dims, preferred_element_type=f32)


def _sigmoid(x):
    return 1.0 / (1.0 + jnp.exp(-x))


def _mm(name, a, b, grid, a_spec, b_spec, out_shape, out_spec, dims, reduce=False, add=None, add_spec=None,
        add_scale=1.0, out_dtype=f32):
    has_add = add is not None
    k_axis = len(grid) - 1

    def body(*refs):
        if has_add:
            a_ref, b_ref, add_ref, o_ref = refs
        else:
            a_ref, b_ref, o_ref = refs
        r = _dot(a_ref[...].astype(bf16), b_ref[...].astype(bf16), dims)
        if reduce:
            k = pl.program_id(k_axis)

            @pl.when(k == 0)
            def _():
                o_ref[...] = r + add_scale * add_ref[...] if has_add else r

            @pl.when(k > 0)
            def _():
                o_ref[...] += r
        else:
            o_ref[...] = (r + add_scale * add_ref[...] if has_add else r).astype(out_dtype)

    ins = [a, b] + ([add] if has_add else [])
    specs = [a_spec, b_spec] + ([add_spec] if has_add else [])
    return pl.pallas_call(body, grid=grid, in_specs=specs, out_specs=out_spec, out_shape=_sds(out_shape, out_dtype),
                          name=name, compiler_params=_params())(*ins)


def _tm(s):
    return min(TM, s)


def _act_spec(a, tm, width):
    if a.ndim == 3:
        return pl.BlockSpec((None, tm, width), lambda i, k: (k, i, 0))
    return pl.BlockSpec((tm, width), lambda i, k: (i, k))


def _mm_nn_col(name, x, w4, layer, natural=False):
    s, kk = x.shape
    nq = w4.shape[3]
    tm = _tm(s)
    if natural:
        out_shape, out_spec = (s, 4 * nq), pl.BlockSpec((tm, nq), lambda i, k: (i, k))
    else:
        out_shape, out_spec = (4, s, nq), pl.BlockSpec((None, tm, nq), lambda i, k: (k, i, 0))
    return _mm(name, x, w4, (s // tm, 4), pl.BlockSpec((tm, kk), lambda i, k: (i, 0)),
               pl.BlockSpec((None, None, kk, nq), lambda i, k: (k, layer, 0, 0)), out_shape, out_spec, _NN)


def _mm_nn_row(name, a, w4, layer):
    s = a.shape[-2]
    kq, n = w4.shape[2], w4.shape[3]
    tm = _tm(s)
    return _mm(name, a, w4, (s // tm, 4), _act_spec(a, tm, kq),
               pl.BlockSpec((None, None, kq, n), lambda i, k: (k, layer, 0, 0)), (s, n),
               pl.BlockSpec((tm, n), lambda i, k: (i, 0)), _NN, reduce=True)


def _mm_nt_col(name, dh, w4, layer, add, add_scale):
    s = dh.shape[-2]
    kk, nq = w4.shape[2], w4.shape[3]
    tm = _tm(s)
    row = pl.BlockSpec((tm, kk), lambda i, k: (i, 0))
    return _mm(name, dh, w4, (s // tm, 4), _act_spec(dh, tm, nq),
               pl.BlockSpec((None, None, kk, nq), lambda i, k: (k, layer, 0, 0)), (s, kk), row, _NT,
               reduce=True, add=add, add_spec=row, add_scale=add_scale)


def _mm_nt_row(name, dy, w4, layer, natural=False, add=None):
    s, n = dy.shape
    kq = w4.shape[2]
    tm = _tm(s)
    if natural:
        out_shape, out_spec = (s, 4 * kq), pl.BlockSpec((tm, kq), lambda i, k: (i, k))
    else:
        out_shape, out_spec = (4, s, kq), pl.BlockSpec((None, tm, kq), lambda i, k: (k, i, 0))
    return _mm(name, dy, w4, (s // tm, 4), pl.BlockSpec((tm, n), lambda i, k: (i, 0)),
               pl.BlockSpec((None, None, kq, n), lambda i, k: (k, layer, 0, 0)), out_shape, out_spec, _NT,
               add=add, add_spec=out_spec if add is not None else None)


def _mm_tn_col(name, x, dh):
    s, kk = x.shape
    if dh.ndim == 3:
        nq = dh.shape[2]
        b_spec = pl.BlockSpec((None, s, nq), lambda k, j: (k, 0, 0))
    else:
        nq = dh.shape[1] // 4
        b_spec = pl.BlockSpec((s, nq), lambda k, j: (0, k))
    tk = min(256, kk)
    return _mm(name, x, dh, (4, kk // tk), pl.BlockSpec((s, tk), lambda k, j: (0, j)), b_spec, (4, kk, nq),
               pl.BlockSpec((None, tk, nq), lambda k, j: (k, j, 0)), _TN, out_dtype=bf16)


def _mm_tn_row(name, a, dy):
    s, n = dy.shape
    if a.ndim == 3:
        kq = a.shape[2]
        a_spec = pl.BlockSpec((None, s, kq), lambda k, j: (k, 0, 0))
    else:
        kq = a.shape[1] // 4
        a_spec = pl.BlockSpec((s, kq), lambda k, j: (0, k))
    tn = min(512, n)
    return _mm(name, a, dy, (4, n // tn), a_spec, pl.BlockSpec((s, tn), lambda k, j: (0, j)), (4, kq, n),
               pl.BlockSpec((None, kq, tn), lambda k, j: (k, 0, j)), _TN, out_dtype=bf16)


def _tr(s):
    return min(TR, s)


def _rows(tm, d):
    return pl.BlockSpec((tm, d), lambda i: (i, 0))


def _vec(d):
    return pl.BlockSpec((1, d), lambda i: (0, 0))


def _ln_fwd(name, x, mix, g, b):
    s, d = x.shape
    tm = _tr(s)

    def body(x_ref, m_ref, g_ref, b_ref, y_ref, xh_ref, rs_ref):
        r = ALPHA * x_ref[...] + m_ref[...]
        mu = jnp.mean(r, axis=-1, keepdims=True)
        c = r - mu
        rstd = lax.rsqrt(jnp.mean(c * c, axis=-1, keepdims=True) + LN_EPS)
        xh = c * rstd
        y_ref[...] = xh * g_ref[...] + b_ref[...]
        xh_ref[...] = xh
        rs_ref[...] = rstd

    return pl.pallas_call(
        body, grid=(s // tm,), in_specs=[_rows(tm, d), _rows(tm, d), _vec(d), _vec(d)],
        out_specs=[_rows(tm, d), _rows(tm, d), _rows(tm, 1)],
        out_shape=[_sds((s, d)), _sds((s, d)), _sds((s, 1))], name=name, compiler_params=_params())(x, mix, g, b)


def _ln_bwd_rows(dy, xh, rstd, g):
    dxh = dy * g
    m1 = jnp.mean(dxh, axis=-1, keepdims=True)
    m2 = jnp.mean(dxh * xh, axis=-1, keepdims=True)
    return rstd * (dxh - m1 - xh * m2)


def _ln_bwd(name, dy, xh, rstd, g):
    s, d = dy.shape
    tm = _tr(s)

    def body(dy_ref, xh_ref, rs_ref, g_ref, dr_ref, dg_ref, db_ref):
        @pl.when(pl.program_id(0) == 0)
        def _():
            dg_ref[...] = jnp.zeros_like(dg_ref)
            db_ref[...] = jnp.zeros_like(db_ref)

        dyv, xhv = dy_ref[...], xh_ref[...]
        dr_ref[...] = _ln_bwd_rows(dyv, xhv, rs_ref[...], g_ref[...])
        dg_ref[...] += jnp.sum(dyv * xhv, axis=0, keepdims=True)
        db_ref[...] += jnp.sum(dyv, axis=0, keepdims=True)

    return pl.pallas_call(
        body, grid=(s // tm,), in_specs=[_rows(tm, d), _rows(tm, d), _rows(tm, 1), _vec(d)],
        out_specs=[_rows(tm, d), _vec(d), _vec(d)],
        out_shape=[_sds((s, d)), _sds((1, d)), _sds((1, d))], name=name, compiler_params=_params())(dy, xh, rstd, g)


def _sm_spec(tm, w):
    return pl.BlockSpec((None, tm, w), lambda k, i: (k, i, 0))


def _swiglu_fwd(name, hg, hu):
    _, s, w = hg.shape
    tm = _tr(s)

    def body(g_ref, u_ref, a_ref):
        g = g_ref[...]
        a_ref[...] = (g * _sigmoid(g) * u_ref[...]).astype(bf16)

    return pl.pallas_call(body, grid=(4, s // tm), in_specs=[_sm_spec(tm, w)] * 2, out_specs=_sm_spec(tm, w),
                          out_shape=_sds(hg.shape, bf16), name=name, compiler_params=_params())(hg, hu)


def _swiglu_bwd(name, da, hg, hu):
    _, s, w = hg.shape
    tm = _tr(s)

    def body(da_ref, g_ref, u_ref, dg_ref, du_ref):
        g, da_v = g_ref[...], da_ref[...]
        sg = _sigmoid(g)
        du_ref[...] = (da_v * g * sg).astype(bf16)
        dg_ref[...] = (da_v * u_ref[...] * sg * (1.0 + g * (1.0 - sg))).astype(bf16)

    return pl.pallas_call(body, grid=(4, s // tm), in_specs=[_sm_spec(tm, w)] * 3, out_specs=[_sm_spec(tm, w)] * 2,
                          out_shape=[_sds(hg.shape, bf16)] * 2, name=name, compiler_params=_params())(da, hg, hu)


def _ple_fwd(name, x2, gp, bias, pp):
    s, d = x2.shape
    tm = _tr(s)

    def body(x_ref, gp_ref, b_ref, pp_ref, y_ref):
        y_ref[...] = x_ref[...] + _sigmoid(gp_ref[...] + b_ref[...]) * pp_ref[...]

    return pl.pallas_call(body, grid=(s // tm,), in_specs=[_rows(tm, d), _rows(tm, d), _vec(d), _rows(tm, d)],
                          out_specs=_rows(tm, d), out_shape=_sds((s, d)), name=name,
                          compiler_params=_params())(x2, gp, bias, pp)


def _ple_bwd(name, dy, gp, bias, pp):
    s, d = dy.shape
    tm = _tr(s)

    def body(dy_ref, gp_ref, b_ref, pp_ref, dgp_ref, dpp_ref, db_ref):
        @pl.when(pl.program_id(0) == 0)
        def _():
            db_ref[...] = jnp.zeros_like(db_ref)

        dyv = dy_ref[...]
        gate = _sigmoid(gp_ref[...] + b_ref[...])
        dgp = dyv * pp_ref[...] * gate * (1.0 - gate)
        dgp_ref[...] = dgp.astype(bf16)
        dpp_ref[...] = (dyv * gate).astype(bf16)
        db_ref[...] += jnp.sum(dgp, axis=0, keepdims=True)

    return pl.pallas_call(body, grid=(s // tm,), in_specs=[_rows(tm, d), _rows(tm, d), _vec(d), _rows(tm, d)],
                          out_specs=[_rows(tm, d), _rows(tm, d), _vec(d)],
                          out_shape=[_sds((s, d), bf16), _sds((s, d), bf16), _sds((1, d))], name=name,
                          compiler_params=_params())(dy, gp, bias, pp)


def _loss_head(name, y, target):
    s, d = y.shape
    tm = _tr(s)

    def body(y_ref, t_ref, l_ref, dy_ref):
        @pl.when(pl.program_id(0) == 0)
        def _():
            l_ref[...] = jnp.zeros_like(l_ref)

        e = y_ref[...] - t_ref[...]
        dy_ref[...] = e * (1.0 / d)
        tot = jnp.sum(jnp.sum(e * e, axis=1, keepdims=True), axis=0, keepdims=True) * (0.5 / d)
        l_ref[...] += jnp.broadcast_to(tot, l_ref.shape)

    return pl.pallas_call(body, grid=(s // tm,), in_specs=[_rows(tm, d), _rows(tm, d)],
                          out_specs=[pl.BlockSpec((1, LANES), lambda i: (0, 0)), _rows(tm, d)],
                          out_shape=[_sds((1, LANES)), _sds((s, d))], name=name, compiler_params=_params())(y, target)


def _split_dot(x, m01):
    hi = x.astype(bf16)
    lo = (x - hi.astype(f32)).astype(bf16)
    return _dot(hi, m01) + _dot(lo, m01)


def _softplus(z):
    return jnp.maximum(z, 0.0) + jnp.log(1.0 + jnp.exp(-jnp.abs(z)))


def _sba_specs(s, dh, hp):
    qb_n = Q_BLOCK
    blk = pl.BlockSpec((hp, qb_n, dh), lambda hh, i: (hh, i, 0))
    full = pl.BlockSpec((hp, s, dh), lambda hh, i: (hh, 0, 0))
    col1 = pl.BlockSpec((hp, qb_n, 1), lambda hh, i: (hh, i, 0))
    return blk, full, col1


def _sba_fwd(q, k, v, shards):
    h, s, dh = q.shape
    hp, qb_n = SBA_FWD_HEADS, Q_BLOCK
    n = len(shards)
    steps = (h // hp, s // qb_n)

    def body(q_ref, k_ref, v_ref, *rest):
        o_ref, t_ref = rest[n:n + 2]
        local, over_ici, to_sibling = _gather_halves_copies(rest[:n], rest[n + 2:2 * n + 2], *rest[2 * n + 2:])
        i = pl.program_id(1)

        @pl.when(jnp.logical_and(pl.program_id(0) == 0, i == 0))
        def _():
            for cp in local + over_ici:
                cp.start()

        qbs = [q_ref[a].astype(bf16) for a in range(hp)]
        row = lax.broadcasted_iota(jnp.int32, (qb_n, qb_n), 0)
        col = lax.broadcasted_iota(jnp.int32, (qb_n, qb_n), 1)
        later = (row >= col).astype(bf16)

        def step(n, carry):
            tails, accs = carry
            off = pl.multiple_of((i - n) * qb_n, qb_n)
            mask = col < row + jnp.minimum(n, 1) * qb_n
            new_tails, new_accs = [], []
            for a in range(hp):
                kb = k_ref[a, pl.ds(off, qb_n), :].astype(bf16)
                vb = v_ref[a, pl.ds(off, qb_n), :].astype(bf16)
                z = _dot(qbs[a], kb, _NT) * SB_SCALE
                sp = _softplus(z)
                lk = jnp.where(mask, -sp, 0.0)
                cum = _split_dot(lk, later)
                w = jnp.where(mask, jnp.exp(z - sp + cum - lk + tails[a]), 0.0)
                new_tails.append(tails[a] + cum[:, 0:1])
                new_accs.append(accs[a] + _dot(w.astype(bf16), vb))
            return tuple(new_tails), tuple(new_accs)

        init = (tuple(jnp.zeros((qb_n, 1), f32) for _ in range(hp)), tuple(jnp.zeros((qb_n, dh), f32) for _ in range(hp)))
        tails, accs = lax.fori_loop(0, i + 1, step, init)
        for a in range(hp):
            o_ref[a] = accs[a]
            t_ref[a] = tails[a]

        @pl.when(jnp.logical_and(pl.program_id(0) == steps[0] - 1, i == steps[1] - 1))
        def _():
            for arrived, onward in zip(over_ici, to_sibling):
                arrived.wait_recv()
                onward.start()
            for cp in over_ici:
                cp.wait_send()
            for cp in to_sibling + local:
                cp.wait()

    blk, full, col1 = _sba_specs(s, dh, hp)
    sems = _comm_sems(n)
    outs = pl.pallas_call(body, grid=steps, in_specs=[blk, full, full] + [_ANY] * n,
                          out_specs=[blk, col1] + [_ANY] * n,
                          out_shape=[_sds((h, s, dh)), _sds((h, s, 1))] + _gather_shapes(shards),
                          scratch_shapes=sems + sems[:2], name="sba_fwd", compiler_params=_params())(q, k, v, *shards)
    return outs[0], outs[1], outs[2:]


def _sba_bwd(q, k, v, tot, do, grads, layout):
    h, s, dh = q.shape
    hp, qb_n = SBA_BWD_HEADS, Q_BLOCK
    n_in = len(grads)
    scatter_shape = _scatter_shapes(grads, layout)
    n_out = len(scatter_shape)
    steps = (h // hp, s // qb_n)

    def body(q_ref, k_ref, v_ref, t_ref, do_ref, *rest):
        dq_ref, dk_ref, dv_ref = rest[n_in:n_in + 3]
        copies = _scatter_copies(rest[:n_in], rest[n_in + 3:n_in + 3 + n_out], layout, *rest[n_in + 3 + n_out:])
        i = pl.program_id(1)

        @pl.when(jnp.logical_and(pl.program_id(0) == 0, i == 0))
        def _():
            for cp in copies:
                cp.start()

        @pl.when(i == 0)
        def _():
            dk_ref[...] = jnp.zeros_like(dk_ref)
            dv_ref[...] = jnp.zeros_like(dv_ref)

        qbs = [q_ref[a].astype(bf16) for a in range(hp)]
        dobs = [do_ref[a].astype(bf16) for a in range(hp)]
        tots = [t_ref[a] for a in range(hp)]
        row = lax.broadcasted_iota(jnp.int32, (qb_n, qb_n), 0)
        col = lax.broadcasted_iota(jnp.int32, (qb_n, qb_n), 1)
        upto = (row <= col).astype(bf16)
        before = (row < col).astype(bf16)

        def step(j, carry):
            heads, eheads, dqs = carry
            off = pl.multiple_of(j * qb_n, qb_n)
            mask = col < row + jnp.minimum(i - j, 1) * qb_n
            new_heads, new_eheads, new_dqs = [], [], []
            for a in range(hp):
                kb = k_ref[a, pl.ds(off, qb_n), :].astype(bf16)
                vb = v_ref[a, pl.ds(off, qb_n), :].astype(bf16)
                z = _dot(qbs[a], kb, _NT) * SB_SCALE
                sp = _softplus(z)
                lk = jnp.where(mask, -sp, 0.0)
                pre = _split_dot(lk, upto)
                w = jnp.where(mask, jnp.exp(z - sp + (tots[a] - heads[a] - pre)), 0.0)
                e = _dot(dobs[a], vb, _NT) * w
                epre = eheads[a] + _split_dot(e, before)
                dz = jnp.where(mask, e * jnp.exp(-sp) - epre * jnp.exp(z - sp), 0.0) * SB_SCALE
                dzb = dz.astype(bf16)
                dk_ref[a, pl.ds(off, qb_n), :] += _dot(dzb, qbs[a], _TN)
                dv_ref[a, pl.ds(off, qb_n), :] += _dot(w.astype(bf16), dobs[a], _TN)
                new_heads.append(heads[a] + pre[:, qb_n - 1:qb_n])
                new_eheads.append(eheads[a] + jnp.sum(e, axis=1, keepdims=True))
                new_dqs.append(dqs[a] + _dot(dzb, kb))
            return tuple(new_heads), tuple(new_eheads), tuple(new_dqs)

        zeros = tuple(jnp.zeros((qb_n, 1), f32) for _ in range(hp))
        _, _, dqs = lax.fori_loop(0, i + 1, step, (zeros, zeros, tuple(jnp.zeros((qb_n, dh), f32) for _ in range(hp))))
        for a in range(hp):
            dq_ref[a] = dqs[a]

        @pl.when(jnp.logical_and(pl.program_id(0) == steps[0] - 1, i == steps[1] - 1))
        def _():
            for cp in copies:
                cp.wait()

    blk, full, col1 = _sba_specs(s, dh, hp)
    outs = pl.pallas_call(body, grid=steps, in_specs=[blk, full, full, col1, blk] + [_ANY] * n_in,
                          out_specs=[blk, full, full] + [_ANY] * n_out,
                          out_shape=[_sds((h, s, dh))] * 3 + scatter_shape, scratch_shapes=_comm_sems(n_in),
                          name="sba_bwd", compiler_params=_params())(q, k, v, tot, do, *grads)
    return outs[0], outs[1], outs[2], outs[3:]


def _pool_fwd(hsm, pool_w, pool_scale):
    _, s, wd = hsm.shape
    ch = min(256, s)

    def body(u_ref, w_ref, sc_ref, b_ref, pooled_ref, pad_ref):
        pad_ref[0:POOL_HALO, :] = jnp.zeros((POOL_HALO, wd), f32)
        pad_ref[POOL_HALO:POOL_HALO + s, :] = u_ref[...]
        for g, win in enumerate(POOL_WINDOWS):
            cols = slice(g * GROUP_DIM, (g + 1) * GROUP_DIM)
            wg = w_ref[g].astype(bf16)
            for r0 in range(0, s, ch):
                acc = pad_ref[POOL_HALO + r0:POOL_HALO + r0 + ch, cols]
                own = acc
                for dlt in range(1, win):
                    acc = acc + pad_ref[POOL_HALO + r0 - dlt:POOL_HALO + r0 - dlt + ch, cols]
                t = r0 + lax.broadcasted_iota(jnp.int32, (ch, 1), 0)
                cnt = jnp.minimum(t + 1, win).astype(f32)
                pooled = acc / cnt - own
                pooled_ref[r0:r0 + ch, cols] = pooled
                b_ref[r0:r0 + ch, cols] = _dot(pooled.astype(bf16), wg) * sc_ref[:, cols]

    return pl.pallas_call(
        body, grid=(1,),
        in_specs=[pl.BlockSpec((None, s, wd), lambda i: (3, 0, 0)), pl.BlockSpec(pool_w.shape, lambda i: (0, 0, 0)),
                  _vec(wd)],
        out_specs=[pl.BlockSpec((s, wd), lambda i: (0, 0))] * 2, out_shape=[_sds((s, wd))] * 2,
        scratch_shapes=[pltpu.VMEM((POOL_HALO + s, wd), f32)], name="pool_fwd",
        compiler_params=_params())(hsm, pool_w, pool_scale)


def _pool_bwd(dcat, pooled, pool_w, pool_scale):
    s, wd = pooled.shape
    ch = min(256, s)

    def body(db_ref, p_ref, w_ref, sc_ref, du_ref, dw_ref, dsc_ref, pad_ref):
        pad_ref[s:s + POOL_HALO, :] = jnp.zeros((POOL_HALO, wd), f32)
        for g, win in enumerate(POOL_WINDOWS):
            cols = slice(g * GROUP_DIM, (g + 1) * GROUP_DIM)
            wg = w_ref[g].astype(bf16)
            pooled_g = p_ref[:, cols].astype(bf16)
            db = db_ref[:, cols]
            dmixed = (db * sc_ref[:, cols]).astype(bf16)
            dsc_ref[:, cols] = jnp.sum(db * _dot(pooled_g, wg), axis=0, keepdims=True)
            dw_ref[g] = _dot(pooled_g, dmixed, _TN)
            dpooled = _dot(dmixed, wg, _NT)
            t = lax.broadcasted_iota(jnp.int32, (s, 1), 0)
            pad_ref[0:s, cols] = dpooled / jnp.minimum(t + 1, win).astype(f32)
            for r0 in range(0, s, ch):
                acc = pad_ref[r0:r0 + ch, cols]
                for dlt in range(1, win):
                    acc = acc + pad_ref[r0 + dlt:r0 + dlt + ch, cols]
                du_ref[r0:r0 + ch, cols] = acc - dpooled[r0:r0 + ch]

    return pl.pallas_call(
        body, grid=(1,),
        in_specs=[pl.BlockSpec((s, wd), lambda i: (0, 1)), pl.BlockSpec((s, wd), lambda i: (0, 0)),
                  pl.BlockSpec(pool_w.shape, lambda i: (0, 0, 0)), _vec(wd)],
        out_specs=[pl.BlockSpec((s, wd), lambda i: (0, 0)), pl.BlockSpec(pool_w.shape, lambda i: (0, 0, 0)), _vec(wd)],
        out_shape=[_sds((s, wd)), _sds(pool_w.shape), _sds((1, wd))],
        scratch_shapes=[pltpu.VMEM((s + POOL_HALO, wd), f32)], name="pool_bwd",
        compiler_params=_params())(dcat, pooled, pool_w, pool_scale)


def _conv_fwd(hsm, dw, ln_g, ln_b):
    _, s, wd = hsm.shape
    rows, halo = CONV_ROWS, CONV_HALO

    def body(a_ref, g_ref, dw_ref, lg_ref, lb_ref, out_ref, hc_ref, xh_ref, rs_ref, pad_ref):
        hc = a_ref[...] * _sigmoid(g_ref[...])
        hc_ref[...] = hc
        pad_ref[0:halo, :] = jnp.zeros((halo, wd), f32)
        pad_ref[halo:halo + s, :] = hc
        taps = dw_ref[...]

        def chunk(c, _):
            base = pl.multiple_of(c * rows, rows)
            win = pad_ref[pl.ds(base, rows + halo), :]
            y = jnp.zeros((rows, wd), f32)
            for k in range(CONV_TAPS):
                lo = halo - (CONV_TAPS - 1) + k
                y = y + taps[k:k + 1, :] * win[lo:lo + rows]
            mu = jnp.mean(y, axis=-1, keepdims=True)
            cen = y - mu
            rstd = lax.rsqrt(jnp.mean(cen * cen, axis=-1, keepdims=True) + LN_EPS)
            xh = cen * rstd
            n = xh * lg_ref[...] + lb_ref[...]
            out_ref[pl.ds(base, rows), :] = n * _sigmoid(n)
            xh_ref[pl.ds(base, rows), :] = xh
            rs_ref[pl.ds(base, rows), :] = rstd
            return 0

        lax.fori_loop(0, s // rows, chunk, 0)

    full = pl.BlockSpec((s, wd), lambda i: (0, 0))
    return pl.pallas_call(
        body, grid=(1,),
        in_specs=[pl.BlockSpec((None, s, wd), lambda i: (0, 0, 0)), pl.BlockSpec((None, s, wd), lambda i: (1, 0, 0)),
                  pl.BlockSpec(dw.shape, lambda i: (0, 0)), _vec(wd), _vec(wd)],
        out_specs=[full, full, full, pl.BlockSpec((s, 1), lambda i: (0, 0))],
        out_shape=[_sds((s, wd))] * 3 + [_sds((s, 1))],
        scratch_shapes=[pltpu.VMEM((halo + s, wd), f32)], name="conv_fwd",
        compiler_params=_params())(hsm, hsm, dw, ln_g, ln_b)


def _conv_bwd(dcat, hsm, hc, xh, rstd, dw, ln_g, ln_b):
    s, wd = hc.shape
    rows, halo = CONV_ROWS, CONV_HALO

    def body(dc_ref, a_ref, g_ref, hc_ref, xh_ref, rs_ref, dw_ref, lg_ref, lb_ref,
             da_ref, dg_ref, ddw_ref, dlg_ref, dlb_ref, hpad_ref, ypad_ref):
        hpad_ref[0:halo, :] = jnp.zeros((halo, wd), f32)
        hpad_ref[halo:halo + s, :] = hc_ref[...]
        ypad_ref[s:s + halo, :] = jnp.zeros((halo, wd), f32)
        ddw_ref[...] = jnp.zeros_like(ddw_ref)
        dlg_ref[...] = jnp.zeros_like(dlg_ref)
        dlb_ref[...] = jnp.zeros_like(dlb_ref)
        taps = dw_ref[...]

        def norm_bwd(c, _):
            base = pl.multiple_of(c * rows, rows)
            xhv = xh_ref[pl.ds(base, rows), :]
            n = xhv * lg_ref[...] + lb_ref[...]
            sn = _sigmoid(n)
            dn = dc_ref[pl.ds(base, rows), :] * sn * (1.0 + n * (1.0 - sn))
            dlg_ref[...] += jnp.sum(dn * xhv, axis=0, keepdims=True)
            dlb_ref[...] += jnp.sum(dn, axis=0, keepdims=True)
            ypad_ref[pl.ds(base, rows), :] = _ln_bwd_rows(dn, xhv, rs_ref[pl.ds(base, rows), :], lg_ref[...])
            return 0

        lax.fori_loop(0, s // rows, norm_bwd, 0)

        def conv_bwd(c, _):
            base = pl.multiple_of(c * rows, rows)
            ywin = ypad_ref[pl.ds(base, rows + halo), :]
            hwin = hpad_ref[pl.ds(base, rows + halo), :]
            dy = ywin[0:rows]
            dhc = jnp.zeros((rows, wd), f32)
            for k in range(CONV_TAPS):
                fwd = CONV_TAPS - 1 - k
                dhc = dhc + taps[k:k + 1, :] * ywin[fwd:fwd + rows]
                lo = halo - (CONV_TAPS - 1) + k
                ddw_ref[k:k + 1, :] += jnp.sum(dy * hwin[lo:lo + rows], axis=0, keepdims=True)
            sg = _sigmoid(g_ref[pl.ds(base, rows), :])
            da_ref[pl.ds(base, rows), :] = dhc * sg
            dg_ref[pl.ds(base, rows), :] = dhc * a_ref[pl.ds(base, rows), :] * sg * (1.0 - sg)
            return 0

        lax.fori_loop(0, s // rows, conv_bwd, 0)

    full = pl.BlockSpec((s, wd), lambda i: (0, 0))
    tap_spec = pl.BlockSpec(dw.shape, lambda i: (0, 0))
    return pl.pallas_call(
        body, grid=(1,),
        in_specs=[full, pl.BlockSpec((None, s, wd), lambda i: (0, 0, 0)), pl.BlockSpec((None, s, wd), lambda i: (1, 0, 0)),
                  full, full, pl.BlockSpec((s, 1), lambda i: (0, 0)), tap_spec, _vec(wd), _vec(wd)],
        out_specs=[full, full, tap_spec, _vec(wd), _vec(wd)],
        out_shape=[_sds((s, wd)), _sds((s, wd)), _sds(dw.shape), _sds((1, wd)), _sds((1, wd))],
        scratch_shapes=[pltpu.VMEM((halo + s, wd), f32), pltpu.VMEM((s + halo, wd), f32)], name="conv_bwd",
        compiler_params=_params())(dcat, hsm, hsm, hc, xh, rstd, dw, ln_g, ln_b)


_GELU_C = 0.7978845608028654
_GELU_A = 0.044715


def _gelu(x):
    return 0.5 * x * (1.0 + jnp.tanh(_GELU_C * (x + _GELU_A * x * x * x)))


def _gelu_grad(x):
    th = jnp.tanh(_GELU_C * (x + _GELU_A * x * x * x))
    return 0.5 * (1.0 + th) + 0.5 * x * (1.0 - th * th) * _GELU_C * (1.0 + 3.0 * _GELU_A * x * x)


def _causal_sg_w(w_ref, g):
    row = lax.broadcasted_iota(jnp.int32, (SG_CHUNK, SG_CHUNK), 0)
    col = lax.broadcasted_iota(jnp.int32, (SG_CHUNK, SG_CHUNK), 1)
    return jnp.where(col <= row, w_ref[g], 0.0).astype(bf16), col <= row


def _gmlp_fwd(hsm, ln_g, ln_b, sg_w, sg_bt):
    _, s, wd = hsm.shape
    ck = SG_CHUNK

    def body(zu_ref, zv_ref, lg_ref, lb_ref, w_ref, bt_ref, out_ref, xh_ref, rs_ref):
        u = _gelu(zu_ref[...])
        vg = _gelu(zv_ref[...])
        mu = jnp.mean(vg, axis=-1, keepdims=True)
        cen = vg - mu
        rstd = lax.rsqrt(jnp.mean(cen * cen, axis=-1, keepdims=True) + LN_EPS)
        xh = cen * rstd
        xh_ref[...] = xh
        rs_ref[...] = rstd
        vn = (xh * lg_ref[...] + lb_ref[...]).astype(bf16)
        for g in range(4):
            cols = slice(g * GROUP_DIM, (g + 1) * GROUP_DIM)
            wm, _ = _causal_sg_w(w_ref, g)
            sv = _dot(wm, vn[:, cols]) + bt_ref[:, g:g + 1]
            out_ref[:, cols] = u[:, cols] * sv

    rows_spec = pl.BlockSpec((ck, wd), lambda i: (i, 0))
    return pl.pallas_call(
        body, grid=(s // ck,),
        in_specs=[pl.BlockSpec((None, ck, wd), lambda i: (2, i, 0)), pl.BlockSpec((None, ck, wd), lambda i: (3, i, 0)),
                  _vec(wd), _vec(wd), pl.BlockSpec(sg_w.shape, lambda i: (0, 0, 0)),
                  pl.BlockSpec(sg_bt.shape, lambda i: (0, 0))],
        out_specs=[rows_spec, rows_spec, pl.BlockSpec((ck, 1), lambda i: (i, 0))],
        out_shape=[_sds((s, wd)), _sds((s, wd)), _sds((s, 1))], name="gmlp_fwd",
        compiler_params=_params())(hsm, hsm, ln_g, ln_b, sg_w, sg_bt)


def _gmlp_bwd(dcat, hsm, xh, rstd, ln_g, ln_b, sg_w, sg_bt):
    s, wd = xh.shape
    ck = SG_CHUNK

    def body(dd_ref, zu_ref, zv_ref, xh_ref, rs_ref, lg_ref, lb_ref, w_ref, bt_ref,
             dzu_ref, dzv_ref, dw_ref, dbb_ref, dlg_ref, dlb_ref):
        @pl.when(pl.program_id(0) == 0)
        def _():
            dw_ref[...] = jnp.zeros_like(dw_ref)
            dbb_ref[...] = jnp.zeros_like(dbb_ref)
            dlg_ref[...] = jnp.zeros_like(dlg_ref)
            dlb_ref[...] = jnp.zeros_like(dlb_ref)

        zu, zv, dd, xhv = zu_ref[...], zv_ref[...], dd_ref[...], xh_ref[...]
        u = _gelu(zu)
        vn = (xhv * lg_ref[...] + lb_ref[...]).astype(bf16)
        du_parts, dvn_parts = [], []
        for g in range(4):
            cols = slice(g * GROUP_DIM, (g + 1) * GROUP_DIM)
            wm, keep = _causal_sg_w(w_ref, g)
            sv = _dot(wm, vn[:, cols]) + bt_ref[:, g:g + 1]
            du_parts.append(dd[:, cols] * sv)
            dsv = dd[:, cols] * u[:, cols]
            dsvb = dsv.astype(bf16)
            dbb_ref[g] += jnp.broadcast_to(jnp.sum(dsv, axis=1, keepdims=True), (ck, GROUP_DIM))
            dw_ref[g] += jnp.where(keep, _dot(dsvb, vn[:, cols], _NT), 0.0)
            dvn_parts.append(_dot(wm, dsvb, _TN))
        du = jnp.concatenate(du_parts, axis=1)
        dvn = jnp.concatenate(dvn_parts, axis=1)
        dlg_ref[...] += jnp.sum(dvn * xhv, axis=0, keepdims=True)
        dlb_ref[...] += jnp.sum(dvn, axis=0, keepdims=True)
        dzv_ref[...] = _ln_bwd_rows(dvn, xhv, rs_ref[...], lg_ref[...]) * _gelu_grad(zv)
        dzu_ref[...] = du * _gelu_grad(zu)

    rows_spec = pl.BlockSpec((ck, wd), lambda i: (i, 0))
    wspec = pl.BlockSpec(sg_w.shape, lambda i: (0, 0, 0))
    return pl.pallas_call(
        body, grid=(s // ck,),
        in_specs=[pl.BlockSpec((ck, wd), lambda i: (i, 1)), pl.BlockSpec((None, ck, wd), lambda i: (2, i, 0)),
                  pl.BlockSpec((None, ck, wd), lambda i: (3, i, 0)), rows_spec, pl.BlockSpec((ck, 1), lambda i: (i, 0)),
                  _vec(wd), _vec(wd), wspec, pl.BlockSpec(sg_bt.shape, lambda i: (0, 0))],
        out_specs=[rows_spec, rows_spec, wspec, wspec, _vec(wd), _vec(wd)],
        out_shape=[_sds((s, wd)), _sds((s, wd)), _sds(sg_w.shape), _sds(sg_w.shape), _sds((1, wd)), _sds((1, wd))],
        name="gmlp_bwd", compiler_params=_params())(dcat, hsm, hsm, xh, rstd, ln_g, ln_b, sg_w, sg_bt)


def _to_heads(x2d):
    s = x2d.shape[0]
    return jnp.transpose(x2d.reshape(s, N_HEADS, HEAD_DIM), (1, 0, 2)).astype(bf16)


def _from_heads(x3d):
    s = x3d.shape[1]
    return jnp.transpose(x3d, (1, 0, 2)).reshape(s, N_HEADS * HEAD_DIM)


def _local_step(x, p, target, wts, late_shards, small):
    wts = dict(wts)
    saved = []
    for i in range(DEPTH):
        tag = f"l{i}"
        if i % 2 == 0:
            hsm = _mm_nn_col(tag + "_in", x, wts["even_w_in"], 0)
            q, k, v = _to_heads(hsm[0]), _to_heads(hsm[1]), _to_heads(hsm[2])
            att_heads, sba_tot, late = _sba_fwd(q, k, v, late_shards)
            wts.update(zip(_LATE, late))
            att = _from_heads(att_heads)
            pool_out, pooled = _pool_fwd(hsm, small["pool_w"], small["pool_scale"])
            cat = jnp.concatenate([att, pool_out], axis=1).astype(bf16)
            mix = _mm_nn_row(tag + "_out", cat, wts["even_w_out"], 0)
            mixer_saved = (hsm, q, k, v, sba_tot, pooled, cat)
        else:
            hsm = _mm_nn_col(tag + "_in", x, wts["odd_w_in"], 0)
            conv_out, hc, cxh, crs = _conv_fwd(hsm, small["conv_dw"], small["conv_ln_g"], small["conv_ln_b"])
            sg_out, sxh, srs = _gmlp_fwd(hsm, small["sg_ln_g"], small["sg_ln_b"], small["sg_w"], small["sg_bt"])
            cat = jnp.concatenate([conv_out, sg_out], axis=1).astype(bf16)
            mix = _mm_nn_row(tag + "_out", cat, wts["odd_w_out"], 0)
            mixer_saved = (hsm, hc, cxh, crs, sxh, srs, cat)
        x1, xh1, rs1 = _ln_fwd(tag + "_ln_mix", x, mix, small["ln_mix_g"][i:i + 1], small["ln_mix_b"][i:i + 1])
        hg = _mm_nn_col(tag + "_gate", x1, wts["ffn_w_gate"], i)
        hu = _mm_nn_col(tag + "_up", x1, wts["ffn_w_up"], i)
        act = _swiglu_fwd(tag + "_swiglu", hg, hu)
        ffn = _mm_nn_row(tag + "_down", act, wts["ffn_w_down"], i)
        x2, xh2, rs2 = _ln_fwd(tag + "_ln_ffn", x1, ffn, small["ln_ffn_g"][i:i + 1], small["ln_ffn_b"][i:i + 1])
        gp = _mm_nn_row(tag + "_ple_gate", x2, wts["ple_w_gate"], i)
        pp = _mm_nn_col(tag + "_ple_proj", p[i], wts["ple_w_proj"], i, natural=True)
        x3 = _ple_fwd(tag + "_ple", x2, gp, small["ple_b_gate"][i:i + 1], pp)
        saved.append((x, mixer_saved, x1, xh1, rs1, hg, hu, act, x2, xh2, rs2, gp, pp))
        x = x3

    loss_part, dx = _loss_head("loss_head", x, target)

    big = {n: [None] * wts[n].shape[1] for n in wts}
    received = {}
    sm = {}
    per_layer = {n: [None] * DEPTH for n in ("ln_mix_g", "ln_mix_b", "ln_ffn_g", "ln_ffn_b", "ple_b_gate")}
    for i in reversed(range(DEPTH)):
        tag = f"l{i}b"
        x0, mixer_saved, x1, xh1, rs1, hg, hu, act, x2, xh2, rs2, gp, pp = saved[i]
        dgp, dpp, per_layer["ple_b_gate"][i] = _ple_bwd(tag + "_ple", dx, gp, small["ple_b_gate"][i:i + 1], pp)
        big["ple_w_proj"][i] = _mm_tn_col(tag + "_dproj", p[i], dpp)
        big["ple_w_gate"][i] = _mm_tn_row(tag + "_dgate", x2, dgp)
        dx2 = _mm_nt_row(tag + "_dx2", dgp, wts["ple_w_gate"], i, natural=True, add=dx)
        dr2, per_layer["ln_ffn_g"][i], per_layer["ln_ffn_b"][i] = _ln_bwd(tag + "_ln_ffn", dx2, xh2, rs2,
                                                                           small["ln_ffn_g"][i:i + 1])
        dact = _mm_nt_row(tag + "_dact", dr2, wts["ffn_w_down"], i)
        big["ffn_w_down"][i] = _mm_tn_row(tag + "_ddown", act, dr2)
        dhg, dhu = _swiglu_bwd(tag + "_swiglu", dact, hg, hu)
        big["ffn_w_gate"][i] = _mm_tn_col(tag + "_dgatew", x1, dhg)
        big["ffn_w_up"][i] = _mm_tn_col(tag + "_dupw", x1, dhu)
        part = _mm_nt_col(tag + "_dx1a", dhg, wts["ffn_w_gate"], i, dr2, ALPHA)
        dx1 = _mm_nt_col(tag + "_dx1b", dhu, wts["ffn_w_up"], i, part, 1.0)
        dr1, per_layer["ln_mix_g"][i], per_layer["ln_mix_b"][i] = _ln_bwd(tag + "_ln_mix", dx1, xh1, rs1,
                                                                           small["ln_mix_g"][i:i + 1])
        if i % 2 == 0:
            hsm, q, k, v, sba_tot, pooled, cat = mixer_saved
            big["even_w_out"][0] = _mm_tn_row(tag + "_dout", cat, dr1)
            dcat = _mm_nt_row(tag + "_dcat", dr1, wts["even_w_out"], 0, natural=True)
            entries, layout = _scatter_plan(big, _LATE)
            dq, dk, dv, got = _sba_bwd(q, k, v, sba_tot, _to_heads(dcat[:, :N_HEADS * HEAD_DIM]), entries, layout)
            received.update(zip(_LATE, got))
            du, sm["pool_w"], sm["pool_scale"] = _pool_bwd(dcat, pooled, small["pool_w"], small["pool_scale"])
            dhsm = jnp.stack([_from_heads(dq), _from_heads(dk), _from_heads(dv), du]).astype(bf16)
            w_in = "even_w_in"
        else:
            hsm, hc, cxh, crs, sxh, srs, cat = mixer_saved
            big["odd_w_out"][0] = _mm_tn_row(tag + "_dout", cat, dr1)
            dcat = _mm_nt_row(tag + "_dcat", dr1, wts["odd_w_out"], 0, natural=True)
            da, dg, sm["conv_dw"], sm["conv_ln_g"], sm["conv_ln_b"] = _conv_bwd(
                dcat, hsm, hc, cxh, crs, small["conv_dw"], small["conv_ln_g"], small["conv_ln_b"])
            dzu, dzv, sm["sg_w"], dsgb, sm["sg_ln_g"], sm["sg_ln_b"] = _gmlp_bwd(
                dcat, hsm, sxh, srs, small["sg_ln_g"], small["sg_ln_b"], small["sg_w"], small["sg_bt"])
            sm["sg_b"] = dsgb[:, :, 0]
            dhsm = jnp.stack([da, dg, dzu, dzv]).astype(bf16)
            w_in = "odd_w_in"
        big[w_in][0] = _mm_tn_col(tag + "_din", x0, dhsm)
        dx = _mm_nt_col(tag + "_dx", dhsm, wts[w_in], 0, dr1, ALPHA)
    for n, parts in per_layer.items():
        sm[n] = jnp.concatenate(parts, axis=0)
    return loss_part, dx, received, _scatter_plan(big, _EARLY), sm


def _scatter_plan(big, names):
    entries, layout = [], []
    for pi, n in enumerate(names):
        for li, g in enumerate(big[n]):
            entries.append(g)
            layout.append((pi, li))
    return entries, layout


def _place():
    x, y, c = lax.axis_index("x"), lax.axis_index("y"), lax.axis_index("c")
    return x, y, c, [(1 - x, y), (x, 1 - y), (1 - x, 1 - y)]


def _gather_chips(shards):
    n = len(shards)

    def body(*refs):
        copies = _gather_copies(refs[:n], refs[n:2 * n], *refs[2 * n:])
        for cp in copies:
            cp.start()
        for cp in copies:
            cp.wait()

    return pl.pallas_call(body, in_specs=[_ANY] * n, out_specs=[_ANY] * n, out_shape=_gather_shapes(shards),
                          scratch_shapes=_comm_sems(n), name="gather_chips")(*shards)


def _comm_sems(n):
    return [pltpu.SemaphoreType.DMA((n, 3)), pltpu.SemaphoreType.DMA((n, 3)), pltpu.SemaphoreType.DMA((n,))]


def _gather_shapes(shards):
    return [_sds((N_CHIPS,) + a.shape, a.dtype) for a in shards]


def _gather_copies(ins, outs, send, recv, loc):
    x, y, c, chips = _place()
    mine = 2 * x + y
    copies = []
    for t in range(len(ins)):
        copies.append(pltpu.make_async_copy(ins[t], outs[t].at[mine], loc.at[t]))
        for j, (cx, cy) in enumerate(chips):
            copies.append(pltpu.make_async_remote_copy(
                src_ref=ins[t], dst_ref=outs[t].at[mine], send_sem=send.at[t, j], recv_sem=recv.at[t, j],
                device_id=(cx, cy, c), device_id_type=_MESH))
    return copies


def _gather_halves_copies(ins, outs, send, recv, loc, send_on, recv_on):
    x, y, c, chips = _place()
    mine = 2 * x + y
    local, over_ici, onward = [], [], []
    for t in range(len(ins)):
        nl, r, _ = ins[t].shape
        half = r // 2
        rows = pl.ds(pl.multiple_of(c * half, 32), half)
        local.append(pltpu.make_async_copy(ins[t], outs[t].at[mine], loc.at[t]))
        for j, (cx, cy) in enumerate(chips):
            over_ici.append(pltpu.make_async_remote_copy(
                src_ref=ins[t].at[pl.ds(0, nl), rows], dst_ref=outs[t].at[mine, pl.ds(0, nl), rows],
                send_sem=send.at[t, j], recv_sem=recv.at[t, j], device_id=(cx, cy, c), device_id_type=_MESH))
            landed = outs[t].at[2 * cx + cy, pl.ds(0, nl), rows]
            onward.append(pltpu.make_async_remote_copy(
                src_ref=landed, dst_ref=landed, send_sem=send_on.at[t, j], recv_sem=recv_on.at[t, j],
                device_id=(x, y, 1 - c), device_id_type=_MESH))
    return local, over_ici, onward


def _scatter_shapes(grads, layout):
    shapes = {}
    for e, (pi, li) in enumerate(layout):
        r, cdim = grads[e].shape[1:]
        shapes[pi] = (N_CHIPS, max(li + 1, shapes.get(pi, (0, 0))[1]), r, cdim)
    return [_sds(shapes[pi], grads[0].dtype) for pi in range(len(shapes))]


def _scatter_copies(ins, outs, layout, send, recv, loc):
    x, y, c, chips = _place()
    mine = 2 * x + y
    copies = []
    for e, (pi, li) in enumerate(layout):
        copies.append(pltpu.make_async_copy(ins[e].at[mine], outs[pi].at[mine, li], loc.at[e]))
        for j, (cx, cy) in enumerate(chips):
            copies.append(pltpu.make_async_remote_copy(
                src_ref=ins[e].at[2 * cx + cy], dst_ref=outs[pi].at[mine, li], send_sem=send.at[e, j],
                recv_sem=recv.at[e, j], device_id=(cx, cy, c), device_id_type=_MESH))
    return copies


def _final_exchange(grads, layout, block):
    n_in = len(grads)
    out_shape = _scatter_shapes(grads, layout)
    n_out = len(out_shape)

    def body(*refs):
        blk_in, blk_out = refs[n_in], refs[n_in + 1 + n_out]
        send, recv, loc, send_all, recv_all, loc_all = refs[n_in + n_out + 2:]
        copies = _scatter_copies(refs[:n_in], refs[n_in + 1:n_in + 1 + n_out], layout, send, recv, loc)
        x, y, c, _ = _place()
        mine = 4 * x + 2 * y + c
        copies.append(pltpu.make_async_copy(blk_in, blk_out.at[mine], loc_all))
        for m in range(1, N_DEV):
            fx, fy, fc = (m >> 2) & 1, (m >> 1) & 1, m & 1
            peer = (x + fx - 2 * x * fx, y + fy - 2 * y * fy, c + fc - 2 * c * fc)
            copies.append(pltpu.make_async_remote_copy(
                src_ref=blk_in, dst_ref=blk_out.at[mine], send_sem=send_all.at[m - 1], recv_sem=recv_all.at[m - 1],
                device_id=peer, device_id_type=_MESH))
        for cp in copies:
            cp.start()
        for cp in copies:
            cp.wait()

    outs = pl.pallas_call(
        body, in_specs=[_ANY] * (n_in + 1), out_specs=[_ANY] * (n_out + 1),
        out_shape=out_shape + [_sds((N_DEV,) + block.shape)],
        scratch_shapes=_comm_sems(n_in) + [pltpu.SemaphoreType.DMA((N_DEV - 1,)), pltpu.SemaphoreType.DMA((N_DEV - 1,)),
                                           pltpu.SemaphoreType.DMA(())],
        name="final_exchange")(*grads, block)
    return outs[:n_out], outs[n_out]


def _swap_cores(arrays):
    n = len(arrays)

    def body(*refs):
        ins, outs = refs[:n], refs[n:2 * n]
        send, recv = refs[2 * n:]
        x, y, c, _ = _place()
        started = []
        for t in range(n):
            rc = pltpu.make_async_remote_copy(src_ref=ins[t], dst_ref=outs[t], send_sem=send.at[t], recv_sem=recv.at[t],
                                              device_id=(x, y, 1 - c), device_id_type=_MESH)
            rc.start()
            started.append(rc)
        for rc in started:
            rc.wait()

    return pl.pallas_call(
        body, in_specs=[_ANY] * n, out_specs=[_ANY] * n, out_shape=[_sds(a.shape, a.dtype) for a in arrays],
        scratch_shapes=[pltpu.SemaphoreType.DMA((n,)), pltpu.SemaphoreType.DMA((n,))], name="swap_cores")(*arrays)


def _row_tile(r):
    for t in (256, 128, 64, 32, 16, 8):
        if r % t == 0:
            return t
    return r


def _sum_stack(name, stack):
    n, r, c = stack.shape
    tr = _row_tile(r)

    def body(s_ref, o_ref):
        acc = s_ref[0].astype(f32)
        for t in range(1, n):
            acc = acc + s_ref[t].astype(f32)
        o_ref[...] = acc

    return pl.pallas_call(body, grid=(r // tr,), in_specs=[pl.BlockSpec((n, tr, c), lambda i: (0, i, 0))],
                          out_specs=pl.BlockSpec((tr, c), lambda i: (i, 0)), out_shape=_sds((r, c)), name=name,
                          compiler_params=_params())(stack)


def _adamw(name, w, g_a, g_b, m, v):
    r, c = w.shape
    tr = _row_tile(r)
    two = g_b is not None
    bc1 = 1.0 - ADAM_B1 ** ADAM_STEP
    bc2 = 1.0 - ADAM_B2 ** ADAM_STEP

    def body(*refs):
        if two:
            w_ref, ga_ref, gb_ref, m_ref, v_ref, g_out, d_out, m_out, v_out = refs
            g = ga_ref[...] + gb_ref[...]
        else:
            w_ref, ga_ref, m_ref, v_ref, g_out, d_out, m_out, v_out = refs
            g = ga_ref[...]
        m_new = ADAM_B1 * m_ref[...] + (1.0 - ADAM_B1) * g
        v_new = ADAM_B2 * v_ref[...] + (1.0 - ADAM_B2) * (g * g)
        g_out[...] = g
        m_out[...] = m_new
        v_out[...] = v_new
        d_out[...] = -ADAM_LR * ((m_new / bc1) / (jnp.sqrt(v_new / bc2) + ADAM_EPS) + ADAM_WD * w_ref[...])

    spec = pl.BlockSpec((tr, c), lambda i: (i, 0))
    ins = [w, g_a] + ([g_b] if two else []) + [m, v]
    return pl.pallas_call(body, grid=(r // tr,), in_specs=[spec] * len(ins), out_specs=[spec] * 4,
                          out_shape=[_sds((r, c))] * 4, name=name, compiler_params=_params())(*ins)


_EARLY = ("even_w_in", "even_w_out")
_LATE = ("odd_w_in", "odd_w_out", "ffn_w_gate", "ffn_w_up", "ffn_w_down", "ple_w_proj", "ple_w_gate")
_BIG = _EARLY + _LATE
_SHARDED_SMALL = ("conv_dw", "conv_ln_g", "conv_ln_b", "sg_ln_g", "sg_ln_b")
_SMALL = ("pool_w", "pool_scale", "conv_dw", "conv_ln_g", "conv_ln_b", "sg_ln_g", "sg_ln_b", "sg_w", "sg_b",
          "ln_mix_g", "ln_mix_b", "ln_ffn_g", "ln_ffn_b", "ple_b_gate")
_WEIGHTS = ("even_w_in", "even_w_out", "pool_w", "pool_scale", "odd_w_in", "odd_w_out", "conv_dw", "conv_ln_g",
            "conv_ln_b", "sg_ln_g", "sg_ln_b", "sg_w", "sg_b", "ln_mix_g", "ln_mix_b", "ffn_w_gate", "ffn_w_up",
            "ffn_w_down", "ln_ffn_g", "ln_ffn_b", "ple_w_proj", "ple_w_gate", "ple_b_gate")


def _pack(arrays):
    flat = jnp.concatenate([a.reshape(-1) for a in arrays])
    pad = (-flat.shape[0]) % (256 * LANES)
    return jnp.pad(flat, (0, pad)).reshape(-1, LANES)


def _unpack(packed, shapes):
    flat = packed.reshape(-1)
    out, off = [], 0
    for shp in shapes:
        size = 1
        for dim in shp:
            size *= dim
        out.append(flat[off:off + size].reshape(shp))
        off += size
    return out


def _unshard_last(g4):
    return jnp.concatenate([g4[k] for k in range(N_CHIPS)], axis=-1)


def kernel(x, p, even_w_in, even_w_out, pool_w, pool_scale, odd_w_in, odd_w_out, conv_dw, conv_ln_g, conv_ln_b, sg_ln_g, sg_ln_b, sg_w, sg_b, ln_mix_g, ln_mix_b, ffn_w_gate, ffn_w_up, ffn_w_down, ln_ffn_g, ln_ffn_b, ple_w_proj, ple_w_gate, ple_b_gate, loss_target, m_even_w_in, m_even_w_out, m_pool_w, m_pool_scale, m_odd_w_in, m_odd_w_out, m_conv_dw, m_conv_ln_g, m_conv_ln_b, m_sg_ln_g, m_sg_ln_b, m_sg_w, m_sg_b, m_ln_mix_g, m_ln_mix_b, m_ffn_w_gate, m_ffn_w_up, m_ffn_w_down, m_ln_ffn_g, m_ln_ffn_b, m_ple_w_proj, m_ple_w_gate, m_ple_b_gate, v_even_w_in, v_even_w_out, v_pool_w, v_pool_scale, v_odd_w_in, v_odd_w_out, v_conv_dw, v_conv_ln_g, v_conv_ln_b, v_sg_ln_g, v_sg_ln_b, v_sg_w, v_sg_b, v_ln_mix_g, v_ln_mix_b, v_ffn_w_gate, v_ffn_w_up, v_ffn_w_down, v_ln_ffn_g, v_ln_ffn_b, v_ple_w_proj, v_ple_w_gate, v_ple_b_gate):
    w = dict(even_w_in=even_w_in, even_w_out=even_w_out, pool_w=pool_w, pool_scale=pool_scale, odd_w_in=odd_w_in,
             odd_w_out=odd_w_out, conv_dw=conv_dw, conv_ln_g=conv_ln_g, conv_ln_b=conv_ln_b, sg_ln_g=sg_ln_g,
             sg_ln_b=sg_ln_b, sg_w=sg_w, sg_b=sg_b, ln_mix_g=ln_mix_g, ln_mix_b=ln_mix_b, ffn_w_gate=ffn_w_gate,
             ffn_w_up=ffn_w_up, ffn_w_down=ffn_w_down, ln_ffn_g=ln_ffn_g, ln_ffn_b=ln_ffn_b, ple_w_proj=ple_w_proj,
             ple_w_gate=ple_w_gate, ple_b_gate=ple_b_gate)
    mom = dict(even_w_in=m_even_w_in, even_w_out=m_even_w_out, pool_w=m_pool_w, pool_scale=m_pool_scale,
               odd_w_in=m_odd_w_in, odd_w_out=m_odd_w_out, conv_dw=m_conv_dw, conv_ln_g=m_conv_ln_g,
               conv_ln_b=m_conv_ln_b, sg_ln_g=m_sg_ln_g, sg_ln_b=m_sg_ln_b, sg_w=m_sg_w, sg_b=m_sg_b,
               ln_mix_g=m_ln_mix_g, ln_mix_b=m_ln_mix_b, ffn_w_gate=m_ffn_w_gate, ffn_w_up=m_ffn_w_up,
               ffn_w_down=m_ffn_w_down, ln_ffn_g=m_ln_ffn_g, ln_ffn_b=m_ln_ffn_b, ple_w_proj=m_ple_w_proj,
               ple_w_gate=m_ple_w_gate, ple_b_gate=m_ple_b_gate)
    var = dict(even_w_in=v_even_w_in, even_w_out=v_even_w_out, pool_w=v_pool_w, pool_scale=v_pool_scale,
               odd_w_in=v_odd_w_in, odd_w_out=v_odd_w_out, conv_dw=v_conv_dw, conv_ln_g=v_conv_ln_g,
               conv_ln_b=v_conv_ln_b, sg_ln_g=v_sg_ln_g, sg_ln_b=v_sg_ln_b, sg_w=v_sg_w, sg_b=v_sg_b,
               ln_mix_g=v_ln_mix_g, ln_mix_b=v_ln_mix_b, ffn_w_gate=v_ffn_w_gate, ffn_w_up=v_ffn_w_up,
               ffn_w_down=v_ffn_w_down, ln_ffn_g=v_ln_ffn_g, ln_ffn_b=v_ln_ffn_b, ple_w_proj=v_ple_w_proj,
               ple_w_gate=v_ple_w_gate, ple_b_gate=v_ple_b_gate)

    gathered = _gather_chips([w[n].astype(bf16) for n in _EARLY] + [w[n] for n in _SHARDED_SMALL])
    wts = dict(zip(_EARLY, gathered[:len(_EARLY)]))
    small = {n: w[n][0] for n in ("pool_w", "sg_w")}
    small.update({n: w[n] for n in ("pool_scale", "ln_mix_g", "ln_mix_b", "ln_ffn_g", "ln_ffn_b", "ple_b_gate")})
    small["sg_bt"] = jnp.transpose(w["sg_b"][0])
    for n, g4 in zip(_SHARDED_SMALL, gathered[len(_EARLY):]):
        small[n] = _unshard_last(g4)[0]
        if n != "conv_dw":
            small[n] = small[n][None]

    loss_part, grad_x, received, last_plan, sm = _local_step(x[0], p[:, 0], loss_target[0], wts,
                                                             [w[n].astype(bf16) for n in _LATE], small)

    sm_shapes = [(1,) + sm[n].shape if n in ("pool_w", "sg_w", "conv_dw", "sg_b") else sm[n].shape for n in _SMALL]
    got, small_stack = _final_exchange(*last_plan, _pack([sm[n] for n in _SMALL] + [loss_part[0, 0:1]]))
    received.update(zip(_EARLY, got))

    chip_sums = []
    for n in _BIG:
        _, nl, r, c = received[n].shape
        chip_sums.append(_sum_stack("sum_" + n, received[n].reshape(N_CHIPS, nl * r, c)))
    other = _swap_cores(chip_sums)
    results = {}
    for n, mine, theirs in zip(_BIG, chip_sums, other):
        shp = w[n].shape
        flat = (shp[0] * shp[1], shp[2])
        outs = _adamw("adamw_" + n, w[n].reshape(flat), mine, theirs, mom[n].reshape(flat), var[n].reshape(flat))
        results[n] = [o.reshape(shp) for o in outs]

    total = _sum_stack("sum_small", small_stack)
    parts = _unpack(total, sm_shapes + [(1,)])
    loss = parts[-1][0]
    chip = 2 * lax.axis_index("x") + lax.axis_index("y")
    g_small = {}
    for n, g in zip(_SMALL, parts[:-1]):
        if n in _SHARDED_SMALL:
            width = w[n].shape[-1]
            g = lax.dynamic_slice_in_dim(g, chip * width, width, axis=g.ndim - 1)
        g_small[n] = g
    shapes = [w[n].shape for n in _SMALL]
    outs = _adamw("adamw_small", _pack([w[n] for n in _SMALL]), _pack([g_small[n] for n in _SMALL]), None,
                  _pack([mom[n] for n in _SMALL]), _pack([var[n] for n in _SMALL]))
    unpacked = [_unpack(o, shapes) for o in outs]
    for idx, n in enumerate(_SMALL):
        results[n] = [u[idx] for u in unpacked]

    return (loss, grad_x[None], *[results[n][0] for n in _WEIGHTS], *[results[n][1] for n in _WEIGHTS],
            *[results[n][2] for n in _WEIGHTS], *[results[n][3] for n in _WEIGHTS])
```

```python
import jax
import jax.numpy as jnp
from jax import lax
from jax.experimental import pallas as pl
from jax.experimental.pallas import tpu as pltpu

f32 = jnp.float32
bf16 = jnp.bfloat16

D_MODEL = 1024
N_HEADS = 8
HEAD_DIM = 64
Q_BLOCK = 128
POOL_WINDOWS = (2, 4, 8, 16)
GROUP_DIM = 128
CONV_TAPS = 31
SG_CHUNK = 128
DEPTH = 2
ALPHA = (2 * DEPTH) ** 0.25
LN_EPS = 1e-5
SB_SCALE = HEAD_DIM ** -0.5
ADAM_LR, ADAM_B1, ADAM_B2, ADAM_EPS, ADAM_WD, ADAM_STEP = 0.001, 0.9, 0.999, 1e-08, 0.01, 10
N_CHIPS = 4
N_DEV = 8
LANES = 128
VMEM_LIMIT = 56 * 1024 * 1024
TM = 2048
TN_COLS = 512
TR = 512
SBA_FWD_HEADS = 8
SBA_BWD_HEADS = 8
CONV_ROWS = 64
CONV_HALO = 32
POOL_HALO = 16

_NN = (((1,), (0,)), ((), ()))
_NT = (((1,), (1,)), ((), ()))
_TN = (((0,), (0,)), ((), ()))
_ANY = pl.BlockSpec(memory_space=pl.ANY)
_MESH = pl.DeviceIdType.MESH


def _params():
    return pltpu.CompilerParams(vmem_limit_bytes=VMEM_LIMIT)


def _sds(shape, dtype=f32):
    return jax.ShapeDtypeStruct(tuple(shape), dtype)


def _dot(a, b, dims=_NN):
    return lax.dot_general(a, b, dims, preferred_element_type=f32)


def _sigmoid(x):
    return 1.0 / (1.0 + jnp.exp(-x))


def _mm(name, a, b, grid, a_spec, b_spec, out_shape, out_spec, dims, reduce=False, add=None, add_spec=None,
        add_scale=1.0, out_dtype=f32):
    has_add = add is not None
    k_axis = len(grid) - 1

    def body(*refs):
        if has_add:
            a_ref, b_ref, add_ref, o_ref = refs
        else:
            a_ref, b_ref, o_ref = refs
        r = _dot(a_ref[...].astype(bf16), b_ref[...].astype(bf16), dims)
        if reduce:
            k = pl.program_id(k_axis)

            @pl.when(k == 0)
            def _():
                o_ref[...] = r + add_scale * add_ref[...] if has_add else r

            @pl.when(k > 0)
            def _():
                o_ref[...] += r
        else:
            o_ref[...] = (r + add_scale * add_ref[...] if has_add else r).astype(out_dtype)

    ins = [a, b] + ([add] if has_add else [])
    specs = [a_spec, b_spec] + ([add_spec] if has_add else [])
    return pl.pallas_call(body, grid=grid, in_specs=specs, out_specs=out_spec, out_shape=_sds(out_shape, out_dtype),
                          name=name, compiler_params=_params())(*ins)


def _tm(s):
    return min(TM, s)


def _act_spec(a, tm, width):
    if a.ndim == 3:
        return pl.BlockSpec((None, tm, width), lambda i, k: (k, i, 0))
    return pl.BlockSpec((tm, width), lambda i, k: (i, k))


def _mm_nn_col(name, x, w4, layer, natural=False):
    s, kk = x.shape
    nq = w4.shape[3]
    tm = _tm(s)
    if natural:
        out_shape, out_spec = (s, 4 * nq), pl.BlockSpec((tm, nq), lambda i, k: (i, k))
    else:
        out_shape, out_spec = (4, s, nq), pl.BlockSpec((None, tm, nq), lambda i, k: (k, i, 0))
    return _mm(name, x, w4, (s // tm, 4), pl.BlockSpec((tm, kk), lambda i, k: (i, 0)),
               pl.BlockSpec((None, None, kk, nq), lambda i, k: (k, layer, 0, 0)), out_shape, out_spec, _NN)


def _mm_nn_row(name, a, w4, layer):
    s = a.shape[-2]
    kq, n = w4.shape[2], w4.shape[3]
    tm = _tm(s)
    return _mm(name, a, w4, (s // tm, 4), _act_spec(a, tm, kq),
               pl.BlockSpec((None, None, kq, n), lambda i, k: (k, layer, 0, 0)), (s, n),
               pl.BlockSpec((tm, n), lambda i, k: (i, 0)), _NN, reduce=True)


def _mm_nt_col(name, dh, w4, layer, add, add_scale):
    s = dh.shape[-2]
    kk, nq = w4.shape[2], w4.shape[3]
    tm = _tm(s)
    row = pl.BlockSpec((tm, kk), lambda i, k: (i, 0))
    return _mm(name, dh, w4, (s // tm, 4), _act_spec(dh, tm, nq),
               pl.BlockSpec((None, None, kk, nq), lambda i, k: (k, layer, 0, 0)), (s, kk), row, _NT,
               reduce=True, add=add, add_spec=row, add_scale=add_scale)


def _mm_nt_row(name, dy, w4, layer, natural=False, add=None):
    s, n = dy.shape
    kq = w4.shape[2]
    tm = _tm(s)
    if natural:
        out_shape, out_spec = (s, 4 * kq), pl.BlockSpec((tm, kq), lambda i, k: (i, k))
    else:
        out_shape, out_spec = (4, s, kq), pl.BlockSpec((None, tm, kq), lambda i, k: (k, i, 0))
    return _mm(name, dy, w4, (s // tm, 4), pl.BlockSpec((tm, n), lambda i, k: (i, 0)),
               pl.BlockSpec((None, None, kq, n), lambda i, k: (k, layer, 0, 0)), out_shape, out_spec, _NT,
               add=add, add_spec=out_spec if add is not None else None)


def _mm_tn_col(name, x, dh):
    s, kk = x.shape
    if dh.ndim == 3:
        nq = dh.shape[2]
        b_spec = pl.BlockSpec((None, s, nq), lambda k, j: (k, 0, 0))
    else:
        nq = dh.shape[1] // 4
        b_spec = pl.BlockSpec((s, nq), lambda k, j: (0, k))
    tk = min(TN_COLS, kk)
    return _mm(name, x, dh, (4, kk // tk), pl.BlockSpec((s, tk), lambda k, j: (0, j)), b_spec, (4, kk, nq),
               pl.BlockSpec((None, tk, nq), lambda k, j: (k, j, 0)), _TN, out_dtype=bf16)


def _mm_tn_row(name, a, dy):
    s, n = dy.shape
    if a.ndim == 3:
        kq = a.shape[2]
        a_spec = pl.BlockSpec((None, s, kq), lambda k, j: (k, 0, 0))
    else:
        kq = a.shape[1] // 4
        a_spec = pl.BlockSpec((s, kq), lambda k, j: (0, k))
    tn = min(2 * TN_COLS, n)
    return _mm(name, a, dy, (4, n // tn), a_spec, pl.BlockSpec((s, tn), lambda k, j: (0, j)), (4, kq, n),
               pl.BlockSpec((None, kq, tn), lambda k, j: (k, 0, j)), _TN, out_dtype=bf16)


def _tr(s):
    return min(TR, s)


def _rows(tm, d):
    return pl.BlockSpec((tm, d), lambda i: (i, 0))


def _vec(d):
    return pl.BlockSpec((1, d), lambda i: (0, 0))


def _ln_fwd(name, x, mix, g, b):
    s, d = x.shape
    tm = _tr(s)

    def body(x_ref, m_ref, g_ref, b_ref, y_ref, xh_ref, rs_ref):
        r = ALPHA * x_ref[...] + m_ref[...]
        mu = jnp.mean(r, axis=-1, keepdims=True)
        c = r - mu
        rstd = lax.rsqrt(jnp.mean(c * c, axis=-1, keepdims=True) + LN_EPS)
        xh = c * rstd
        y_ref[...] = xh * g_ref[...] + b_ref[...]
        xh_ref[...] = xh
        rs_ref[...] = rstd

    return pl.pallas_call(
        body, grid=(s // tm,), in_specs=[_rows(tm, d), _rows(tm, d), _vec(d), _vec(d)],
        out_specs=[_rows(tm, d), _rows(tm, d), _rows(tm, 1)],
        out_shape=[_sds((s, d)), _sds((s, d)), _sds((s, 1))], name=name, compiler_params=_params())(x, mix, g, b)


def _ln_bwd_rows(dy, xh, rstd, g):
    dxh = dy * g
    m1 = jnp.mean(dxh, axis=-1, keepdims=True)
    m2 = jnp.mean(dxh * xh, axis=-1, keepdims=True)
    return rstd * (dxh - m1 - xh * m2)


def _ln_bwd(name, dy, xh, rstd, g):
    s, d = dy.shape
    tm = _tr(s)

    def body(dy_ref, xh_ref, rs_ref, g_ref, dr_ref, dg_ref, db_ref):
        @pl.when(pl.program_id(0) == 0)
        def _():
            dg_ref[...] = jnp.zeros_like(dg_ref)
            db_ref[...] = jnp.zeros_like(db_ref)

        dyv, xhv = dy_ref[...], xh_ref[...]
        dr_ref[...] = _ln_bwd_rows(dyv, xhv, rs_ref[...], g_ref[...])
        dg_ref[...] += jnp.sum(dyv * xhv, axis=0, keepdims=True)
        db_ref[...] += jnp.sum(dyv, axis=0, keepdims=True)

    return pl.pallas_call(
        body, grid=(s // tm,), in_specs=[_rows(tm, d), _rows(tm, d), _rows(tm, 1), _vec(d)],
        out_specs=[_rows(tm, d), _vec(d), _vec(d)],
        out_shape=[_sds((s, d)), _sds((1, d)), _sds((1, d))], name=name, compiler_params=_params())(dy, xh, rstd, g)


def _sm_spec(tm, w):
    return pl.BlockSpec((None, tm, w), lambda k, i: (k, i, 0))


def _swiglu_fwd(name, hg, hu):
    _, s, w = hg.shape
    tm = _tr(s)

    def body(g_ref, u_ref, a_ref):
        g = g_ref[...]
        a_ref[...] = (g * _sigmoid(g) * u_ref[...]).astype(bf16)

    return pl.pallas_call(body, grid=(4, s // tm), in_specs=[_sm_spec(tm, w)] * 2, out_specs=_sm_spec(tm, w),
                          out_shape=_sds(hg.shape, bf16), name=name, compiler_params=_params())(hg, hu)


def _swiglu_bwd(name, da, hg, hu):
    _, s, w = hg.shape
    tm = _tr(s)

    def body(da_ref, g_ref, u_ref, dg_ref, du_ref):
        g, da_v = g_ref[...], da_ref[...]
        sg = _sigmoid(g)
        du_ref[...] = (da_v * g * sg).astype(bf16)
        dg_ref[...] = (da_v * u_ref[...] * sg * (1.0 + g * (1.0 - sg))).astype(bf16)

    return pl.pallas_call(body, grid=(4, s // tm), in_specs=[_sm_spec(tm, w)] * 3, out_specs=[_sm_spec(tm, w)] * 2,
                          out_shape=[_sds(hg.shape, bf16)] * 2, name=name, compiler_params=_params())(da, hg, hu)


def _ple_fwd(name, x2, gp, bias, pp):
    s, d = x2.shape
    tm = _tr(s)

    def body(x_ref, gp_ref, b_ref, pp_ref, y_ref):
        y_ref[...] = x_ref[...] + _sigmoid(gp_ref[...] + b_ref[...]) * pp_ref[...]

    return pl.pallas_call(body, grid=(s // tm,), in_specs=[_rows(tm, d), _rows(tm, d), _vec(d), _rows(tm, d)],
                          out_specs=_rows(tm, d), out_shape=_sds((s, d)), name=name,
                          compiler_params=_params())(x2, gp, bias, pp)


def _ple_bwd(name, dy, gp, bias, pp):
    s, d = dy.shape
    tm = _tr(s)

    def body(dy_ref, gp_ref, b_ref, pp_ref, dgp_ref, dpp_ref, db_ref):
        @pl.when(pl.program_id(0) == 0)
        def _():
            db_ref[...] = jnp.zeros_like(db_ref)

        dyv = dy_ref[...]
        gate = _sigmoid(gp_ref[...] + b_ref[...])
        dgp = dyv * pp_ref[...] * gate * (1.0 - gate)
        dgp_ref[...] = dgp.astype(bf16)
        dpp_ref[...] = (dyv * gate).astype(bf16)
        db_ref[...] += jnp.sum(dgp, axis=0, keepdims=True)

    return pl.pallas_call(body, grid=(s // tm,), in_specs=[_rows(tm, d), _rows(tm, d), _vec(d), _rows(tm, d)],
                          out_specs=[_rows(tm, d), _rows(tm, d), _vec(d)],
                          out_shape=[_sds((s, d), bf16), _sds((s, d), bf16), _sds((1, d))], name=name,
                          compiler_params=_params())(dy, gp, bias, pp)


def _loss_head(name, y, target):
    s, d = y.shape
    tm = _tr(s)

    def body(y_ref, t_ref, l_ref, dy_ref):
        @pl.when(pl.program_id(0) == 0)
        def _():
            l_ref[...] = jnp.zeros_like(l_ref)

        e = y_ref[...] - t_ref[...]
        dy_ref[...] = e * (1.0 / d)
        tot = jnp.sum(jnp.sum(e * e, axis=1, keepdims=True), axis=0, keepdims=True) * (0.5 / d)
        l_ref[...] += jnp.broadcast_to(tot, l_ref.shape)

    return pl.pallas_call(body, grid=(s // tm,), in_specs=[_rows(tm, d), _rows(tm, d)],
                          out_specs=[pl.BlockSpec((1, LANES), lambda i: (0, 0)), _rows(tm, d)],
                          out_shape=[_sds((1, LANES)), _sds((s, d))], name=name, compiler_params=_params())(y, target)


def _split_dot(x, m01):
    hi = x.astype(bf16)
    lo = (x - hi.astype(f32)).astype(bf16)
    return _dot(hi, m01) + _dot(lo, m01)


def _softplus(z):
    return jnp.maximum(z, 0.0) + jnp.log(1.0 + jnp.exp(-jnp.abs(z)))


def _sba_specs(s, dh, hp):
    qb_n = Q_BLOCK
    blk = pl.BlockSpec((hp, qb_n, dh), lambda hh, i: (hh, i, 0))
    full = pl.BlockSpec((hp, s, dh), lambda hh, i: (hh, 0, 0))
    col1 = pl.BlockSpec((hp, qb_n, 1), lambda hh, i: (hh, i, 0))
    return blk, full, col1


def _sba_fwd(q, k, v, shards):
    h, s, dh = q.shape
    hp, qb_n = SBA_FWD_HEADS, Q_BLOCK
    n = len(shards)
    steps = (h // hp, s // qb_n)

    def body(q_ref, k_ref, v_ref, *rest):
        o_ref, t_ref = rest[n:n + 2]
        local, over_ici, to_sibling = _gather_halves_copies(rest[:n], rest[n + 2:2 * n + 2], *rest[2 * n + 2:])
        i = pl.program_id(1)

        @pl.when(jnp.logical_and(pl.program_id(0) == 0, i == 0))
        def _():
            for cp in local + over_ici:
                cp.start()

        qbs = [q_ref[a].astype(bf16) for a in range(hp)]
        row = lax.broadcasted_iota(jnp.int32, (qb_n, qb_n), 0)
        col = lax.broadcasted_iota(jnp.int32, (qb_n, qb_n), 1)
        later = (row >= col).astype(bf16)

        def step(n, carry):
            tails, accs = carry
            off = pl.multiple_of((i - n) * qb_n, qb_n)
            mask = col < row + jnp.minimum(n, 1) * qb_n
            new_tails, new_accs = [], []
            for a in range(hp):
                kb = k_ref[a, pl.ds(off, qb_n), :].astype(bf16)
                vb = v_ref[a, pl.ds(off, qb_n), :].astype(bf16)
                z = _dot(qbs[a], kb, _NT) * SB_SCALE
                sp = _softplus(z)
                lk = jnp.where(mask, -sp, 0.0)
                cum = _split_dot(lk, later)
                w = jnp.where(mask, jnp.exp(z - sp + cum - lk + tails[a]), 0.0)
                new_tails.append(tails[a] + cum[:, 0:1])
                new_accs.append(accs[a] + _dot(w.astype(bf16), vb))
            return tuple(new_tails), tuple(new_accs)

        init = (tuple(jnp.zeros((qb_n, 1), f32) for _ in range(hp)), tuple(jnp.zeros((qb_n, dh), f32) for _ in range(hp)))
        tails, accs = lax.fori_loop(0, i + 1, step, init)
        for a in range(hp):
            o_ref[a] = accs[a]
            t_ref[a] = tails[a]

        @pl.when(jnp.logical_and(pl.program_id(0) == steps[0] - 1, i == steps[1] - 1))
        def _():
            for arrived, onward in zip(over_ici, to_sibling):
                arrived.wait_recv()
                onward.start()
            for cp in over_ici:
                cp.wait_send()
            for cp in to_sibling + local:
                cp.wait()

    blk, full, col1 = _sba_specs(s, dh, hp)
    sems = _comm_sems(n)
    outs = pl.pallas_call(body, grid=steps, in_specs=[blk, full, full] + [_ANY] * n,
                          out_specs=[blk, col1] + [_ANY] * n,
                          out_shape=[_sds((h, s, dh)), _sds((h, s, 1))] + _gather_shapes(shards),
                          scratch_shapes=sems + sems[:2], name="sba_fwd", compiler_params=_params())(q, k, v, *shards)
    return outs[0], outs[1], outs[2:]


def _sba_bwd(q, k, v, tot, do, grads, layout):
    h, s, dh = q.shape
    hp, qb_n = SBA_BWD_HEADS, Q_BLOCK
    n_in = len(grads)
    scatter_shape = _scatter_shapes(grads, layout)
    n_out = len(scatter_shape)
    steps = (h // hp, s // qb_n)

    def body(q_ref, k_ref, v_ref, t_ref, do_ref, *rest):
        dq_ref, dk_ref, dv_ref = rest[n_in:n_in + 3]
        copies = _scatter_copies(rest[:n_in], rest[n_in + 3:n_in + 3 + n_out], layout, *rest[n_in + 3 + n_out:])
        i = pl.program_id(1)

        @pl.when(jnp.logical_and(pl.program_id(0) == 0, i == 0))
        def _():
            for cp in copies:
                cp.start()

        @pl.when(i == 0)
        def _():
            dk_ref[...] = jnp.zeros_like(dk_ref)
            dv_ref[...] = jnp.zeros_like(dv_ref)

        qbs = [q_ref[a].astype(bf16) for a in range(hp)]
        dobs = [do_ref[a].astype(bf16) for a in range(hp)]
        tots = [t_ref[a] for a in range(hp)]
        row = lax.broadcasted_iota(jnp.int32, (qb_n, qb_n), 0)
        col = lax.broadcasted_iota(jnp.int32, (qb_n, qb_n), 1)
        upto = (row <= col).astype(bf16)
        before = (row < col).astype(bf16)

        def step(j, carry):
            heads, eheads, dqs = carry
            off = pl.multiple_of(j * qb_n, qb_n)
            mask = col < row + jnp.minimum(i - j, 1) * qb_n
            new_heads, new_eheads, new_dqs = [], [], []
            for a in range(hp):
                kb = k_ref[a, pl.ds(off, qb_n), :].astype(bf16)
                vb = v_ref[a, pl.ds(off, qb_n), :].astype(bf16)
                z = _dot(qbs[a], kb, _NT) * SB_SCALE
                sp = _softplus(z)
                lk = jnp.where(mask, -sp, 0.0)
                pre = _split_dot(lk, upto)
                w = jnp.where(mask, jnp.exp(z - sp + (tots[a] - heads[a] - pre)), 0.0)
                e = _dot(dobs[a], vb, _NT) * w
                epre = eheads[a] + _split_dot(e, before)
                dz = jnp.where(mask, e * jnp.exp(-sp) - epre * jnp.exp(z - sp), 0.0) * SB_SCALE
                dzb = dz.astype(bf16)
                dk_ref[a, pl.ds(off, qb_n), :] += _dot(dzb, qbs[a], _TN)
                dv_ref[a, pl.ds(off, qb_n), :] += _dot(w.astype(bf16), dobs[a], _TN)
                new_heads.append(heads[a] + pre[:, qb_n - 1:qb_n])
                new_eheads.append(eheads[a] + jnp.sum(e, axis=1, keepdims=True))
                new_dqs.append(dqs[a] + _dot(dzb, kb))
            return tuple(new_heads), tuple(new_eheads), tuple(new_dqs)

        zeros = tuple(jnp.zeros((qb_n, 1), f32) for _ in range(hp))
        _, _, dqs = lax.fori_loop(0, i + 1, step, (zeros, zeros, tuple(jnp.zeros((qb_n, dh), f32) for _ in range(hp))))
        for a in range(hp):
            dq_ref[a] = dqs[a]

        @pl.when(jnp.logical_and(pl.program_id(0) == steps[0] - 1, i == steps[1] - 1))
        def _():
            for cp in copies:
                cp.wait()

    blk, full, col1 = _sba_specs(s, dh, hp)
    outs = pl.pallas_call(body, grid=steps, in_specs=[blk, full, full, col1, blk] + [_ANY] * n_in,
                          out_specs=[blk, full, full] + [_ANY] * n_out,
                          out_shape=[_sds((h, s, dh))] * 3 + scatter_shape, scratch_shapes=_comm_sems(n_in),
                          name="sba_bwd", compiler_params=_params())(q, k, v, tot, do, *grads)
    return outs[0], outs[1], outs[2], outs[3:]


def _pool_fwd(hsm, pool_w, pool_scale):
    _, s, wd = hsm.shape
    ch = min(256, s)

    def body(u_ref, w_ref, sc_ref, b_ref, pooled_ref, pad_ref):
        pad_ref[0:POOL_HALO, :] = jnp.zeros((POOL_HALO, wd), f32)
        pad_ref[POOL_HALO:POOL_HALO + s, :] = u_ref[...]
        for g, win in enumerate(POOL_WINDOWS):
            cols = slice(g * GROUP_DIM, (g + 1) * GROUP_DIM)
            wg = w_ref[g].astype(bf16)
            for r0 in range(0, s, ch):
                acc = pad_ref[POOL_HALO + r0:POOL_HALO + r0 + ch, cols]
                own = acc
                for dlt in range(1, win):
                    acc = acc + pad_ref[POOL_HALO + r0 - dlt:POOL_HALO + r0 - dlt + ch, cols]
                t = r0 + lax.broadcasted_iota(jnp.int32, (ch, 1), 0)
                cnt = jnp.minimum(t + 1, win).astype(f32)
                pooled = acc / cnt - own
                pooled_ref[r0:r0 + ch, cols] = pooled
                b_ref[r0:r0 + ch, cols] = _dot(pooled.astype(bf16), wg) * sc_ref[:, cols]

    return pl.pallas_call(
        body, grid=(1,),
        in_specs=[pl.BlockSpec((None, s, wd), lambda i: (3, 0, 0)), pl.BlockSpec(pool_w.shape, lambda i: (0, 0, 0)),
                  _vec(wd)],
        out_specs=[pl.BlockSpec((s, wd), lambda i: (0, 0))] * 2, out_shape=[_sds((s, wd))] * 2,
        scratch_shapes=[pltpu.VMEM((POOL_HALO + s, wd), f32)], name="pool_fwd",
        compiler_params=_params())(hsm, pool_w, pool_scale)


def _pool_bwd(dcat, pooled, pool_w, pool_scale):
    s, wd = pooled.shape
    ch = min(256, s)

    def body(db_ref, p_ref, w_ref, sc_ref, du_ref, dw_ref, dsc_ref, pad_ref):
        pad_ref[s:s + POOL_HALO, :] = jnp.zeros((POOL_HALO, wd), f32)
        for g, win in enumerate(POOL_WINDOWS):
            cols = slice(g * GROUP_DIM, (g + 1) * GROUP_DIM)
            wg = w_ref[g].astype(bf16)
            pooled_g = p_ref[:, cols].astype(bf16)
            db = db_ref[:, cols]
            dmixed = (db * sc_ref[:, cols]).astype(bf16)
            dsc_ref[:, cols] = jnp.sum(db * _dot(pooled_g, wg), axis=0, keepdims=True)
            dw_ref[g] = _dot(pooled_g, dmixed, _TN)
            dpooled = _dot(dmixed, wg, _NT)
            t = lax.broadcasted_iota(jnp.int32, (s, 1), 0)
            pad_ref[0:s, cols] = dpooled / jnp.minimum(t + 1, win).astype(f32)
            for r0 in range(0, s, ch):
                acc = pad_ref[r0:r0 + ch, cols]
                for dlt in range(1, win):
                    acc = acc + pad_ref[r0 + dlt:r0 + dlt + ch, cols]
                du_ref[r0:r0 + ch, cols] = acc - dpooled[r0:r0 + ch]

    return pl.pallas_call(
        body, grid=(1,),
        in_specs=[pl.BlockSpec((s, wd), lambda i: (0, 1)), pl.BlockSpec((s, wd), lambda i: (0, 0)),
                  pl.BlockSpec(pool_w.shape, lambda i: (0, 0, 0)), _vec(wd)],
        out_specs=[pl.BlockSpec((s, wd), lambda i: (0, 0)), pl.BlockSpec(pool_w.shape, lambda i: (0, 0, 0)), _vec(wd)],
        out_shape=[_sds((s, wd)), _sds(pool_w.shape), _sds((1, wd))],
        scratch_shapes=[pltpu.VMEM((s + POOL_HALO, wd), f32)], name="pool_bwd",
        compiler_params=_params())(dcat, pooled, pool_w, pool_scale)


def _conv_fwd(hsm, dw, ln_g, ln_b):
    _, s, wd = hsm.shape
    rows, halo = CONV_ROWS, CONV_HALO

    def body(a_ref, g_ref, dw_ref, lg_ref, lb_ref, out_ref, hc_ref, xh_ref, rs_ref, pad_ref):
        hc = a_ref[...] * _sigmoid(g_ref[...])
        hc_ref[...] = hc
        pad_ref[0:halo, :] = jnp.zeros((halo, wd), f32)
        pad_ref[halo:halo + s, :] = hc
        taps = dw_ref[...]

        def chunk(c, _):
            base = pl.multiple_of(c * rows, rows)
            win = pad_ref[pl.ds(base, rows + halo), :]
            y = jnp.zeros((rows, wd), f32)
            for k in range(CONV_TAPS):
                lo = halo - (CONV_TAPS - 1) + k
                y = y + taps[k:k + 1, :] * win[lo:lo + rows]
            mu = jnp.mean(y, axis=-1, keepdims=True)
            cen = y - mu
            rstd = lax.rsqrt(jnp.mean(cen * cen, axis=-1, keepdims=True) + LN_EPS)
            xh = cen * rstd
            n = xh * lg_ref[...] + lb_ref[...]
            out_ref[pl.ds(base, rows), :] = n * _sigmoid(n)
            xh_ref[pl.ds(base, rows), :] = xh
            rs_ref[pl.ds(base, rows), :] = rstd
            return 0

        lax.fori_loop(0, s // rows, chunk, 0)

    full = pl.BlockSpec((s, wd), lambda i: (0, 0))
    return pl.pallas_call(
        body, grid=(1,),
        in_specs=[pl.BlockSpec((None, s, wd), lambda i: (0, 0, 0)), pl.BlockSpec((None, s, wd), lambda i: (1, 0, 0)),
                  pl.BlockSpec(dw.shape, lambda i: (0, 0)), _vec(wd), _vec(wd)],
        out_specs=[full, full, full, pl.BlockSpec((s, 1), lambda i: (0, 0))],
        out_shape=[_sds((s, wd))] * 3 + [_sds((s, 1))],
        scratch_shapes=[pltpu.VMEM((halo + s, wd), f32)], name="conv_fwd",
        compiler_params=_params())(hsm, hsm, dw, ln_g, ln_b)


def _conv_bwd(dcat, hsm, hc, xh, rstd, dw, ln_g, ln_b):
    s, wd = hc.shape
    rows, halo = CONV_ROWS, CONV_HALO

    def body(dc_ref, a_ref, g_ref, hc_ref, xh_ref, rs_ref, dw_ref, lg_ref, lb_ref,
             da_ref, dg_ref, ddw_ref, dlg_ref, dlb_ref, hpad_ref, ypad_ref):
        hpad_ref[0:halo, :] = jnp.zeros((halo, wd), f32)
        hpad_ref[halo:halo + s, :] = hc_ref[...]
        ypad_ref[s:s + halo, :] = jnp.zeros((halo, wd), f32)
        ddw_ref[...] = jnp.zeros_like(ddw_ref)
        dlg_ref[...] = jnp.zeros_like(dlg_ref)
        dlb_ref[...] = jnp.zeros_like(dlb_ref)
        taps = dw_ref[...]

        def norm_bwd(c, _):
            base = pl.multiple_of(c * rows, rows)
            xhv = xh_ref[pl.ds(base, rows), :]
            n = xhv * lg_ref[...] + lb_ref[...]
            sn = _sigmoid(n)
            dn = dc_ref[pl.ds(base, rows), :] * sn * (1.0 + n * (1.0 - sn))
            dlg_ref[...] += jnp.sum(dn * xhv, axis=0, keepdims=True)
            dlb_ref[...] += jnp.sum(dn, axis=0, keepdims=True)
            ypad_ref[pl.ds(base, rows), :] = _ln_bwd_rows(dn, xhv, rs_ref[pl.ds(base, rows), :], lg_ref[...])
            return 0

        lax.fori_loop(0, s // rows, norm_bwd, 0)

        def conv_bwd(c, _):
            base = pl.multiple_of(c * rows, rows)
            ywin = ypad_ref[pl.ds(base, rows + halo), :]
            hwin = hpad_ref[pl.ds(base, rows + halo), :]
            dy = ywin[0:rows]
            dhc = jnp.zeros((rows, wd), f32)
            for k in range(CONV_TAPS):
                fwd = CONV_TAPS - 1 - k
                dhc = dhc + taps[k:k + 1, :] * ywin[fwd:fwd + rows]
                lo = halo - (CONV_TAPS - 1) + k
                ddw_ref[k:k + 1, :] += jnp.sum(dy * hwin[lo:lo + rows], axis=0, keepdims=True)
            sg = _sigmoid(g_ref[pl.ds(base, rows), :])
            da_ref[pl.ds(base, rows), :] = dhc * sg
            dg_ref[pl.ds(base, rows), :] = dhc * a_ref[pl.ds(base, rows), :] * sg * (1.0 - sg)
            return 0

        lax.fori_loop(0, s // rows, conv_bwd, 0)

    full = pl.BlockSpec((s, wd), lambda i: (0, 0))
    tap_spec = pl.BlockSpec(dw.shape, lambda i: (0, 0))
    return pl.pallas_call(
        body, grid=(1,),
        in_specs=[full, pl.BlockSpec((None, s, wd), lambda i: (0, 0, 0)), pl.BlockSpec((None, s, wd), lambda i: (1, 0, 0)),
                  full, full, pl.BlockSpec((s, 1), lambda i: (0, 0)), tap_spec, _vec(wd), _vec(wd)],
        out_specs=[full, full, tap_spec, _vec(wd), _vec(wd)],
        out_shape=[_sds((s, wd)), _sds((s, wd)), _sds(dw.shape), _sds((1, wd)), _sds((1, wd))],
        scratch_shapes=[pltpu.VMEM((halo + s, wd), f32), pltpu.VMEM((s + halo, wd), f32)], name="conv_bwd",
        compiler_params=_params())(dcat, hsm, hsm, hc, xh, rstd, dw, ln_g, ln_b)


_GELU_C = 0.7978845608028654
_GELU_A = 0.044715


def _gelu(x):
    return 0.5 * x * (1.0 + jnp.tanh(_GELU_C * (x + _GELU_A * x * x * x)))


def _gelu_grad(x):
    th = jnp.tanh(_GELU_C * (x + _GELU_A * x * x * x))
    return 0.5 * (1.0 + th) + 0.5 * x * (1.0 - th * th) * _GELU_C * (1.0 + 3.0 * _GELU_A * x * x)


def _causal_sg_w(w_ref, g):
    row = lax.broadcasted_iota(jnp.int32, (SG_CHUNK, SG_CHUNK), 0)
    col = lax.broadcasted_iota(jnp.int32, (SG_CHUNK, SG_CHUNK), 1)
    return jnp.where(col <= row, w_ref[g], 0.0).astype(bf16), col <= row


def _gmlp_fwd(hsm, ln_g, ln_b, sg_w, sg_bt):
    _, s, wd = hsm.shape
    ck = SG_CHUNK

    def body(zu_ref, zv_ref, lg_ref, lb_ref, w_ref, bt_ref, out_ref, xh_ref, rs_ref):
        u = _gelu(zu_ref[...])
        vg = _gelu(zv_ref[...])
        mu = jnp.mean(vg, axis=-1, keepdims=True)
        cen = vg - mu
        rstd = lax.rsqrt(jnp.mean(cen * cen, axis=-1, keepdims=True) + LN_EPS)
        xh = cen * rstd
        xh_ref[...] = xh
        rs_ref[...] = rstd
        vn = (xh * lg_ref[...] + lb_ref[...]).astype(bf16)
        for g in range(4):
            cols = slice(g * GROUP_DIM, (g + 1) * GROUP_DIM)
            wm, _ = _causal_sg_w(w_ref, g)
            sv = _dot(wm, vn[:, cols]) + bt_ref[:, g:g + 1]
            out_ref[:, cols] = u[:, cols] * sv

    rows_spec = pl.BlockSpec((ck, wd), lambda i: (i, 0))
    return pl.pallas_call(
        body, grid=(s // ck,),
        in_specs=[pl.BlockSpec((None, ck, wd), lambda i: (2, i, 0)), pl.BlockSpec((None, ck, wd), lambda i: (3, i, 0)),
                  _vec(wd), _vec(wd), pl.BlockSpec(sg_w.shape, lambda i: (0, 0, 0)),
                  pl.BlockSpec(sg_bt.shape, lambda i: (0, 0))],
        out_specs=[rows_spec, rows_spec, pl.BlockSpec((ck, 1), lambda i: (i, 0))],
        out_shape=[_sds((s, wd)), _sds((s, wd)), _sds((s, 1))], name="gmlp_fwd",
        compiler_params=_params())(hsm, hsm, ln_g, ln_b, sg_w, sg_bt)


def _gmlp_bwd(dcat, hsm, xh, rstd, ln_g, ln_b, sg_w, sg_bt):
    s, wd = xh.shape
    ck = SG_CHUNK

    def body(dd_ref, zu_ref, zv_ref, xh_ref, rs_ref, lg_ref, lb_ref, w_ref, bt_ref,
             dzu_ref, dzv_ref, dw_ref, dbb_ref, dlg_ref, dlb_ref):
        @pl.when(pl.program_id(0) == 0)
        def _():
            dw_ref[...] = jnp.zeros_like(dw_ref)
            dbb_ref[...] = jnp.zeros_like(dbb_ref)
            dlg_ref[...] = jnp.zeros_like(dlg_ref)
            dlb_ref[...] = jnp.zeros_like(dlb_ref)

        zu, zv, dd, xhv = zu_ref[...], zv_ref[...], dd_ref[...], xh_ref[...]
        u = _gelu(zu)
        vn = (xhv * lg_ref[...] + lb_ref[...]).astype(bf16)
        du_parts, dvn_parts = [], []
        for g in range(4):
            cols = slice(g * GROUP_DIM, (g + 1) * GROUP_DIM)
            wm, keep = _causal_sg_w(w_ref, g)
            sv = _dot(wm, vn[:, cols]) + bt_ref[:, g:g + 1]
            du_parts.append(dd[:, cols] * sv)
            dsv = dd[:, cols] * u[:, cols]
            dsvb = dsv.astype(bf16)
            dbb_ref[g] += jnp.broadcast_to(jnp.sum(dsv, axis=1, keepdims=True), (ck, GROUP_DIM))
            dw_ref[g] += jnp.where(keep, _dot(dsvb, vn[:, cols], _NT), 0.0)
            dvn_parts.append(_dot(wm, dsvb, _TN))
        du = jnp.concatenate(du_parts, axis=1)
        dvn = jnp.concatenate(dvn_parts, axis=1)
        dlg_ref[...] += jnp.sum(dvn * xhv, axis=0, keepdims=True)
        dlb_ref[...] += jnp.sum(dvn, axis=0, keepdims=True)
        dzv_ref[...] = _ln_bwd_rows(dvn, xhv, rs_ref[...], lg_ref[...]) * _gelu_grad(zv)
        dzu_ref[...] = du * _gelu_grad(zu)

    rows_spec = pl.BlockSpec((ck, wd), lambda i: (i, 0))
    wspec = pl.BlockSpec(sg_w.shape, lambda i: (0, 0, 0))
    return pl.pallas_call(
        body, grid=(s // ck,),
        in_specs=[pl.BlockSpec((ck, wd), lambda i: (i, 1)), pl.BlockSpec((None, ck, wd), lambda i: (2, i, 0)),
                  pl.BlockSpec((None, ck, wd), lambda i: (3, i, 0)), rows_spec, pl.BlockSpec((ck, 1), lambda i: (i, 0)),
                  _vec(wd), _vec(wd), wspec, pl.BlockSpec(sg_bt.shape, lambda i: (0, 0))],
        out_specs=[rows_spec, rows_spec, wspec, wspec, _vec(wd), _vec(wd)],
        out_shape=[_sds((s, wd)), _sds((s, wd)), _sds(sg_w.shape), _sds(sg_w.shape), _sds((1, wd)), _sds((1, wd))],
        name="gmlp_bwd", compiler_params=_params())(dcat, hsm, hsm, xh, rstd, ln_g, ln_b, sg_w, sg_bt)


def _to_heads(x2d):
    s = x2d.shape[0]
    return jnp.transpose(x2d.reshape(s, N_HEADS, HEAD_DIM), (1, 0, 2)).astype(bf16)


def _from_heads(x3d):
    s = x3d.shape[1]
    return jnp.transpose(x3d, (1, 0, 2)).reshape(s, N_HEADS * HEAD_DIM)


def _local_step(x, p, target, wts, late_shards, small):
    wts = dict(wts)
    saved = []
    for i in range(DEPTH):
        tag = f"l{i}"
        if i % 2 == 0:
            hsm = _mm_nn_col(tag + "_in", x, wts["even_w_in"], 0)
            q, k, v = _to_heads(hsm[0]), _to_heads(hsm[1]), _to_heads(hsm[2])
            att_heads, sba_tot, late = _sba_fwd(q, k, v, late_shards)
            wts.update(zip(_LATE, late))
            att = _from_heads(att_heads)
            pool_out, pooled = _pool_fwd(hsm, small["pool_w"], small["pool_scale"])
            cat = jnp.concatenate([att, pool_out], axis=1).astype(bf16)
            mix = _mm_nn_row(tag + "_out", cat, wts["even_w_out"], 0)
            mixer_saved = (hsm, q, k, v, sba_tot, pooled, cat)
        else:
            hsm = _mm_nn_col(tag + "_in", x, wts["odd_w_in"], 0)
            conv_out, hc, cxh, crs = _conv_fwd(hsm, small["conv_dw"], small["conv_ln_g"], small["conv_ln_b"])
            sg_out, sxh, srs = _gmlp_fwd(hsm, small["sg_ln_g"], small["sg_ln_b"], small["sg_w"], small["sg_bt"])
            cat = jnp.concatenate([conv_out, sg_out], axis=1).astype(bf16)
            mix = _mm_nn_row(tag + "_out", cat, wts["odd_w_out"], 0)
            mixer_saved = (hsm, hc, cxh, crs, sxh, srs, cat)
        x1, xh1, rs1 = _ln_fwd(tag + "_ln_mix", x, mix, small["ln_mix_g"][i:i + 1], small["ln_mix_b"][i:i + 1])
        hg = _mm_nn_col(tag + "_gate", x1, wts["ffn_w_gate"], i)
        hu = _mm_nn_col(tag + "_up", x1, wts["ffn_w_up"], i)
        act = _swiglu_fwd(tag + "_swiglu", hg, hu)
        ffn = _mm_nn_row(tag + "_down", act, wts["ffn_w_down"], i)
        x2, xh2, rs2 = _ln_fwd(tag + "_ln_ffn", x1, ffn, small["ln_ffn_g"][i:i + 1], small["ln_ffn_b"][i:i + 1])
        gp = _mm_nn_row(tag + "_ple_gate", x2, wts["ple_w_gate"], i)
        pp = _mm_nn_col(tag + "_ple_proj", p[i], wts["ple_w_proj"], i, natural=True)
        x3 = _ple_fwd(tag + "_ple", x2, gp, small["ple_b_gate"][i:i + 1], pp)
        saved.append((x, mixer_saved, x1, xh1, rs1, hg, hu, act, x2, xh2, rs2, gp, pp))
        x = x3

    loss_part, dx = _loss_head("loss_head", x, target)

    big = {n: [None] * wts[n].shape[1] for n in wts}
    received = {}
    sm = {}
    per_layer = {n: [None] * DEPTH for n in ("ln_mix_g", "ln_mix_b", "ln_ffn_g", "ln_ffn_b", "ple_b_gate")}
    for i in reversed(range(DEPTH)):
        tag = f"l{i}b"
        x0, mixer_saved, x1, xh1, rs1, hg, hu, act, x2, xh2, rs2, gp, pp = saved[i]
        dgp, dpp, per_layer["ple_b_gate"][i] = _ple_bwd(tag + "_ple", dx, gp, small["ple_b_gate"][i:i + 1], pp)
        big["ple_w_proj"][i] = _mm_tn_col(tag + "_dproj", p[i], dpp)
        big["ple_w_gate"][i] = _mm_tn_row(tag + "_dgate", x2, dgp)
        dx2 = _mm_nt_row(tag + "_dx2", dgp, wts["ple_w_gate"], i, natural=True, add=dx)
        dr2, per_layer["ln_ffn_g"][i], per_layer["ln_ffn_b"][i] = _ln_bwd(tag + "_ln_ffn", dx2, xh2, rs2,
                                                                           small["ln_ffn_g"][i:i + 1])
        dact = _mm_nt_row(tag + "_dact", dr2, wts["ffn_w_down"], i)
        big["ffn_w_down"][i] = _mm_tn_row(tag + "_ddown", act, dr2)
        dhg, dhu = _swiglu_bwd(tag + "_swiglu", dact, hg, hu)
        big["ffn_w_gate"][i] = _mm_tn_col(tag + "_dgatew", x1, dhg)
        big["ffn_w_up"][i] = _mm_tn_col(tag + "_dupw", x1, dhu)
        part = _mm_nt_col(tag + "_dx1a", dhg, wts["ffn_w_gate"], i, dr2, ALPHA)
        dx1 = _mm_nt_col(tag + "_dx1b", dhu, wts["ffn_w_up"], i, part, 1.0)
        dr1, per_layer["ln_mix_g"][i], per_layer["ln_mix_b"][i] = _ln_bwd(tag + "_ln_mix", dx1, xh1, rs1,
                                                                           small["ln_mix_g"][i:i + 1])
        if i % 2 == 0:
            hsm, q, k, v, sba_tot, pooled, cat = mixer_saved
            big["even_w_out"][0] = _mm_tn_row(tag + "_dout", cat, dr1)
            dcat = _mm_nt_row(tag + "_dcat", dr1, wts["even_w_out"], 0, natural=True)
            entries, layout = _scatter_plan(big, _LATE)
            dq, dk, dv, got = _sba_bwd(q, k, v, sba_tot, _to_heads(dcat[:, :N_HEADS * HEAD_DIM]), entries, layout)
            received.update(zip(_LATE, got))
            du, sm["pool_w"], sm["pool_scale"] = _pool_bwd(dcat, pooled, small["pool_w"], small["pool_scale"])
            dhsm = jnp.stack([_from_heads(dq), _from_heads(dk), _from_heads(dv), du]).astype(bf16)
            w_in = "even_w_in"
        else:
            hsm, hc, cxh, crs, sxh, srs, cat = mixer_saved
            big["odd_w_out"][0] = _mm_tn_row(tag + "_dout", cat, dr1)
            dcat = _mm_nt_row(tag + "_dcat", dr1, wts["odd_w_out"], 0, natural=True)
            da, dg, sm["conv_dw"], sm["conv_ln_g"], sm["conv_ln_b"] = _conv_bwd(
                dcat, hsm, hc, cxh, crs, small["conv_dw"], small["conv_ln_g"], small["conv_ln_b"])
            dzu, dzv, sm["sg_w"], dsgb, sm["sg_ln_g"], sm["sg_ln_b"] = _gmlp_bwd(
                dcat, hsm, sxh, srs, small["sg_ln_g"], small["sg_ln_b"], small["sg_w"], small["sg_bt"])
            sm["sg_b"] = dsgb[:, :, 0]
            dhsm = jnp.stack([da, dg, dzu, dzv]).astype(bf16)
            w_in = "odd_w_in"
        big[w_in][0] = _mm_tn_col(tag + "_din", x0, dhsm)
        dx = _mm_nt_col(tag + "_dx", dhsm, wts[w_in], 0, dr1, ALPHA)
    for n, parts in per_layer.items():
        sm[n] = jnp.concatenate(parts, axis=0)
    return loss_part, dx, received, _scatter_plan(big, _EARLY), sm


def _scatter_plan(big, names):
    entries, layout = [], []
    for pi, n in enumerate(names):
        for li, g in enumerate(big[n]):
            entries.append(g)
            layout.append((pi, li))
    return entries, layout


def _place():
    x, y, c = lax.axis_index("x"), lax.axis_index("y"), lax.axis_index("c")
    return x, y, c, [(1 - x, y), (x, 1 - y), (1 - x, 1 - y)]


def _gather_chips(shards):
    n = len(shards)

    def body(*refs):
        copies = _gather_copies(refs[:n], refs[n:2 * n], *refs[2 * n:])
        for cp in copies:
            cp.start()
        for cp in copies:
            cp.wait()

    return pl.pallas_call(body, in_specs=[_ANY] * n, out_specs=[_ANY] * n, out_shape=_gather_shapes(shards),
                          scratch_shapes=_comm_sems(n), name="gather_chips")(*shards)


def _comm_sems(n):
    return [pltpu.SemaphoreType.DMA((n, 3)), pltpu.SemaphoreType.DMA((n, 3)), pltpu.SemaphoreType.DMA((n,))]


def _gather_shapes(shards):
    return [_sds((N_CHIPS,) + a.shape, a.dtype) for a in shards]


def _gather_copies(ins, outs, send, recv, loc):
    x, y, c, chips = _place()
    mine = 2 * x + y
    copies = []
    for t in range(len(ins)):
        copies.append(pltpu.make_async_copy(ins[t], outs[t].at[mine], loc.at[t]))
        for j, (cx, cy) in enumerate(chips):
            copies.append(pltpu.make_async_remote_copy(
                src_ref=ins[t], dst_ref=outs[t].at[mine], send_sem=send.at[t, j], recv_sem=recv.at[t, j],
                device_id=(cx, cy, c), device_id_type=_MESH))
    return copies


def _gather_halves_copies(ins, outs, send, recv, loc, send_on, recv_on):
    x, y, c, chips = _place()
    mine = 2 * x + y
    local, over_ici, onward = [], [], []
    for t in range(len(ins)):
        nl, r, _ = ins[t].shape
        half = r // 2
        rows = pl.ds(pl.multiple_of(c * half, 32), half)
        local.append(pltpu.make_async_copy(ins[t], outs[t].at[mine], loc.at[t]))
        for j, (cx, cy) in enumerate(chips):
            over_ici.append(pltpu.make_async_remote_copy(
                src_ref=ins[t].at[pl.ds(0, nl), rows], dst_ref=outs[t].at[mine, pl.ds(0, nl), rows],
                send_sem=send.at[t, j], recv_sem=recv.at[t, j], device_id=(cx, cy, c), device_id_type=_MESH))
            landed = outs[t].at[2 * cx + cy, pl.ds(0, nl), rows]
            onward.append(pltpu.make_async_remote_copy(
                src_ref=landed, dst_ref=landed, send_sem=send_on.at[t, j], recv_sem=recv_on.at[t, j],
                device_id=(x, y, 1 - c), device_id_type=_MESH))
    return local, over_ici, onward


def _scatter_shapes(grads, layout):
    shapes = {}
    for e, (pi, li) in enumerate(layout):
        r, cdim = grads[e].shape[1:]
        shapes[pi] = (N_CHIPS, max(li + 1, shapes.get(pi, (0, 0))[1]), r, cdim)
    return [_sds(shapes[pi], grads[0].dtype) for pi in range(len(shapes))]


def _scatter_copies(ins, outs, layout, send, recv, loc):
    x, y, c, chips = _place()
    mine = 2 * x + y
    copies = []
    for e, (pi, li) in enumerate(layout):
        copies.append(pltpu.make_async_copy(ins[e].at[mine], outs[pi].at[mine, li], loc.at[e]))
        for j, (cx, cy) in enumerate(chips):
            copies.append(pltpu.make_async_remote_copy(
                src_ref=ins[e].at[2 * cx + cy], dst_ref=outs[pi].at[mine, li], send_sem=send.at[e, j],
                recv_sem=recv.at[e, j], device_id=(cx, cy, c), device_id_type=_MESH))
    return copies


def _final_exchange(grads, layout, block):
    n_in = len(grads)
    out_shape = _scatter_shapes(grads, layout)
    n_out = len(out_shape)

    def body(*refs):
        blk_in, blk_out = refs[n_in], refs[n_in + 1 + n_out]
        send, recv, loc, send_all, recv_all, loc_all = refs[n_in + n_out + 2:]
        copies = _scatter_copies(refs[:n_in], refs[n_in + 1:n_in + 1 + n_out], layout, send, recv, loc)
        x, y, c, _ = _place()
        mine = 4 * x + 2 * y + c
        copies.append(pltpu.make_async_copy(blk_in, blk_out.at[mine], loc_all))
        for m in range(1, N_DEV):
            fx, fy, fc = (m >> 2) & 1, (m >> 1) & 1, m & 1
            peer = (x + fx - 2 * x * fx, y + fy - 2 * y * fy, c + fc - 2 * c * fc)
            copies.append(pltpu.make_async_remote_copy(
                src_ref=blk_in, dst_ref=blk_out.at[mine], send_sem=send_all.at[m - 1], recv_sem=recv_all.at[m - 1],
                device_id=peer, device_id_type=_MESH))
        for cp in copies:
            cp.start()
        for cp in copies:
            cp.wait()

    outs = pl.pallas_call(
        body, in_specs=[_ANY] * (n_in + 1), out_specs=[_ANY] * (n_out + 1),
        out_shape=out_shape + [_sds((N_DEV,) + block.shape)],
        scratch_shapes=_comm_sems(n_in) + [pltpu.SemaphoreType.DMA((N_DEV - 1,)), pltpu.SemaphoreType.DMA((N_DEV - 1,)),
                                           pltpu.SemaphoreType.DMA(())],
        name="final_exchange")(*grads, block)
    return outs[:n_out], outs[n_out]


def _swap_cores(arrays):
    n = len(arrays)

    def body(*refs):
        ins, outs = refs[:n], refs[n:2 * n]
        send, recv = refs[2 * n:]
        x, y, c, _ = _place()
        started = []
        for t in range(n):
            rc = pltpu.make_async_remote_copy(src_ref=ins[t], dst_ref=outs[t], send_sem=send.at[t], recv_sem=recv.at[t],
                                              device_id=(x, y, 1 - c), device_id_type=_MESH)
            rc.start()
            started.append(rc)
        for rc in started:
            rc.wait()

    return pl.pallas_call(
        body, in_specs=[_ANY] * n, out_specs=[_ANY] * n, out_shape=[_sds(a.shape, a.dtype) for a in arrays],
        scratch_shapes=[pltpu.SemaphoreType.DMA((n,)), pltpu.SemaphoreType.DMA((n,))], name="swap_cores")(*arrays)


def _row_tile(r):
    for t in (256, 128, 64, 32, 16, 8):
        if r % t == 0:
            return t
    return r


def _sum_stack(name, stack):
    n, r, c = stack.shape
    tr = _row_tile(r)

    def body(s_ref, o_ref):
        acc = s_ref[0].astype(f32)
        for t in range(1, n):
            acc = acc + s_ref[t].astype(f32)
        o_ref[...] = acc

    return pl.pallas_call(body, grid=(r // tr,), in_specs=[pl.BlockSpec((n, tr, c), lambda i: (0, i, 0))],
                          out_specs=pl.BlockSpec((tr, c), lambda i: (i, 0)), out_shape=_sds((r, c)), name=name,
                          compiler_params=_params())(stack)


def _adamw(name, w, g_a, g_b, m, v):
    r, c = w.shape
    tr = _row_tile(r)
    two = g_b is not None
    bc1 = 1.0 - ADAM_B1 ** ADAM_STEP
    bc2 = 1.0 - ADAM_B2 ** ADAM_STEP

    def body(*refs):
        if two:
            w_ref, ga_ref, gb_ref, m_ref, v_ref, g_out, d_out, m_out, v_out = refs
            g = ga_ref[...] + gb_ref[...]
        else:
            w_ref, ga_ref, m_ref, v_ref, g_out, d_out, m_out, v_out = refs
            g = ga_ref[...]
        m_new = ADAM_B1 * m_ref[...] + (1.0 - ADAM_B1) * g
        v_new = ADAM_B2 * v_ref[...] + (1.0 - ADAM_B2) * (g * g)
        g_out[...] = g
        m_out[...] = m_new
        v_out[...] = v_new
        d_out[...] = -ADAM_LR * ((m_new / bc1) / (jnp.sqrt(v_new / bc2) + ADAM_EPS) + ADAM_WD * w_ref[...])

    spec = pl.BlockSpec((tr, c), lambda i: (i, 0))
    ins = [w, g_a] + ([g_b] if two else []) + [m, v]
    return pl.pallas_call(body, grid=(r // tr,), in_specs=[spec] * len(ins), out_specs=[spec] * 4,
                          out_shape=[_sds((r, c))] * 4, name=name, compiler_params=_params())(*ins)


_EARLY = ("even_w_in", "even_w_out")
_LATE = ("odd_w_in", "odd_w_out", "ffn_w_gate", "ffn_w_up", "ffn_w_down", "ple_w_proj", "ple_w_gate")
_BIG = _EARLY + _LATE
_SHARDED_SMALL = ("conv_dw", "conv_ln_g", "conv_ln_b", "sg_ln_g", "sg_ln_b")
_SMALL = ("pool_w", "pool_scale", "conv_dw", "conv_ln_g", "conv_ln_b", "sg_ln_g", "sg_ln_b", "sg_w", "sg_b",
          "ln_mix_g", "ln_mix_b", "ln_ffn_g", "ln_ffn_b", "ple_b_gate")
_WEIGHTS = ("even_w_in", "even_w_out", "pool_w", "pool_scale", "odd_w_in", "odd_w_out", "conv_dw", "conv_ln_g",
            "conv_ln_b", "sg_ln_g", "sg_ln_b", "sg_w", "sg_b", "ln_mix_g", "ln_mix_b", "ffn_w_gate", "ffn_w_up",
            "ffn_w_down", "ln_ffn_g", "ln_ffn_b", "ple_w_proj", "ple_w_gate", "ple_b_gate")


def _pack(arrays):
    flat = jnp.concatenate([a.reshape(-1) for a in arrays])
    pad = (-flat.shape[0]) % (256 * LANES)
    return jnp.pad(flat, (0, pad)).reshape(-1, LANES)


def _unpack(packed, shapes):
    flat = packed.reshape(-1)
    out, off = [], 0
    for shp in shapes:
        size = 1
        for dim in shp:
            size *= dim
        out.append(flat[off:off + size].reshape(shp))
        off += size
    return out


def _unshard_last(g4):
    return jnp.concatenate([g4[k] for k in range(N_CHIPS)], axis=-1)


def kernel(x, p, even_w_in, even_w_out, pool_w, pool_scale, odd_w_in, odd_w_out, conv_dw, conv_ln_g, conv_ln_b, sg_ln_g, sg_ln_b, sg_w, sg_b, ln_mix_g, ln_mix_b, ffn_w_gate, ffn_w_up, ffn_w_down, ln_ffn_g, ln_ffn_b, ple_w_proj, ple_w_gate, ple_b_gate, loss_target, m_even_w_in, m_even_w_out, m_pool_w, m_pool_scale, m_odd_w_in, m_odd_w_out, m_conv_dw, m_conv_ln_g, m_conv_ln_b, m_sg_ln_g, m_sg_ln_b, m_sg_w, m_sg_b, m_ln_mix_g, m_ln_mix_b, m_ffn_w_gate, m_ffn_w_up, m_ffn_w_down, m_ln_ffn_g, m_ln_ffn_b, m_ple_w_proj, m_ple_w_gate, m_ple_b_gate, v_even_w_in, v_even_w_out, v_pool_w, v_pool_scale, v_odd_w_in, v_odd_w_out, v_conv_dw, v_conv_ln_g, v_conv_ln_b, v_sg_ln_g, v_sg_ln_b, v_sg_w, v_sg_b, v_ln_mix_g, v_ln_mix_b, v_ffn_w_gate, v_ffn_w_up, v_ffn_w_down, v_ln_ffn_g, v_ln_ffn_b, v_ple_w_proj, v_ple_w_gate, v_ple_b_gate):
    w = dict(even_w_in=even_w_in, even_w_out=even_w_out, pool_w=pool_w, pool_scale=pool_scale, odd_w_in=odd_w_in,
             odd_w_out=odd_w_out, conv_dw=conv_dw, conv_ln_g=conv_ln_g, conv_ln_b=conv_ln_b, sg_ln_g=sg_ln_g,
             sg_ln_b=sg_ln_b, sg_w=sg_w, sg_b=sg_b, ln_mix_g=ln_mix_g, ln_mix_b=ln_mix_b, ffn_w_gate=ffn_w_gate,
             ffn_w_up=ffn_w_up, ffn_w_down=ffn_w_down, ln_ffn_g=ln_ffn_g, ln_ffn_b=ln_ffn_b, ple_w_proj=ple_w_proj,
             ple_w_gate=ple_w_gate, ple_b_gate=ple_b_gate)
    mom = dict(even_w_in=m_even_w_in, even_w_out=m_even_w_out, pool_w=m_pool_w, pool_scale=m_pool_scale,
               odd_w_in=m_odd_w_in, odd_w_out=m_odd_w_out, conv_dw=m_conv_dw, conv_ln_g=m_conv_ln_g,
               conv_ln_b=m_conv_ln_b, sg_ln_g=m_sg_ln_g, sg_ln_b=m_sg_ln_b, sg_w=m_sg_w, sg_b=m_sg_b,
               ln_mix_g=m_ln_mix_g, ln_mix_b=m_ln_mix_b, ffn_w_gate=m_ffn_w_gate, ffn_w_up=m_ffn_w_up,
               ffn_w_down=m_ffn_w_down, ln_ffn_g=m_ln_ffn_g, ln_ffn_b=m_ln_ffn_b, ple_w_proj=m_ple_w_proj,
               ple_w_gate=m_ple_w_gate, ple_b_gate=m_ple_b_gate)
    var = dict(even_w_in=v_even_w_in, even_w_out=v_even_w_out, pool_w=v_pool_w, pool_scale=v_pool_scale,
               odd_w_in=v_odd_w_in, odd_w_out=v_odd_w_out, conv_dw=v_conv_dw, conv_ln_g=v_conv_ln_g,
               conv_ln_b=v_conv_ln_b, sg_ln_g=v_sg_ln_g, sg_ln_b=v_sg_ln_b, sg_w=v_sg_w, sg_b=v_sg_b,
               ln_mix_g=v_ln_mix_g, ln_mix_b=v_ln_mix_b, ffn_w_gate=v_ffn_w_gate, ffn_w_up=v_ffn_w_up,
               ffn_w_down=v_ffn_w_down, ln_ffn_g=v_ln_ffn_g, ln_ffn_b=v_ln_ffn_b, ple_w_proj=v_ple_w_proj,
               ple_w_gate=v_ple_w_gate, ple_b_gate=v_ple_b_gate)

    gathered = _gather_chips([w[n].astype(bf16) for n in _EARLY] + [w[n] for n in _SHARDED_SMALL])
    wts = dict(zip(_EARLY, gathered[:len(_EARLY)]))
    small = {n: w[n][0] for n in ("pool_w", "sg_w")}
    small.update({n: w[n] for n in ("pool_scale", "ln_mix_g", "ln_mix_b", "ln_ffn_g", "ln_ffn_b", "ple_b_gate")})
    small["sg_bt"] = jnp.transpose(w["sg_b"][0])
    for n, g4 in zip(_SHARDED_SMALL, gathered[len(_EARLY):]):
        small[n] = _unshard_last(g4)[0]
        if n != "conv_dw":
            small[n] = small[n][None]

    loss_part, grad_x, received, last_plan, sm = _local_step(x[0], p[:, 0], loss_target[0], wts,
                                                             [w[n].astype(bf16) for n in _LATE], small)

    sm_shapes = [(1,) + sm[n].shape if n in ("pool_w", "sg_w", "conv_dw", "sg_b") else sm[n].shape for n in _SMALL]
    got, small_stack = _final_exchange(*last_plan, _pack([sm[n] for n in _SMALL] + [loss_part[0, 0:1]]))
    received.update(zip(_EARLY, got))

    chip_sums = []
    for n in _BIG:
        _, nl, r, c = received[n].shape
        chip_sums.append(_sum_stack("sum_" + n, received[n].reshape(N_CHIPS, nl * r, c)))
    other = _swap_cores(chip_sums)
    results = {}
    for n, mine, theirs in zip(_BIG, chip_sums, other):
        shp = w[n].shape
        flat = (shp[0] * shp[1], shp[2])
        outs = _adamw("adamw_" + n, w[n].reshape(flat), mine, theirs, mom[n].reshape(flat), var[n].reshape(flat))
        results[n] = [o.reshape(shp) for o in outs]

    total = _sum_stack("sum_small", small_stack)
    parts = _unpack(total, sm_shapes + [(1,)])
    loss = parts[-1][0]
    chip = 2 * lax.axis_index("x") + lax.axis_index("y")
    g_small = {}
    for n, g in zip(_SMALL, parts[:-1]):
        if n in _SHARDED_SMALL:
            width = w[n].shape[-1]
            g = lax.dynamic_slice_in_dim(g, chip * width, width, axis=g.ndim - 1)
        g_small[n] = g
    shapes = [w[n].shape for n in _SMALL]
    outs = _adamw("adamw_small", _pack([w[n] for n in _SMALL]), _pack([g_small[n] for n in _SMALL]), None,
                  _pack([mom[n] for n in _SMALL]), _pack([var[n] for n in _SMALL]))
    unpacked = [_unpack(o, shapes) for o in outs]
    for idx, n in enumerate(_SMALL):
        results[n] = [u[idx] for u in unpacked]

    return (loss, grad_x[None], *[results[n][0] for n in _WEIGHTS], *[results[n][1] for n in _WEIGHTS],
            *[results[n][2] for n in _WEIGHTS], *[results[n][3] for n in _WEIGHTS])
```

```python
import jax
import jax.numpy as jnp
from jax import lax
from jax.experimental import pallas as pl
from jax.experimental.pallas import tpu as pltpu

f32 = jnp.float32
bf16 = jnp.bfloat16

D_MODEL = 1024
N_HEADS = 8
HEAD_DIM = 64
Q_BLOCK = 128
POOL_WINDOWS = (2, 4, 8, 16)
GROUP_DIM = 128
CONV_TAPS = 31
SG_CHUNK = 128
DEPTH = 2
ALPHA = (2 * DEPTH) ** 0.25
LN_EPS = 1e-5
SB_SCALE = HEAD_DIM ** -0.5
ADAM_LR, ADAM_B1, ADAM_B2, ADAM_EPS, ADAM_WD, ADAM_STEP = 0.001, 0.9, 0.999, 1e-08, 0.01, 10
N_CHIPS = 4
N_DEV = 8
LANES = 128
VMEM_LIMIT = 56 * 1024 * 1024
TM = 2048
TN_COLS = 512
TR = 512
SBA_FWD_HEADS = 8
SBA_BWD_HEADS = 8
CONV_ROWS = 64
CONV_HALO = 32
POOL_HALO = 16

_NN = (((1,), (0,)), ((), ()))
_NT = (((1,), (1,)), ((), ()))
_TN = (((0,), (0,)), ((), ()))
_ANY = pl.BlockSpec(memory_space=pl.ANY)
_MESH = pl.DeviceIdType.MESH


def _params():
    return pltpu.CompilerParams(vmem_limit_bytes=VMEM_LIMIT)


def _sds(shape, dtype=f32):
    return jax.ShapeDtypeStruct(tuple(shape), dtype)


def _dot(a, b, dims=_NN):
    return lax.dot_general(a, b, dims, preferred_element_type=f32)


def _sigmoid(x):
    return 1.0 / (1.0 + jnp.exp(-x))


def _mm(name, a, b, grid, a_spec, b_spec, out_shape, out_spec, dims, reduce=False, add=None, add_spec=None,
        add_scale=1.0, out_dtype=f32):
    has_add = add is not None
    k_axis = len(grid) - 1

    def body(*refs):
        if has_add:
            a_ref, b_ref, add_ref, o_ref = refs
        else:
            a_ref, b_ref, o_ref = refs
        r = _dot(a_ref[...].astype(bf16), b_ref[...].astype(bf16), dims)
        if reduce:
            k = pl.program_id(k_axis)

            @pl.when(k == 0)
            def _():
                o_ref[...] = r + add_scale * add_ref[...] if has_add else r

            @pl.when(k > 0)
            def _():
                o_ref[...] += r
        else:
            o_ref[...] = (r + add_scale * add_ref[...] if has_add else r).astype(out_dtype)

    ins = [a, b] + ([add] if has_add else [])
    specs = [a_spec, b_spec] + ([add_spec] if has_add else [])
    return pl.pallas_call(body, grid=grid, in_specs=specs, out_specs=out_spec, out_shape=_sds(out_shape, out_dtype),
                          name=name, compiler_params=_params())(*ins)


def _tm(s):
    return min(TM, s)


def _act_spec(a, tm, width):
    if a.ndim == 3:
        return pl.BlockSpec((None, tm, width), lambda i, k: (k, i, 0))
    return pl.BlockSpec((tm, width), lambda i, k: (i, k))


def _mm_nn_col(name, x, w4, layer, natural=False):
    s, kk = x.shape
    nq = w4.shape[3]
    tm = _tm(s)
    if natural:
        out_shape, out_spec = (s, 4 * nq), pl.BlockSpec((tm, nq), lambda i, k: (i, k))
    else:
        out_shape, out_spec = (4, s, nq), pl.BlockSpec((None, tm, nq), lambda i, k: (k, i, 0))
    return _mm(name, x, w4, (s // tm, 4), pl.BlockSpec((tm, kk), lambda i, k: (i, 0)),
               pl.BlockSpec((None, None, kk, nq), lambda i, k: (k, layer, 0, 0)), out_shape, out_spec, _NN)


def _mm_nn_row(name, a, w4, layer):
    s = a.shape[-2]
    kq, n = w4.shape[2], w4.shape[3]
    tm = _tm(s)
    return _mm(name, a, w4, (s // tm, 4), _act_spec(a, tm, kq),
               pl.BlockSpec((None, None, kq, n), lambda i, k: (k, layer, 0, 0)), (s, n),
               pl.BlockSpec((tm, n), lambda i, k: (i, 0)), _NN, reduce=True)


def _mm_nt_col(name, dh, w4, layer, add, add_scale):
    s = dh.shape[-2]
    kk, nq = w4.shape[2], w4.shape[3]
    tm = _tm(s)
    row = pl.BlockSpec((tm, kk), lambda i, k: (i, 0))
    return _mm(name, dh, w4, (s // tm, 4), _act_spec(dh, tm, nq),
               pl.BlockSpec((None, None, kk, nq), lambda i, k: (k, layer, 0, 0)), (s, kk), row, _NT,
               reduce=True, add=add, add_spec=row, add_scale=add_scale)


def _mm_nt_row(name, dy, w4, layer, natural=False, add=None):
    s, n = dy.shape
    kq = w4.shape[2]
    tm = _tm(s)
    if natural:
        out_shape, out_spec = (s, 4 * kq), pl.BlockSpec((tm, kq), lambda i, k: (i, k))
    else:
        out_shape, out_spec = (4, s, kq), pl.BlockSpec((None, tm, kq), lambda i, k: (k, i, 0))
    return _mm(name, dy, w4, (s // tm, 4), pl.BlockSpec((tm, n), lambda i, k: (i, 0)),
               pl.BlockSpec((None, None, kq, n), lambda i, k: (k, layer, 0, 0)), out_shape, out_spec, _NT,
               add=add, add_spec=out_spec if add is not None else None)


def _mm_tn_col(name, x, dh):
    s, kk = x.shape
    if dh.ndim == 3:
        nq = dh.shape[2]
        b_spec = pl.BlockSpec((None, s, nq), lambda k, j: (k, 0, 0))
    else:
        nq = dh.shape[1] // 4
        b_spec = pl.BlockSpec((s, nq), lambda k, j: (0, k))
    tk = min(TN_COLS, kk)
    return _mm(name, x, dh, (4, kk // tk), pl.BlockSpec((s, tk), lambda k, j: (0, j)), b_spec, (4, kk, nq),
               pl.BlockSpec((None, tk, nq), lambda k, j: (k, j, 0)), _TN, out_dtype=bf16)


def _mm_tn_row(name, a, dy):
    s, n = dy.shape
    if a.ndim == 3:
        kq = a.shape[2]
        a_spec = pl.BlockSpec((None, s, kq), lambda k, j: (k, 0, 0))
    else:
        kq = a.shape[1] // 4
        a_spec = pl.BlockSpec((s, kq), lambda k, j: (0, k))
    tn = min(2 * TN_COLS, n)
    return _mm(name, a, dy, (4, n // tn), a_spec, pl.BlockSpec((s, tn), lambda k, j: (0, j)), (4, kq, n),
               pl.BlockSpec((None, kq, tn), lambda k, j: (k, 0, j)), _TN, out_dtype=bf16)


def _tr(s):
    return min(TR, s)


def _rows(tm, d):
    return pl.BlockSpec((tm, d), lambda i: (i, 0))


def _vec(d):
    return pl.BlockSpec((1, d), lambda i: (0, 0))


def _ln_fwd(name, x, mix, g, b):
    s, d = x.shape
    tm = _tr(s)

    def body(x_ref, m_ref, g_ref, b_ref, y_ref, xh_ref, rs_ref):
        r = ALPHA * x_ref[...] + m_ref[...]
        mu = jnp.mean(r, axis=-1, keepdims=True)
        c = r - mu
        rstd = lax.rsqrt(jnp.mean(c * c, axis=-1, keepdims=True) + LN_EPS)
        xh = c * rstd
        y_ref[...] = xh * g_ref[...] + b_ref[...]
        xh_ref[...] = xh
        rs_ref[...] = rstd

    return pl.pallas_call(
        body, grid=(s // tm,), in_specs=[_rows(tm, d), _rows(tm, d), _vec(d), _vec(d)],
        out_specs=[_rows(tm, d), _rows(tm, d), _rows(tm, 1)],
        out_shape=[_sds((s, d)), _sds((s, d)), _sds((s, 1))], name=name, compiler_params=_params())(x, mix, g, b)


def _ln_bwd_rows(dy, xh, rstd, g):
    dxh = dy * g
    m1 = jnp.mean(dxh, axis=-1, keepdims=True)
    m2 = jnp.mean(dxh * xh, axis=-1, keepdims=True)
    return rstd * (dxh - m1 - xh * m2)


def _ln_bwd(name, dy, xh, rstd, g):
    s, d = dy.shape
    tm = _tr(s)

    def body(dy_ref, xh_ref, rs_ref, g_ref, dr_ref, dg_ref, db_ref):
        @pl.when(pl.program_id(0) == 0)
        def _():
            dg_ref[...] = jnp.zeros_like(dg_ref)
            db_ref[...] = jnp.zeros_like(db_ref)

        dyv, xhv = dy_ref[...], xh_ref[...]
        dr_ref[...] = _ln_bwd_rows(dyv, xhv, rs_ref[...], g_ref[...])
        dg_ref[...] += jnp.sum(dyv * xhv, axis=0, keepdims=True)
        db_ref[...] += jnp.sum(dyv, axis=0, keepdims=True)

    return pl.pallas_call(
        body, grid=(s // tm,), in_specs=[_rows(tm, d), _rows(tm, d), _rows(tm, 1), _vec(d)],
        out_specs=[_rows(tm, d), _vec(d), _vec(d)],
        out_shape=[_sds((s, d)), _sds((1, d)), _sds((1, d))], name=name, compiler_params=_params())(dy, xh, rstd, g)


def _sm_spec(tm, w):
    return pl.BlockSpec((None, tm, w), lambda k, i: (k, i, 0))


def _swiglu_fwd(name, hg, hu):
    _, s, w = hg.shape
    tm = _tr(s)

    def body(g_ref, u_ref, a_ref):
        g = g_ref[...]
        a_ref[...] = (g * _sigmoid(g) * u_ref[...]).astype(bf16)

    return pl.pallas_call(body, grid=(4, s // tm), in_specs=[_sm_spec(tm, w)] * 2, out_specs=_sm_spec(tm, w),
                          out_shape=_sds(hg.shape, bf16), name=name, compiler_params=_params())(hg, hu)


def _swiglu_bwd(name, da, hg, hu):
    _, s, w = hg.shape
    tm = _tr(s)

    def body(da_ref, g_ref, u_ref, dg_ref, du_ref):
        g, da_v = g_ref[...], da_ref[...]
        sg = _sigmoid(g)
        du_ref[...] = (da_v * g * sg).astype(bf16)
        dg_ref[...] = (da_v * u_ref[...] * sg * (1.0 + g * (1.0 - sg))).astype(bf16)

    return pl.pallas_call(body, grid=(4, s // tm), in_specs=[_sm_spec(tm, w)] * 3, out_specs=[_sm_spec(tm, w)] * 2,
                          out_shape=[_sds(hg.shape, bf16)] * 2, name=name, compiler_params=_params())(da, hg, hu)


def _ple_fwd(name, x2, gp, bias, pp):
    s, d = x2.shape
    tm = _tr(s)

    def body(x_ref, gp_ref, b_ref, pp_ref, y_ref):
        y_ref[...] = x_ref[...] + _sigmoid(gp_ref[...] + b_ref[...]) * pp_ref[...]

    return pl.pallas_call(body, grid=(s // tm,), in_specs=[_rows(tm, d), _rows(tm, d), _vec(d), _rows(tm, d)],
                          out_specs=_rows(tm, d), out_shape=_sds((s, d)), name=name,
                          compiler_params=_params())(x2, gp, bias, pp)


def _ple_bwd(name, dy, gp, bias, pp):
    s, d = dy.shape
    tm = _tr(s)

    def body(dy_ref, gp_ref, b_ref, pp_ref, dgp_ref, dpp_ref, db_ref):
        @pl.when(pl.program_id(0) == 0)
        def _():
            db_ref[...] = jnp.zeros_like(db_ref)

        dyv = dy_ref[...]
        gate = _sigmoid(gp_ref[...] + b_ref[...])
        dgp = dyv * pp_ref[...] * gate * (1.0 - gate)
        dgp_ref[...] = dgp.astype(bf16)
        dpp_ref[...] = (dyv * gate).astype(bf16)
        db_ref[...] += jnp.sum(dgp, axis=0, keepdims=True)

    return pl.pallas_call(body, grid=(s // tm,), in_specs=[_rows(tm, d), _rows(tm, d), _vec(d), _rows(tm, d)],
                          out_specs=[_rows(tm, d), _rows(tm, d), _vec(d)],
                          out_shape=[_sds((s, d), bf16), _sds((s, d), bf16), _sds((1, d))], name=name,
                          compiler_params=_params())(dy, gp, bias, pp)


def _loss_head(name, y, target):
    s, d = y.shape
    tm = _tr(s)

    def body(y_ref, t_ref, l_ref, dy_ref):
        @pl.when(pl.program_id(0) == 0)
        def _():
            l_ref[...] = jnp.zeros_like(l_ref)

        e = y_ref[...] - t_ref[...]
        dy_ref[...] = e * (1.0 / d)
        tot = jnp.sum(jnp.sum(e * e, axis=1, keepdims=True), axis=0, keepdims=True) * (0.5 / d)
        l_ref[...] += jnp.broadcast_to(tot, l_ref.shape)

    return pl.pallas_call(body, grid=(s // tm,), in_specs=[_rows(tm, d), _rows(tm, d)],
                          out_specs=[pl.BlockSpec((1, LANES), lambda i: (0, 0)), _rows(tm, d)],
                          out_shape=[_sds((1, LANES)), _sds((s, d))], name=name, compiler_params=_params())(y, target)


def _split_dot(x, m01):
    hi = x.astype(bf16)
    lo = (x - hi.astype(f32)).astype(bf16)
    return _dot(hi, m01) + _dot(lo, m01)


def _softplus(z):
    return jnp.maximum(z, 0.0) + jnp.log(1.0 + jnp.exp(-jnp.abs(z)))


def _sba_specs(s, dh, hp):
    qb_n = Q_BLOCK
    blk = pl.BlockSpec((hp, qb_n, dh), lambda hh, i: (hh, i, 0))
    full = pl.BlockSpec((hp, s, dh), lambda hh, i: (hh, 0, 0))
    col1 = pl.BlockSpec((hp, qb_n, 1), lambda hh, i: (hh, i, 0))
    return blk, full, col1


def _sba_fwd(q, k, v, shards):
    h, s, dh = q.shape
    hp, qb_n = SBA_FWD_HEADS, Q_BLOCK
    n = len(shards)
    steps = (h // hp, s // qb_n)

    def body(q_ref, k_ref, v_ref, *rest):
        o_ref, t_ref = rest[n:n + 2]
        local, over_ici, to_sibling = _gather_halves_copies(rest[:n], rest[n + 2:2 * n + 2], *rest[2 * n + 2:])
        i = pl.program_id(1)

        @pl.when(jnp.logical_and(pl.program_id(0) == 0, i == 0))
        def _():
            for cp in local + over_ici:
                cp.start()

        qbs = [q_ref[a].astype(bf16) for a in range(hp)]
        row = lax.broadcasted_iota(jnp.int32, (qb_n, qb_n), 0)
        col = lax.broadcasted_iota(jnp.int32, (qb_n, qb_n), 1)
        later = (row >= col).astype(bf16)

        def step(n, carry):
            tails, accs = carry
            off = pl.multiple_of((i - n) * qb_n, qb_n)
            mask = col < row + jnp.minimum(n, 1) * qb_n
            new_tails, new_accs = [], []
            for a in range(hp):
                kb = k_ref[a, pl.ds(off, qb_n), :].astype(bf16)
                vb = v_ref[a, pl.ds(off, qb_n), :].astype(bf16)
                z = _dot(qbs[a], kb, _NT) * SB_SCALE
                sp = _softplus(z)
                lk = jnp.where(mask, -sp, 0.0)
                cum = _split_dot(lk, later)
                w = jnp.where(mask, jnp.exp(z - sp + cum - lk + tails[a]), 0.0)
                new_tails.append(tails[a] + cum[:, 0:1])
                new_accs.append(accs[a] + _dot(w.astype(bf16), vb))
            return tuple(new_tails), tuple(new_accs)

        init = (tuple(jnp.zeros((qb_n, 1), f32) for _ in range(hp)), tuple(jnp.zeros((qb_n, dh), f32) for _ in range(hp)))
        tails, accs = lax.fori_loop(0, i + 1, step, init)
        for a in range(hp):
            o_ref[a] = accs[a]
            t_ref[a] = tails[a]

        @pl.when(jnp.logical_and(pl.program_id(0) == steps[0] - 1, i == steps[1] - 1))
        def _():
            for arrived, onward in zip(over_ici, to_sibling):
                arrived.wait_recv()
                onward.start()
            for cp in over_ici:
                cp.wait_send()
            for cp in to_sibling + local:
                cp.wait()

    blk, full, col1 = _sba_specs(s, dh, hp)
    sems = _comm_sems(n)
    outs = pl.pallas_call(body, grid=steps, in_specs=[blk, full, full] + [_ANY] * n,
                          out_specs=[blk, col1] + [_ANY] * n,
                          out_shape=[_sds((h, s, dh)), _sds((h, s, 1))] + _gather_shapes(shards),
                          scratch_shapes=sems + sems[:2], name="sba_fwd", compiler_params=_params())(q, k, v, *shards)
    return outs[0], outs[1], outs[2:]


def _sba_bwd(q, k, v, tot, do, grads, layout):
    h, s, dh = q.shape
    hp, qb_n = SBA_BWD_HEADS, Q_BLOCK
    n_in = len(grads)
    scatter_shape = _scatter_shapes(grads, layout)
    n_out = len(scatter_shape)
    steps = (h // hp, s // qb_n)

    def body(q_ref, k_ref, v_ref, t_ref, do_ref, *rest):
        dq_ref, dk_ref, dv_ref = rest[n_in:n_in + 3]
        copies = _scatter_copies(rest[:n_in], rest[n_in + 3:n_in + 3 + n_out], layout, *rest[n_in + 3 + n_out:])
        i = pl.program_id(1)

        @pl.when(jnp.logical_and(pl.program_id(0) == 0, i == 0))
        def _():
            for cp in copies:
                cp.start()

        @pl.when(i == 0)
        def _():
            dk_ref[...] = jnp.zeros_like(dk_ref)
            dv_ref[...] = jnp.zeros_like(dv_ref)

        qbs = [q_ref[a].astype(bf16) for a in range(hp)]
        dobs = [do_ref[a].astype(bf16) for a in range(hp)]
        tots = [t_ref[a] for a in range(hp)]
        row = lax.broadcasted_iota(jnp.int32, (qb_n, qb_n), 0)
        col = lax.broadcasted_iota(jnp.int32, (qb_n, qb_n), 1)
        upto = (row <= col).astype(bf16)
        before = (row < col).astype(bf16)

        def step(j, carry):
            heads, eheads, dqs = carry
            off = pl.multiple_of(j * qb_n, qb_n)
            mask = col < row + jnp.minimum(i - j, 1) * qb_n
            new_heads, new_eheads, new_dqs = [], [], []
            for a in range(hp):
                kb = k_ref[a, pl.ds(off, qb_n), :].astype(bf16)
                vb = v_ref[a, pl.ds(off, qb_n), :].astype(bf16)
                z = _dot(qbs[a], kb, _NT) * SB_SCALE
                sp = _softplus(z)
                lk = jnp.where(mask, -sp, 0.0)
                pre = _split_dot(lk, upto)
                w = jnp.where(mask, jnp.exp(z - sp + (tots[a] - heads[a] - pre)), 0.0)
                e = _dot(dobs[a], vb, _NT) * w
                epre = eheads[a] + _split_dot(e, before)
                dz = jnp.where(mask, e * jnp.exp(-sp) - epre * jnp.exp(z - sp), 0.0) * SB_SCALE
                dzb = dz.astype(bf16)
                dk_ref[a, pl.ds(off, qb_n), :] += _dot(dzb, qbs[a], _TN)
                dv_ref[a, pl.ds(off, qb_n), :] += _dot(w.astype(bf16), dobs[a], _TN)
                new_heads.append(heads[a] + pre[:, qb_n - 1:qb_n])
                new_eheads.append(eheads[a] + jnp.sum(e, axis=1, keepdims=True))
                new_dqs.append(dqs[a] + _dot(dzb, kb))
            return tuple(new_heads), tuple(new_eheads), tuple(new_dqs)

        zeros = tuple(jnp.zeros((qb_n, 1), f32) for _ in range(hp))
        _, _, dqs = lax.fori_loop(0, i + 1, step, (zeros, zeros, tuple(jnp.zeros((qb_n, dh), f32) for _ in range(hp))))
        for a in range(hp):
            dq_ref[a] = dqs[a]

        @pl.when(jnp.logical_and(pl.program_id(0) == steps[0] - 1, i == steps[1] - 1))
        def _():
            for cp in copies:
                cp.wait()

    blk, full, col1 = _sba_specs(s, dh, hp)
    outs = pl.pallas_call(body, grid=steps, in_specs=[blk, full, full, col1, blk] + [_ANY] * n_in,
                          out_specs=[blk, full, full] + [_ANY] * n_out,
                          out_shape=[_sds((h, s, dh))] * 3 + scatter_shape, scratch_shapes=_comm_sems(n_in),
                          name="sba_bwd", compiler_params=_params())(q, k, v, tot, do, *grads)
    return outs[0], outs[1], outs[2], outs[3:]


def _pool_fwd(hsm, pool_w, pool_scale):
    _, s, wd = hsm.shape
    ch = min(256, s)

    def body(u_ref, w_ref, sc_ref, b_ref, pooled_ref, pad_ref):
        pad_ref[0:POOL_HALO, :] = jnp.zeros((POOL_HALO, wd), f32)
        pad_ref[POOL_HALO:POOL_HALO + s, :] = u_ref[...]
        for g, win in enumerate(POOL_WINDOWS):
            cols = slice(g * GROUP_DIM, (g + 1) * GROUP_DIM)
            wg = w_ref[g].astype(bf16)
            for r0 in range(0, s, ch):
                acc = pad_ref[POOL_HALO + r0:POOL_HALO + r0 + ch, cols]
                own = acc
                for dlt in range(1, win):
                    acc = acc + pad_ref[POOL_HALO + r0 - dlt:POOL_HALO + r0 - dlt + ch, cols]
                t = r0 + lax.broadcasted_iota(jnp.int32, (ch, 1), 0)
                cnt = jnp.minimum(t + 1, win).astype(f32)
                pooled = acc / cnt - own
                pooled_ref[r0:r0 + ch, cols] = pooled
                b_ref[r0:r0 + ch, cols] = _dot(pooled.astype(bf16), wg) * sc_ref[:, cols]

    return pl.pallas_call(
        body, grid=(1,),
        in_specs=[pl.BlockSpec((None, s, wd), lambda i: (3, 0, 0)), pl.BlockSpec(pool_w.shape, lambda i: (0, 0, 0)),
                  _vec(wd)],
        out_specs=[pl.BlockSpec((s, wd), lambda i: (0, 0))] * 2, out_shape=[_sds((s, wd))] * 2,
        scratch_shapes=[pltpu.VMEM((POOL_HALO + s, wd), f32)], name="pool_fwd",
        compiler_params=_params())(hsm, pool_w, pool_scale)


def _pool_bwd(dcat, pooled, pool_w, pool_scale):
    s, wd = pooled.shape
    ch = min(256, s)

    def body(db_ref, p_ref, w_ref, sc_ref, du_ref, dw_ref, dsc_ref, pad_ref):
        pad_ref[s:s + POOL_HALO, :] = jnp.zeros((POOL_HALO, wd), f32)
        for g, win in enumerate(POOL_WINDOWS):
            cols = slice(g * GROUP_DIM, (g + 1) * GROUP_DIM)
            wg = w_ref[g].astype(bf16)
            pooled_g = p_ref[:, cols].astype(bf16)
            db = db_ref[:, cols]
            dmixed = (db * sc_ref[:, cols]).astype(bf16)
            dsc_ref[:, cols] = jnp.sum(db * _dot(pooled_g, wg), axis=0, keepdims=True)
            dw_ref[g] = _dot(pooled_g, dmixed, _TN)
            dpooled = _dot(dmixed, wg, _NT)
            t = lax.broadcasted_iota(jnp.int32, (s, 1), 0)
            pad_ref[0:s, cols] = dpooled / jnp.minimum(t + 1, win).astype(f32)
            for r0 in range(0, s, ch):
                acc = pad_ref[r0:r0 + ch, cols]
                for dlt in range(1, win):
                    acc = acc + pad_ref[r0 + dlt:r0 + dlt + ch, cols]
                du_ref[r0:r0 + ch, cols] = acc - dpooled[r0:r0 + ch]

    return pl.pallas_call(
        body, grid=(1,),
        in_specs=[pl.BlockSpec((s, wd), lambda i: (0, 1)), pl.BlockSpec((s, wd), lambda i: (0, 0)),
                  pl.BlockSpec(pool_w.shape, lambda i: (0, 0, 0)), _vec(wd)],
        out_specs=[pl.BlockSpec((s, wd), lambda i: (0, 0)), pl.BlockSpec(pool_w.shape, lambda i: (0, 0, 0)), _vec(wd)],
        out_shape=[_sds((s, wd)), _sds(pool_w.shape), _sds((1, wd))],
        scratch_shapes=[pltpu.VMEM((s + POOL_HALO, wd), f32)], name="pool_bwd",
        compiler_params=_params())(dcat, pooled, pool_w, pool_scale)


def _conv_fwd(hsm, dw, ln_g, ln_b):
    _, s, wd = hsm.shape
    rows, halo = CONV_ROWS, CONV_HALO

    def body(a_ref, g_ref, dw_ref, lg_ref, lb_ref, out_ref, hc_ref, xh_ref, rs_ref, pad_ref):
        hc = a_ref[...] * _sigmoid(g_ref[...])
        hc_ref[...] = hc
        pad_ref[0:halo, :] = jnp.zeros((halo, wd), f32)
        pad_ref[halo:halo + s, :] = hc
        taps = dw_ref[...]

        def chunk(c, _):
            base = pl.multiple_of(c * rows, rows)
            win = pad_ref[pl.ds(base, rows + halo), :]
            y = jnp.zeros((rows, wd), f32)
            for k in range(CONV_TAPS):
                lo = halo - (CONV_TAPS - 1) + k
                y = y + taps[k:k + 1, :] * win[lo:lo + rows]
            mu = jnp.mean(y, axis=-1, keepdims=True)
            cen = y - mu
            rstd = lax.rsqrt(jnp.mean(cen * cen, axis=-1, keepdims=True) + LN_EPS)
            xh = cen * rstd
            n = xh * lg_ref[...] + lb_ref[...]
            out_ref[pl.ds(base, rows), :] = n * _sigmoid(n)
            xh_ref[pl.ds(base, rows), :] = xh
            rs_ref[pl.ds(base, rows), :] = rstd
            return 0

        lax.fori_loop(0, s // rows, chunk, 0)

    full = pl.BlockSpec((s, wd), lambda i: (0, 0))
    return pl.pallas_call(
        body, grid=(1,),
        in_specs=[pl.BlockSpec((None, s, wd), lambda i: (0, 0, 0)), pl.BlockSpec((None, s, wd), lambda i: (1, 0, 0)),
                  pl.BlockSpec(dw.shape, lambda i: (0, 0)), _vec(wd), _vec(wd)],
        out_specs=[full, full, full, pl.BlockSpec((s, 1), lambda i: (0, 0))],
        out_shape=[_sds((s, wd))] * 3 + [_sds((s, 1))],
        scratch_shapes=[pltpu.VMEM((halo + s, wd), f32)], name="conv_fwd",
        compiler_params=_params())(hsm, hsm, dw, ln_g, ln_b)


def _conv_bwd(dcat, hsm, hc, xh, rstd, dw, ln_g, ln_b):
    s, wd = hc.shape
    rows, halo = CONV_ROWS, CONV_HALO

    def body(dc_ref, a_ref, g_ref, hc_ref, xh_ref, rs_ref, dw_ref, lg_ref, lb_ref,
             da_ref, dg_ref, ddw_ref, dlg_ref, dlb_ref, hpad_ref, ypad_ref):
        hpad_ref[0:halo, :] = jnp.zeros((halo, wd), f32)
        hpad_ref[halo:halo + s, :] = hc_ref[...]
        ypad_ref[s:s + halo, :] = jnp.zeros((halo, wd), f32)
        ddw_ref[...] = jnp.zeros_like(ddw_ref)
        dlg_ref[...] = jnp.zeros_like(dlg_ref)
        dlb_ref[...] = jnp.zeros_like(dlb_ref)
        taps = dw_ref[...]

        def norm_bwd(c, _):
            base = pl.multiple_of(c * rows, rows)
            xhv = xh_ref[pl.ds(base, rows), :]
            n = xhv * lg_ref[...] + lb_ref[...]
            sn = _sigmoid(n)
            dn = dc_ref[pl.ds(base, rows), :] * sn * (1.0 + n * (1.0 - sn))
            dlg_ref[...] += jnp.sum(dn * xhv, axis=0, keepdims=True)
            dlb_ref[...] += jnp.sum(dn, axis=0, keepdims=True)
            ypad_ref[pl.ds(base, rows), :] = _ln_bwd_rows(dn, xhv, rs_ref[pl.ds(base, rows), :], lg_ref[...])
            return 0

        lax.fori_loop(0, s // rows, norm_bwd, 0)

        def conv_bwd(c, _):
            base = pl.multiple_of(c * rows, rows)
            ywin = ypad_ref[pl.ds(base, rows + halo), :]
            hwin = hpad_ref[pl.ds(base, rows + halo), :]
            dy = ywin[0:rows]
            dhc = jnp.zeros((rows, wd), f32)
            for k in range(CONV_TAPS):
                fwd = CONV_TAPS - 1 - k
                dhc = dhc + taps[k:k + 1, :] * ywin[fwd:fwd + rows]
                lo = halo - (CONV_TAPS - 1) + k
                ddw_ref[k:k + 1, :] += jnp.sum(dy * hwin[lo:lo + rows], axis=0, keepdims=True)
            sg = _sigmoid(g_ref[pl.ds(base, rows), :])
            da_ref[pl.ds(base, rows), :] = dhc * sg
            dg_ref[pl.ds(base, rows), :] = dhc * a_ref[pl.ds(base, rows), :] * sg * (1.0 - sg)
            return 0

        lax.fori_loop(0, s // rows, conv_bwd, 0)

    full = pl.BlockSpec((s, wd), lambda i: (0, 0))
    tap_spec = pl.BlockSpec(dw.shape, lambda i: (0, 0))
    return pl.pallas_call(
        body, grid=(1,),
        in_specs=[full, pl.BlockSpec((None, s, wd), lambda i: (0, 0, 0)), pl.BlockSpec((None, s, wd), lambda i: (1, 0, 0)),
                  full, full, pl.BlockSpec((s, 1), lambda i: (0, 0)), tap_spec, _vec(wd), _vec(wd)],
        out_specs=[full, full, tap_spec, _vec(wd), _vec(wd)],
        out_shape=[_sds((s, wd)), _sds((s, wd)), _sds(dw.shape), _sds((1, wd)), _sds((1, wd))],
        scratch_shapes=[pltpu.VMEM((halo + s, wd), f32), pltpu.VMEM((s + halo, wd), f32)], name="conv_bwd",
        compiler_params=_params())(dcat, hsm, hsm, hc, xh, rstd, dw, ln_g, ln_b)


_GELU_C = 0.7978845608028654
_GELU_A = 0.044715


def _gelu(x):
    return 0.5 * x * (1.0 + jnp.tanh(_GELU_C * (x + _GELU_A * x * x * x)))


def _gelu_grad(x):
    th = jnp.tanh(_GELU_C * (x + _GELU_A * x * x * x))
    return 0.5 * (1.0 + th) + 0.5 * x * (1.0 - th * th) * _GELU_C * (1.0 + 3.0 * _GELU_A * x * x)


def _causal_sg_w(w_ref, g):
    row = lax.broadcasted_iota(jnp.int32, (SG_CHUNK, SG_CHUNK), 0)
    col = lax.broadcasted_iota(jnp.int32, (SG_CHUNK, SG_CHUNK), 1)
    return jnp.where(col <= row, w_ref[g], 0.0).astype(bf16), col <= row


def _gmlp_fwd(hsm, ln_g, ln_b, sg_w, sg_bt):
    _, s, wd = hsm.shape
    ck = SG_CHUNK

    def body(zu_ref, zv_ref, lg_ref, lb_ref, w_ref, bt_ref, out_ref, xh_ref, rs_ref):
        u = _gelu(zu_ref[...])
        vg = _gelu(zv_ref[...])
        mu = jnp.mean(vg, axis=-1, keepdims=True)
        cen = vg - mu
        rstd = lax.rsqrt(jnp.mean(cen * cen, axis=-1, keepdims=True) + LN_EPS)
        xh = cen * rstd
        xh_ref[...] = xh
        rs_ref[...] = rstd
        vn = (xh * lg_ref[...] + lb_ref[...]).astype(bf16)
        for g in range(4):
            cols = slice(g * GROUP_DIM, (g + 1) * GROUP_DIM)
            wm, _ = _causal_sg_w(w_ref, g)
            sv = _dot(wm, vn[:, cols]) + bt_ref[:, g:g + 1]
            out_ref[:, cols] = u[:, cols] * sv

    rows_spec = pl.BlockSpec((ck, wd), lambda i: (i, 0))
    return pl.pallas_call(
        body, grid=(s // ck,),
        in_specs=[pl.BlockSpec((None, ck, wd), lambda i: (2, i, 0)), pl.BlockSpec((None, ck, wd), lambda i: (3, i, 0)),
                  _vec(wd), _vec(wd), pl.BlockSpec(sg_w.shape, lambda i: (0, 0, 0)),
                  pl.BlockSpec(sg_bt.shape, lambda i: (0, 0))],
        out_specs=[rows_spec, rows_spec, pl.BlockSpec((ck, 1), lambda i: (i, 0))],
        out_shape=[_sds((s, wd)), _sds((s, wd)), _sds((s, 1))], name="gmlp_fwd",
        compiler_params=_params())(hsm, hsm, ln_g, ln_b, sg_w, sg_bt)


def _gmlp_bwd(dcat, hsm, xh, rstd, ln_g, ln_b, sg_w, sg_bt):
    s, wd = xh.shape
    ck = SG_CHUNK

    def body(dd_ref, zu_ref, zv_ref, xh_ref, rs_ref, lg_ref, lb_ref, w_ref, bt_ref,
             dzu_ref, dzv_ref, dw_ref, dbb_ref, dlg_ref, dlb_ref):
        @pl.when(pl.program_id(0) == 0)
        def _():
            dw_ref[...] = jnp.zeros_like(dw_ref)
            dbb_ref[...] = jnp.zeros_like(dbb_ref)
            dlg_ref[...] = jnp.zeros_like(dlg_ref)
            dlb_ref[...] = jnp.zeros_like(dlb_ref)

        zu, zv, dd, xhv = zu_ref[...], zv_ref[...], dd_ref[...], xh_ref[...]
        u = _gelu(zu)
        vn = (xhv * lg_ref[...] + lb_ref[...]).astype(bf16)
        du_parts, dvn_parts = [], []
        for g in range(4):
            cols = slice(g * GROUP_DIM, (g + 1) * GROUP_DIM)
            wm, keep = _causal_sg_w(w_ref, g)
            sv = _dot(wm, vn[:, cols]) + bt_ref[:, g:g + 1]
            du_parts.append(dd[:, cols] * sv)
            dsv = dd[:, cols] * u[:, cols]
            dsvb = dsv.astype(bf16)
            dbb_ref[g] += jnp.broadcast_to(jnp.sum(dsv, axis=1, keepdims=True), (ck, GROUP_DIM))
            dw_ref[g] += jnp.where(keep, _dot(dsvb, vn[:, cols], _NT), 0.0)
            dvn_parts.append(_dot(wm, dsvb, _TN))
        du = jnp.concatenate(du_parts, axis=1)
        dvn = jnp.concatenate(dvn_parts, axis=1)
        dlg_ref[...] += jnp.sum(dvn * xhv, axis=0, keepdims=True)
        dlb_ref[...] += jnp.sum(dvn, axis=0, keepdims=True)
        dzv_ref[...] = _ln_bwd_rows(dvn, xhv, rs_ref[...], lg_ref[...]) * _gelu_grad(zv)
        dzu_ref[...] = du * _gelu_grad(zu)

    rows_spec = pl.BlockSpec((ck, wd), lambda i: (i, 0))
    wspec = pl.BlockSpec(sg_w.shape, lambda i: (0, 0, 0))
    return pl.pallas_call(
        body, grid=(s // ck,),
        in_specs=[pl.BlockSpec((ck, wd), lambda i: (i, 1)), pl.BlockSpec((None, ck, wd), lambda i: (2, i, 0)),
                  pl.BlockSpec((None, ck, wd), lambda i: (3, i, 0)), rows_spec, pl.BlockSpec((ck, 1), lambda i: (i, 0)),
                  _vec(wd), _vec(wd), wspec, pl.BlockSpec(sg_bt.shape, lambda i: (0, 0))],
        out_specs=[rows_spec, rows_spec, wspec, wspec, _vec(wd), _vec(wd)],
        out_shape=[_sds((s, wd)), _sds((s, wd)), _sds(sg_w.shape), _sds(sg_w.shape), _sds((1, wd)), _sds((1, wd))],
        name="gmlp_bwd", compiler_params=_params())(dcat, hsm, hsm, xh, rstd, ln_g, ln_b, sg_w, sg_bt)


def _to_heads(x2d):
    s = x2d.shape[0]
    return jnp.transpose(x2d.reshape(s, N_HEADS, HEAD_DIM), (1, 0, 2)).astype(bf16)


def _from_heads(x3d):
    s = x3d.shape[1]
    return jnp.transpose(x3d, (1, 0, 2)).reshape(s, N_HEADS * HEAD_DIM)


def _local_step(x, p, target, wts, late_shards, small):
    wts = dict(wts)
    saved = []
    for i in range(DEPTH):
        tag = f"l{i}"
        if i % 2 == 0:
            hsm = _mm_nn_col(tag + "_in", x, wts["even_w_in"], 0)
            q, k, v = _to_heads(hsm[0]), _to_heads(hsm[1]), _to_heads(hsm[2])
            att_heads, sba_tot, late = _sba_fwd(q, k, v, late_shards)
            wts.update(zip(_LATE, late))
            att = _from_heads(att_heads)
            pool_out, pooled = _pool_fwd(hsm, small["pool_w"], small["pool_scale"])
            cat = jnp.concatenate([att, pool_out], axis=1).astype(bf16)
            mix = _mm_nn_row(tag + "_out", cat, wts["even_w_out"], 0)
            mixer_saved = (hsm, q, k, v, sba_tot, pooled, cat)
        else:
            hsm = _mm_nn_col(tag + "_in", x, wts["odd_w_in"], 0)
            conv_out, hc, cxh, crs = _conv_fwd(hsm, small["conv_dw"], small["conv_ln_g"], small["conv_ln_b"])
            sg_out, sxh, srs = _gmlp_fwd(hsm, small["sg_ln_g"], small["sg_ln_b"], small["sg_w"], small["sg_bt"])
            cat = jnp.concatenate([conv_out, sg_out], axis=1).astype(bf16)
            mix = _mm_nn_row(tag + "_out", cat, wts["odd_w_out"], 0)
            mixer_saved = (hsm, hc, cxh, crs, sxh, srs, cat)
        x1, xh1, rs1 = _ln_fwd(tag + "_ln_mix", x, mix, small["ln_mix_g"][i:i + 1], small["ln_mix_b"][i:i + 1])
        hg = _mm_nn_col(tag + "_gate", x1, wts["ffn_w_gate"], i)
        hu = _mm_nn_col(tag + "_up", x1, wts["ffn_w_up"], i)
        act = _swiglu_fwd(tag + "_swiglu", hg, hu)
        ffn = _mm_nn_row(tag + "_down", act, wts["ffn_w_down"], i)
        x2, xh2, rs2 = _ln_fwd(tag + "_ln_ffn", x1, ffn, small["ln_ffn_g"][i:i + 1], small["ln_ffn_b"][i:i + 1])
        gp = _mm_nn_row(tag + "_ple_gate", x2, wts["ple_w_gate"], i)
        pp = _mm_nn_col(tag + "_ple_proj", p[i], wts["ple_w_proj"], i, natural=True)
        x3 = _ple_fwd(tag + "_ple", x2, gp, small["ple_b_gate"][i:i + 1], pp)
        saved.append((x, mixer_saved, x1, xh1, rs1, hg, hu, act, x2, xh2, rs2, gp, pp))
        x = x3

    loss_part, dx = _loss_head("loss_head", x, target)

    big = {n: [None] * wts[n].shape[1] for n in wts}
    received = {}
    sm = {}
    per_layer = {n: [None] * DEPTH for n in ("ln_mix_g", "ln_mix_b", "ln_ffn_g", "ln_ffn_b", "ple_b_gate")}
    for i in reversed(range(DEPTH)):
        tag = f"l{i}b"
        x0, mixer_saved, x1, xh1, rs1, hg, hu, act, x2, xh2, rs2, gp, pp = saved[i]
        dgp, dpp, per_layer["ple_b_gate"][i] = _ple_bwd(tag + "_ple", dx, gp, small["ple_b_gate"][i:i + 1], pp)
        big["ple_w_proj"][i] = _mm_tn_col(tag + "_dproj", p[i], dpp)
        big["ple_w_gate"][i] = _mm_tn_row(tag + "_dgate", x2, dgp)
        dx2 = _mm_nt_row(tag + "_dx2", dgp, wts["ple_w_gate"], i, natural=True, add=dx)
        dr2, per_layer["ln_ffn_g"][i], per_layer["ln_ffn_b"][i] = _ln_bwd(tag + "_ln_ffn", dx2, xh2, rs2,
                                                                           small["ln_ffn_g"][i:i + 1])
        dact = _mm_nt_row(tag + "_dact", dr2, wts["ffn_w_down"], i)
        big["ffn_w_down"][i] = _mm_tn_row(tag + "_ddown", act, dr2)
        dhg, dhu = _swiglu_bwd(tag + "_swiglu", dact, hg, hu)
        big["ffn_w_gate"][i] = _mm_tn_col(tag + "_dgatew", x1, dhg)
        big["ffn_w_up"][i] = _mm_tn_col(tag + "_dupw", x1, dhu)
        part = _mm_nt_col(tag + "_dx1a", dhg, wts["ffn_w_gate"], i, dr2, ALPHA)
        dx1 = _mm_nt_col(tag + "_dx1b", dhu, wts["ffn_w_up"], i, part, 1.0)
        dr1, per_layer["ln_mix_g"][i], per_layer["ln_mix_b"][i] = _ln_bwd(tag + "_ln_mix", dx1, xh1, rs1,
                                                                           small["ln_mix_g"][i:i + 1])
        if i % 2 == 0:
            hsm, q, k, v, sba_tot, pooled, cat = mixer_saved
            big["even_w_out"][0] = _mm_tn_row(tag + "_dout", cat, dr1)
            dcat = _mm_nt_row(tag + "_dcat", dr1, wts["even_w_out"], 0, natural=True)
            entries, layout = _scatter_plan(big, _LATE)
            dq, dk, dv, got = _sba_bwd(q, k, v, sba_tot, _to_heads(dcat[:, :N_HEADS * HEAD_DIM]), entries, layout)
            received.update(zip(_LATE, got))
            du, sm["pool_w"], sm["pool_scale"] = _pool_bwd(dcat, pooled, small["pool_w"], small["pool_scale"])
            dhsm = jnp.stack([_from_heads(dq), _from_heads(dk), _from_heads(dv), du]).astype(bf16)
            w_in = "even_w_in"
        else:
            hsm, hc, cxh, crs, sxh, srs, cat = mixer_saved
            big["odd_w_out"][0] = _mm_tn_row(tag + "_dout", cat, dr1)
            dcat = _mm_nt_row(tag + "_dcat", dr1, wts["odd_w_out"], 0, natural=True)
            da, dg, sm["conv_dw"], sm["conv_ln_g"], sm["conv_ln_b"] = _conv_bwd(
                dcat, hsm, hc, cxh, crs, small["conv_dw"], small["conv_ln_g"], small["conv_ln_b"])
            dzu, dzv, sm["sg_w"], dsgb, sm["sg_ln_g"], sm["sg_ln_b"] = _gmlp_bwd(
                dcat, hsm, sxh, srs, small["sg_ln_g"], small["sg_ln_b"], small["sg_w"], small["sg_bt"])
            sm["sg_b"] = dsgb[:, :, 0]
            dhsm = jnp.stack([da, dg, dzu, dzv]).astype(bf16)
            w_in = "odd_w_in"
        big[w_in][0] = _mm_tn_col(tag + "_din", x0, dhsm)
        dx = _mm_nt_col(tag + "_dx", dhsm, wts[w_in], 0, dr1, ALPHA)
    for n, parts in per_layer.items():
        sm[n] = jnp.concatenate(parts, axis=0)
    return loss_part, dx, received, _scatter_plan(big, _EARLY), sm


def _scatter_plan(big, names):
    entries, layout = [], []
    for pi, n in enumerate(names):
        for li, g in enumerate(big[n]):
            entries.append(g)
            layout.append((pi, li))
    return entries, layout


def _place():
    x, y, c = lax.axis_index("x"), lax.axis_index("y"), lax.axis_index("c")
    return x, y, c, [(1 - x, y), (x, 1 - y), (1 - x, 1 - y)]


def _gather_chips(shards):
    n = len(shards)

    def body(*refs):
        copies = _gather_copies(refs[:n], refs[n:2 * n], *refs[2 * n:])
        for cp in copies:
            cp.start()
        for cp in copies:
            cp.wait()

    return pl.pallas_call(body, in_specs=[_ANY] * n, out_specs=[_ANY] * n, out_shape=_gather_shapes(shards),
                          scratch_shapes=_comm_sems(n), name="gather_chips")(*shards)


def _comm_sems(n):
    return [pltpu.SemaphoreType.DMA((n, 3)), pltpu.SemaphoreType.DMA((n, 3)), pltpu.SemaphoreType.DMA((n,))]


def _gather_shapes(shards):
    return [_sds((N_CHIPS,) + a.shape, a.dtype) for a in shards]


def _gather_copies(ins, outs, send, recv, loc):
    x, y, c, chips = _place()
    mine = 2 * x + y
    copies = []
    for t in range(len(ins)):
        copies.append(pltpu.make_async_copy(ins[t], outs[t].at[mine], loc.at[t]))
        for j, (cx, cy) in enumerate(chips):
            copies.append(pltpu.make_async_remote_copy(
                src_ref=ins[t], dst_ref=outs[t].at[mine], send_sem=send.at[t, j], recv_sem=recv.at[t, j],
                device_id=(cx, cy, c), device_id_type=_MESH))
    return copies


def _gather_halves_copies(ins, outs, send, recv, loc, send_on, recv_on):
    x, y, c, chips = _place()
    mine = 2 * x + y
    local, over_ici, onward = [], [], []
    for t in range(len(ins)):
        nl, r, _ = ins[t].shape
        half = r // 2
        rows = pl.ds(pl.multiple_of(c * half, 32), half)
        local.append(pltpu.make_async_copy(ins[t], outs[t].at[mine], loc.at[t]))
        for j, (cx, cy) in enumerate(chips):
            over_ici.append(pltpu.make_async_remote_copy(
                src_ref=ins[t].at[pl.ds(0, nl), rows], dst_ref=outs[t].at[mine, pl.ds(0, nl), rows],
                send_sem=send.at[t, j], recv_sem=recv.at[t, j], device_id=(cx, cy, c), device_id_type=_MESH))
            landed = outs[t].at[2 * cx + cy, pl.ds(0, nl), rows]
            onward.append(pltpu.make_async_remote_copy(
                src_ref=landed, dst_ref=landed, send_sem=send_on.at[t, j], recv_sem=recv_on.at[t, j],
                device_id=(x, y, 1 - c), device_id_type=_MESH))
    return local, over_ici, onward


def _scatter_shapes(grads, layout):
    shapes = {}
    for e, (pi, li) in enumerate(layout):
        r, cdim = grads[e].shape[1:]
        shapes[pi] = (N_CHIPS, max(li + 1, shapes.get(pi, (0, 0))[1]), r, cdim)
    return [_sds(shapes[pi], grads[0].dtype) for pi in range(len(shapes))]


def _scatter_copies(ins, outs, layout, send, recv, loc):
    x, y, c, chips = _place()
    mine = 2 * x + y
    copies = []
    for e, (pi, li) in enumerate(layout):
        copies.append(pltpu.make_async_copy(ins[e].at[mine], outs[pi].at[mine, li], loc.at[e]))
        for j, (cx, cy) in enumerate(chips):
            copies.append(pltpu.make_async_remote_copy(
                src_ref=ins[e].at[2 * cx + cy], dst_ref=outs[pi].at[mine, li], send_sem=send.at[e, j],
                recv_sem=recv.at[e, j], device_id=(cx, cy, c), device_id_type=_MESH))
    return copies


def _final_exchange(grads, layout, block):
    n_in = len(grads)
    out_shape = _scatter_shapes(grads, layout)
    n_out = len(out_shape)

    def body(*refs):
        blk_in, blk_out = refs[n_in], refs[n_in + 1 + n_out]
        send, recv, loc, send_all, recv_all, loc_all = refs[n_in + n_out + 2:]
        copies = _scatter_copies(refs[:n_in], refs[n_in + 1:n_in + 1 + n_out], layout, send, recv, loc)
        x, y, c, _ = _place()
        mine = 4 * x + 2 * y + c
        copies.append(pltpu.make_async_copy(blk_in, blk_out.at[mine], loc_all))
        for m in range(1, N_DEV):
            fx, fy, fc = (m >> 2) & 1, (m >> 1) & 1, m & 1
            peer = (x + fx - 2 * x * fx, y + fy - 2 * y * fy, c + fc - 2 * c * fc)
            copies.append(pltpu.make_async_remote_copy(
                src_ref=blk_in, dst_ref=blk_out.at[mine], send_sem=send_all.at[m - 1], recv_sem=recv_all.at[m - 1],
                device_id=peer, device_id_type=_MESH))
        for cp in copies:
            cp.start()
        for cp in copies:
            cp.wait()

    outs = pl.pallas_call(
        body, in_specs=[_ANY] * (n_in + 1), out_specs=[_ANY] * (n_out + 1),
        out_shape=out_shape + [_sds((N_DEV,) + block.shape)],
        scratch_shapes=_comm_sems(n_in) + [pltpu.SemaphoreType.DMA((N_DEV - 1,)), pltpu.SemaphoreType.DMA((N_DEV - 1,)),
                                           pltpu.SemaphoreType.DMA(())],
        name="final_exchange")(*grads, block)
    return outs[:n_out], outs[n_out]


def _swap_cores(arrays):
    n = len(arrays)

    def body(*refs):
        ins, outs = refs[:n], refs[n:2 * n]
        send, recv = refs[2 * n:]
        x, y, c, _ = _place()
        started = []
        for t in range(n):
            rc = pltpu.make_async_remote_copy(src_ref=ins[t], dst_ref=outs[t], send_sem=send.at[t], recv_sem=recv.at[t],
                                              device_id=(x, y, 1 - c), device_id_type=_MESH)
            rc.start()
            started.append(rc)
        for rc in started:
            rc.wait()

    return pl.pallas_call(
        body, in_specs=[_ANY] * n, out_specs=[_ANY] * n, out_shape=[_sds(a.shape, a.dtype) for a in arrays],
        scratch_shapes=[pltpu.SemaphoreType.DMA((n,)), pltpu.SemaphoreType.DMA((n,))], name="swap_cores")(*arrays)


ROW_TILE_BYTES = 2 * 1024 * 1024


def _row_tile(r, c):
    for t in (1024, 704, 512, 256, 128, 64, 32, 16, 8):
        if r % t == 0 and t * c * 4 <= ROW_TILE_BYTES:
            return t
    return r


def _sum_stack(name, stack):
    n, r, c = stack.shape
    tr = _row_tile(r, c)

    def body(s_ref, o_ref):
        acc = s_ref[0].astype(f32)
        for t in range(1, n):
            acc = acc + s_ref[t].astype(f32)
        o_ref[...] = acc

    return pl.pallas_call(body, grid=(r // tr,), in_specs=[pl.BlockSpec((n, tr, c), lambda i: (0, i, 0))],
                          out_specs=pl.BlockSpec((tr, c), lambda i: (i, 0)), out_shape=_sds((r, c)), name=name,
                          compiler_params=_params())(stack)


def _adamw(name, w, g_a, g_b, m, v):
    r, c = w.shape
    tr = _row_tile(r, c)
    two = g_b is not None
    bc1 = 1.0 - ADAM_B1 ** ADAM_STEP
    bc2 = 1.0 - ADAM_B2 ** ADAM_STEP

    def body(*refs):
        if two:
            w_ref, ga_ref, gb_ref, m_ref, v_ref, g_out, d_out, m_out, v_out = refs
            g = ga_ref[...] + gb_ref[...]
        else:
            w_ref, ga_ref, m_ref, v_ref, g_out, d_out, m_out, v_out = refs
            g = ga_ref[...]
        m_new = ADAM_B1 * m_ref[...] + (1.0 - ADAM_B1) * g
        v_new = ADAM_B2 * v_ref[...] + (1.0 - ADAM_B2) * (g * g)
        g_out[...] = g
        m_out[...] = m_new
        v_out[...] = v_new
        d_out[...] = -ADAM_LR * ((m_new / bc1) / (jnp.sqrt(v_new / bc2) + ADAM_EPS) + ADAM_WD * w_ref[...])

    spec = pl.BlockSpec((tr, c), lambda i: (i, 0))
    ins = [w, g_a] + ([g_b] if two else []) + [m, v]
    return pl.pallas_call(body, grid=(r // tr,), in_specs=[spec] * len(ins), out_specs=[spec] * 4,
                          out_shape=[_sds((r, c))] * 4, name=name, compiler_params=_params())(*ins)


_EARLY = ("even_w_in", "even_w_out")
_LATE = ("odd_w_in", "odd_w_out", "ffn_w_gate", "ffn_w_up", "ffn_w_down", "ple_w_proj", "ple_w_gate")
_BIG = _EARLY + _LATE
_SHARDED_SMALL = ("conv_dw", "conv_ln_g", "conv_ln_b", "sg_ln_g", "sg_ln_b")
_SMALL = ("pool_w", "pool_scale", "conv_dw", "conv_ln_g", "conv_ln_b", "sg_ln_g", "sg_ln_b", "sg_w", "sg_b",
          "ln_mix_g", "ln_mix_b", "ln_ffn_g", "ln_ffn_b", "ple_b_gate")
_WEIGHTS = ("even_w_in", "even_w_out", "pool_w", "pool_scale", "odd_w_in", "odd_w_out", "conv_dw", "conv_ln_g",
            "conv_ln_b", "sg_ln_g", "sg_ln_b", "sg_w", "sg_b", "ln_mix_g", "ln_mix_b", "ffn_w_gate", "ffn_w_up",
            "ffn_w_down", "ln_ffn_g", "ln_ffn_b", "ple_w_proj", "ple_w_gate", "ple_b_gate")


def _pack(arrays):
    flat = jnp.concatenate([a.reshape(-1) for a in arrays])
    pad = (-flat.shape[0]) % (256 * LANES)
    return jnp.pad(flat, (0, pad)).reshape(-1, LANES)


def _unpack(packed, shapes):
    flat = packed.reshape(-1)
    out, off = [], 0
    for shp in shapes:
        size = 1
        for dim in shp:
            size *= dim
        out.append(flat[off:off + size].reshape(shp))
        off += size
    return out


def _unshard_last(g4):
    return jnp.concatenate([g4[k] for k in range(N_CHIPS)], axis=-1)


def kernel(x, p, even_w_in, even_w_out, pool_w, pool_scale, odd_w_in, odd_w_out, conv_dw, conv_ln_g, conv_ln_b, sg_ln_g, sg_ln_b, sg_w, sg_b, ln_mix_g, ln_mix_b, ffn_w_gate, ffn_w_up, ffn_w_down, ln_ffn_g, ln_ffn_b, ple_w_proj, ple_w_gate, ple_b_gate, loss_target, m_even_w_in, m_even_w_out, m_pool_w, m_pool_scale, m_odd_w_in, m_odd_w_out, m_conv_dw, m_conv_ln_g, m_conv_ln_b, m_sg_ln_g, m_sg_ln_b, m_sg_w, m_sg_b, m_ln_mix_g, m_ln_mix_b, m_ffn_w_gate, m_ffn_w_up, m_ffn_w_down, m_ln_ffn_g, m_ln_ffn_b, m_ple_w_proj, m_ple_w_gate, m_ple_b_gate, v_even_w_in, v_even_w_out, v_pool_w, v_pool_scale, v_odd_w_in, v_odd_w_out, v_conv_dw, v_conv_ln_g, v_conv_ln_b, v_sg_ln_g, v_sg_ln_b, v_sg_w, v_sg_b, v_ln_mix_g, v_ln_mix_b, v_ffn_w_gate, v_ffn_w_up, v_ffn_w_down, v_ln_ffn_g, v_ln_ffn_b, v_ple_w_proj, v_ple_w_gate, v_ple_b_gate):
    w = dict(even_w_in=even_w_in, even_w_out=even_w_out, pool_w=pool_w, pool_scale=pool_scale, odd_w_in=odd_w_in,
             odd_w_out=odd_w_out, conv_dw=conv_dw, conv_ln_g=conv_ln_g, conv_ln_b=conv_ln_b, sg_ln_g=sg_ln_g,
             sg_ln_b=sg_ln_b, sg_w=sg_w, sg_b=sg_b, ln_mix_g=ln_mix_g, ln_mix_b=ln_mix_b, ffn_w_gate=ffn_w_gate,
             ffn_w_up=ffn_w_up, ffn_w_down=ffn_w_down, ln_ffn_g=ln_ffn_g, ln_ffn_b=ln_ffn_b, ple_w_proj=ple_w_proj,
             ple_w_gate=ple_w_gate, ple_b_gate=ple_b_gate)
    mom = dict(even_w_in=m_even_w_in, even_w_out=m_even_w_out, pool_w=m_pool_w, pool_scale=m_pool_scale,
               odd_w_in=m_odd_w_in, odd_w_out=m_odd_w_out, conv_dw=m_conv_dw, conv_ln_g=m_conv_ln_g,
               conv_ln_b=m_conv_ln_b, sg_ln_g=m_sg_ln_g, sg_ln_b=m_sg_ln_b, sg_w=m_sg_w, sg_b=m_sg_b,
               ln_mix_g=m_ln_mix_g, ln_mix_b=m_ln_mix_b, ffn_w_gate=m_ffn_w_gate, ffn_w_up=m_ffn_w_up,
               ffn_w_down=m_ffn_w_down, ln_ffn_g=m_ln_ffn_g, ln_ffn_b=m_ln_ffn_b, ple_w_proj=m_ple_w_proj,
               ple_w_gate=m_ple_w_gate, ple_b_gate=m_ple_b_gate)
    var = dict(even_w_in=v_even_w_in, even_w_out=v_even_w_out, pool_w=v_pool_w, pool_scale=v_pool_scale,
               odd_w_in=v_odd_w_in, odd_w_out=v_odd_w_out, conv_dw=v_conv_dw, conv_ln_g=v_conv_ln_g,
               conv_ln_b=v_conv_ln_b, sg_ln_g=v_sg_ln_g, sg_ln_b=v_sg_ln_b, sg_w=v_sg_w, sg_b=v_sg_b,
               ln_mix_g=v_ln_mix_g, ln_mix_b=v_ln_mix_b, ffn_w_gate=v_ffn_w_gate, ffn_w_up=v_ffn_w_up,
               ffn_w_down=v_ffn_w_down, ln_ffn_g=v_ln_ffn_g, ln_ffn_b=v_ln_ffn_b, ple_w_proj=v_ple_w_proj,
               ple_w_gate=v_ple_w_gate, ple_b_gate=v_ple_b_gate)

    gathered = _gather_chips([w[n].astype(bf16) for n in _EARLY] + [w[n] for n in _SHARDED_SMALL])
    wts = dict(zip(_EARLY, gathered[:len(_EARLY)]))
    small = {n: w[n][0] for n in ("pool_w", "sg_w")}
    small.update({n: w[n] for n in ("pool_scale", "ln_mix_g", "ln_mix_b", "ln_ffn_g", "ln_ffn_b", "ple_b_gate")})
    small["sg_bt"] = jnp.transpose(w["sg_b"][0])
    for n, g4 in zip(_SHARDED_SMALL, gathered[len(_EARLY):]):
        small[n] = _unshard_last(g4)[0]
        if n != "conv_dw":
            small[n] = small[n][None]

    loss_part, grad_x, received, last_plan, sm = _local_step(x[0], p[:, 0], loss_target[0], wts,
                                                             [w[n].astype(bf16) for n in _LATE], small)

    sm_shapes = [(1,) + sm[n].shape if n in ("pool_w", "sg_w", "conv_dw", "sg_b") else sm[n].shape for n in _SMALL]
    got, small_stack = _final_exchange(*last_plan, _pack([sm[n] for n in _SMALL] + [loss_part[0, 0:1]]))
    received.update(zip(_EARLY, got))

    chip_sums = []
    for n in _BIG:
        _, nl, r, c = received[n].shape
        chip_sums.append(_sum_stack("sum_" + n, received[n].reshape(N_CHIPS, nl * r, c)))
    other = _swap_cores(chip_sums)
    results = {}
    for n, mine, theirs in zip(_BIG, chip_sums, other):
        shp = w[n].shape
        flat = (shp[0] * shp[1], shp[2])
        outs = _adamw("adamw_" + n, w[n].reshape(flat), mine, theirs, mom[n].reshape(flat), var[n].reshape(flat))
        results[n] = [o.reshape(shp) for o in outs]

    total = _sum_stack("sum_small", small_stack)
    parts = _unpack(total, sm_shapes + [(1,)])
    loss = parts[-1][0]
    chip = 2 * lax.axis_index("x") + lax.axis_index("y")
    g_small = {}
    for n, g in zip(_SMALL, parts[:-1]):
        if n in _SHARDED_SMALL:
            width = w[n].shape[-1]
            g = lax.dynamic_slice_in_dim(g, chip * width, width, axis=g.ndim - 1)
        g_small[n] = g
    shapes = [w[n].shape for n in _SMALL]
    outs = _adamw("adamw_small", _pack([w[n] for n in _SMALL]), _pack([g_small[n] for n in _SMALL]), None,
                  _pack([mom[n] for n in _SMALL]), _pack([var[n] for n in _SMALL]))
    unpacked = [_unpack(o, shapes) for o in outs]
    for idx, n in enumerate(_SMALL):
        results[n] = [u[idx] for u in unpacked]

    return (loss, grad_x[None], *[results[n][0] for n in _WEIGHTS], *[results[n][1] for n in _WEIGHTS],
            *[results[n][2] for n in _WEIGHTS], *[results[n][3] for n in _WEIGHTS])
```
